```python
import jax, jax.numpy as jnp
from jax import lax
import numpy as np

D_MODEL = 1024
BATCH = 8
SEQ = 2048
DEPTH = 2
DEC_BATCH = 32
DEC_SEQ = 8
PAST_LEN = 16384
PAGE_SIZE = 128

N_MIXERS = 2
N_ATT_LAYERS = (DEPTH + 1) // 2
N_DN_LAYERS = DEPTH // 2
EPS = 1e-6

N_HEADS = 8
HEAD_DIM = D_MODEL // N_HEADS
N_KV_HEADS = 2
GROUP = N_HEADS // N_KV_HEADS
ATT_WIDTH = N_HEADS * HEAD_DIM
IDX_HEADS = 8
IDX_DIM = 64
TOPK_MAX = 256
Q_BLOCK = 128
ATT_SIZES = (ATT_WIDTH, N_KV_HEADS * HEAD_DIM, N_KV_HEADS * HEAD_DIM, IDX_HEADS * IDX_DIM, IDX_DIM, IDX_HEADS, ATT_WIDTH)
ATT_IN = 2 * ATT_WIDTH + 2 * N_KV_HEADS * HEAD_DIM + IDX_HEADS * IDX_DIM + IDX_DIM + IDX_HEADS

DN_HEADS = 8
DN_DK = 128
DN_DV = 128
DN_WIDTH = DN_HEADS * DN_DV
CONV_W = 4
CONV_DIM = 2 * DN_HEADS * DN_DK + DN_WIDTH
CHUNK = 64
DN_IN = CONV_DIM + DN_WIDTH + 2 * DN_HEADS

kernel_name = 'dsa_gated_deltanet_hybrid_step'


def _rms_norm(x, gain):
    xf = x.astype(jnp.float32)
    y = xf * lax.rsqrt(jnp.mean(xf * xf, axis=-1, keepdims=True) + EPS)
    return (y * gain.astype(jnp.float32)).astype(x.dtype)


def _l2norm(x):
    xf = x.astype(jnp.float32)
    return xf * lax.rsqrt(jnp.sum(xf * xf, axis=-1, keepdims=True) + EPS)


def _split(p, sizes):
    offs, acc = [], 0
    for s in sizes[:-1]:
        acc += s
        offs.append(acc)
    return jnp.split(p, offs, axis=-1)


def _gather_rows(rows, idx):
    return jax.vmap(lambda r, i: r[i])(rows, idx)


def _gated_out(o, gate, w_out):
    return (o * jax.nn.silu(gate)) @ w_out


def _att_project(xn, w_in, q_gain, k_gain, ik_gain):
    B, T, _ = xn.shape
    q, k, v, iq, ik, iw, gate = _split(xn @ w_in, ATT_SIZES)
    q = _rms_norm(q.reshape(B, T, N_HEADS, HEAD_DIM), q_gain) * (HEAD_DIM ** -0.5)
    k = _rms_norm(k.reshape(B, T, N_KV_HEADS, HEAD_DIM), k_gain)
    v = v.reshape(B, T, N_KV_HEADS, HEAD_DIM)
    iq = iq.reshape(B, T, IDX_HEADS, IDX_DIM)
    ik = _rms_norm(ik, ik_gain)
    iw = iw * (IDX_HEADS ** -0.5 * IDX_DIM ** -0.5)
    return q, k, v, iq, ik, iw, gate


def _indexer_topk(iq, iw, ik, q_pos, topk):
    L = ik.shape[1]
    s = jnp.einsum('bthd,bld->bthl', iq.astype(jnp.float32), ik.astype(jnp.float32))
    score = jnp.einsum('bth,bthl->btl', iw.astype(jnp.float32), jax.nn.relu(s))
    admissible = jnp.arange(L)[None, None, :] <= q_pos[None, :, None]
    score = jnp.where(admissible, score, -jnp.inf)
    _, idx = lax.top_k(score, topk)
    return idx, idx <= q_pos[None, :, None]


def _sparse_attend(q, k_sel, v_sel, valid):
    B, T = q.shape[:2]
    qg = q.reshape(B, T, N_KV_HEADS, GROUP, HEAD_DIM).astype(jnp.float32)
    logits = jnp.einsum('btgrd,btkgd->btgrk', qg, k_sel.astype(jnp.float32))
    logits = jnp.where(valid[:, :, None, None, :], logits, -jnp.inf)
    p = jax.nn.softmax(logits, axis=-1)
    o = jnp.einsum('btgrk,btkgd->btgrd', p, v_sel.astype(jnp.float32))
    return o.reshape(B, T, ATT_WIDTH).astype(q.dtype)


def _att_prompt(q, k, v, iq, iw, ik):
    B, S = q.shape[:2]
    nb = S // Q_BLOCK
    topk = min(TOPK_MAX, S // 4)

    def to_blocks(a):
        return jnp.swapaxes(a.reshape(B, nb, Q_BLOCK, *a.shape[2:]), 0, 1)

    def block(args):
        qb, iqb, iwb, pos = args
        idx, valid = _indexer_topk(iqb, iwb, ik, pos, topk)
        return _sparse_attend(qb, _gather_rows(k, idx), _gather_rows(v, idx), valid)

    pos = jnp.arange(S).reshape(nb, Q_BLOCK)
    out = lax.map(block, (to_blocks(q), to_blocks(iq), to_blocks(iw), pos))
    return jnp.swapaxes(out, 0, 1).reshape(B, S, ATT_WIDTH)


def _att_sample(q, k, v, iq, iw, ik, cache_k, cache_v, cache_idx_k, page_table, layer):
    B, T = q.shape[:2]
    past = page_table.shape[1] * PAGE_SIZE
    ik_past = cache_idx_k[page_table][..., layer, :].reshape(B, past, IDX_DIM)
    ik_all = jnp.concatenate([ik_past.astype(ik.dtype), ik], axis=1)
    q_pos = past + jnp.arange(T)
    idx, valid = _indexer_topk(iq, iw, ik_all, q_pos, min(TOPK_MAX, (past + T) // 4))
    in_past = idx < past
    pidx = jnp.minimum(idx, past - 1)
    phys = jax.vmap(lambda pt, pg: pt[pg])(page_table, pidx // PAGE_SIZE)
    off = pidx % PAGE_SIZE
    nidx = jnp.clip(idx - past, 0, T - 1)

    def select(cache, new):
        old = cache[phys, off][:, :, :, layer]
        return jnp.where(in_past[..., None, None], old.astype(new.dtype), _gather_rows(new, nidx))

    return _sparse_attend(q, select(cache_k, k), select(cache_v, v), valid)


def _short_conv(x, buf, w):
    T = x.shape[1]
    xp = jnp.concatenate([buf.astype(x.dtype), x], axis=1)
    y = sum(xp[:, i:i + T] * w[i] for i in range(CONV_W))
    return jax.nn.silu(y), xp[:, T:]


def _gated_delta_chunked(q, k, v, g, beta, S0):
    B, T, H, _ = q.shape
    n = -(-T // CHUNK)
    pad = n * CHUNK - T

    def blocks(a):
        a = jnp.pad(a, [(0, 0), (0, pad)] + [(0, 0)] * (a.ndim - 2))
        return a.reshape(B, n, CHUNK, H, -1).transpose(1, 0, 3, 2, 4)

    qc, kc, vc = blocks(q), blocks(k), blocks(v)
    gc, bc = blocks(g[..., None])[..., 0], blocks(beta[..., None])[..., 0]
    incl = jnp.tril(jnp.ones((CHUNK, CHUNK), bool))
    strict = jnp.tril(jnp.ones((CHUNK, CHUNK), bool), k=-1)
    eye = jnp.eye(CHUNK, dtype=jnp.float32)

    def step(S, inp):
        qb, kb, vb, gb, bb = inp
        gcum = jnp.cumsum(gb, axis=-1)
        diff = gcum[..., :, None] - gcum[..., None, :]
        decay = jnp.exp(jnp.where(incl, diff, -jnp.inf))
        kk = jnp.einsum('bhid,bhjd->bhij', kb, kb)
        a_mat = bb[..., :, None] * kk * jnp.where(strict, decay, 0.0)
        rhs = bb[..., None] * (vb - jnp.exp(gcum)[..., None] * jnp.einsum('bhcd,bhdv->bhcv', kb, S))
        u = lax.linalg.triangular_solve(eye + a_mat, rhs, left_side=True, lower=True, unit_diagonal=True)
        qk = jnp.einsum('bhid,bhjd->bhij', qb, kb) * decay
        o = jnp.exp(gcum)[..., None] * jnp.einsum('bhcd,bhdv->bhcv', qb, S) + jnp.einsum('bhij,bhjv->bhiv', qk, u)
        glast = gcum[..., -1]
        S_new = jnp.exp(glast)[..., None, None] * S + jnp.einsum(
            'bhcd,bhcv->bhdv', kb * jnp.exp(glast[..., None] - gcum)[..., None], u)
        return S_new, o

    S_fin, o = lax.scan(step, S0, (qc, kc, vc, gc, bc))
    o = o.transpose(1, 0, 3, 2, 4).reshape(B, n * CHUNK, H, -1)[:, :T]
    return o, S_fin


def _dn_branch(xn, S0, conv_buf, w_in, conv_w, A_log, dt_bias, o_gain, w_out):
    B, T, _ = xn.shape
    qkv, z, b, a = _split(xn @ w_in, (CONV_DIM, DN_WIDTH, DN_HEADS, DN_HEADS))
    qkv, new_buf = _short_conv(qkv, conv_buf, conv_w)
    q, k, v = _split(qkv, (DN_HEADS * DN_DK, DN_HEADS * DN_DK, DN_WIDTH))
    q = _l2norm(q.reshape(B, T, DN_HEADS, DN_DK)) * (DN_DK ** -0.5)
    k = _l2norm(k.reshape(B, T, DN_HEADS, DN_DK))
    v = v.reshape(B, T, DN_HEADS, DN_DV).astype(jnp.float32)
    beta = jax.nn.sigmoid(b.astype(jnp.float32))
    g = -jnp.exp(A_log.astype(jnp.float32)) * jax.nn.softplus(a.astype(jnp.float32) + dt_bias.astype(jnp.float32))
    o, S_new = _gated_delta_chunked(q, k, v, g, beta, S0.astype(jnp.float32))
    o = _rms_norm(o, o_gain).astype(xn.dtype).reshape(B, T, DN_WIDTH)
    return _gated_out(o, z, w_out), S_new.astype(S0.dtype), new_buf


def setup_inputs(seed: int = 0) -> dict:
    key = jax.random.key(seed)
    ks = jax.random.split(key, 24)
    n_pages = PAST_LEN // PAGE_SIZE
    n_pool = (5 * DEC_BATCH * n_pages) // 4

    def nrm(k, shape, scale=1.0):
        return scale * jax.random.normal(k, shape, jnp.float32)

    page_table = jax.random.permutation(ks[7], n_pool)[: DEC_BATCH * n_pages].reshape(DEC_BATCH, n_pages).astype(jnp.int32)
    dt = jnp.exp(jax.random.uniform(ks[20], (N_DN_LAYERS, DN_HEADS), jnp.float32, np.log(1e-3), np.log(1e-1)))
    return {
        'x_prompt': nrm(ks[0], (BATCH, SEQ, D_MODEL)),
        'x_sample': nrm(ks[1], (DEC_BATCH, DEC_SEQ, D_MODEL)),
        'cache_k': nrm(ks[2], (n_pool, PAGE_SIZE, N_ATT_LAYERS, N_KV_HEADS, HEAD_DIM)),
        'cache_v': nrm(ks[3], (n_pool, PAGE_SIZE, N_ATT_LAYERS, N_KV_HEADS, HEAD_DIM)),
        'cache_idx_k': nrm(ks[4], (n_pool, PAGE_SIZE, N_ATT_LAYERS, IDX_DIM)),
        'state_dn_S': nrm(ks[5], (DEC_BATCH, N_DN_LAYERS, DN_HEADS, DN_DK, DN_DV), 0.1),
        'state_dn_conv': nrm(ks[6], (DEC_BATCH, N_DN_LAYERS, CONV_W - 1, CONV_DIM)),
        'page_table': page_table,
        'att_norm': 1.0 + nrm(ks[8], (N_ATT_LAYERS, D_MODEL), 0.05),
        'att_w_in': nrm(ks[9], (N_ATT_LAYERS, D_MODEL, ATT_IN), D_MODEL ** -0.5),
        'att_q_gain': 1.0 + nrm(ks[10], (N_ATT_LAYERS, HEAD_DIM), 0.05),
        'att_k_gain': 1.0 + nrm(ks[11], (N_ATT_LAYERS, HEAD_DIM), 0.05),
        'att_ik_gain': 1.0 + nrm(ks[12], (N_ATT_LAYERS, IDX_DIM), 0.05),
        'att_w_out': nrm(ks[13], (N_ATT_LAYERS, ATT_WIDTH, D_MODEL), ATT_WIDTH ** -0.5),
        'dn_norm': 1.0 + nrm(ks[14], (N_DN_LAYERS, D_MODEL), 0.05),
        'dn_w_in': nrm(ks[15], (N_DN_LAYERS, D_MODEL, DN_IN), D_MODEL ** -0.5),
        'dn_conv_w': nrm(ks[16], (N_DN_LAYERS, CONV_W, CONV_DIM), CONV_W ** -0.5),
        'dn_A_log': jnp.log(jax.random.uniform(ks[17], (N_DN_LAYERS, DN_HEADS), jnp.float32, 1.0, 16.0)),
        'dn_dt_bias': dt + jnp.log(-jnp.expm1(-dt)),
        'dn_o_gain': 1.0 + nrm(ks[18], (N_DN_LAYERS, DN_DV), 0.05),
        'dn_w_out': nrm(ks[19], (N_DN_LAYERS, DN_WIDTH, D_MODEL), DN_WIDTH ** -0.5),
    }


def reference(x_prompt, x_sample, cache_k, cache_v, cache_idx_k, state_dn_S, state_dn_conv, page_table,
              att_norm, att_w_in, att_q_gain, att_k_gain, att_ik_gain, att_w_out,
              dn_norm, dn_w_in, dn_conv_w, dn_A_log, dn_dt_bias, dn_o_gain, dn_w_out):
    y_p, y_s = x_prompt, x_sample
    kp, vp, ikp, ks_, vs_, iks = [], [], [], [], [], []
    Sp, cp, Ss, cs = [], [], [], []
    for i in range(DEPTH):
        l = i // N_MIXERS
        if i % N_MIXERS == 0:
            xn = _rms_norm(y_p, att_norm[l])
            q, k, v, iq, ik, iw, gate = _att_project(xn, att_w_in[l], att_q_gain[l], att_k_gain[l], att_ik_gain[l])
            y_p = y_p + _gated_out(_att_prompt(q, k, v, iq, iw, ik), gate, att_w_out[l])
            kp.append(k); vp.append(v); ikp.append(ik)
            xn = _rms_norm(y_s, att_norm[l])
            q, k, v, iq, ik, iw, gate = _att_project(xn, att_w_in[l], att_q_gain[l], att_k_gain[l], att_ik_gain[l])
            o = _att_sample(q, k, v, iq, iw, ik, cache_k, cache_v, cache_idx_k, page_table, l)
            y_s = y_s + _gated_out(o, gate, att_w_out[l])
            ks_.append(k); vs_.append(v); iks.append(ik)
        else:
            xn = _rms_norm(y_p, dn_norm[l])
            B = xn.shape[0]
            S0 = jnp.zeros((B, DN_HEADS, DN_DK, DN_DV), xn.dtype)
            buf0 = jnp.zeros((B, CONV_W - 1, CONV_DIM), xn.dtype)
            out, S_new, buf_new = _dn_branch(xn, S0, buf0, dn_w_in[l], dn_conv_w[l], dn_A_log[l], dn_dt_bias[l], dn_o_gain[l], dn_w_out[l])
            y_p = y_p + out
            Sp.append(S_new); cp.append(buf_new)
            xn = _rms_norm(y_s, dn_norm[l])
            out, S_new, buf_new = _dn_branch(xn, state_dn_S[:, l], state_dn_conv[:, l], dn_w_in[l], dn_conv_w[l], dn_A_log[l], dn_dt_bias[l], dn_o_gain[l], dn_w_out[l])
            y_s = y_s + out
            Ss.append(S_new); cs.append(buf_new)
    k_prompt = jnp.stack(kp, axis=2)
    v_prompt = jnp.stack(vp, axis=2)
    idx_k_prompt = jnp.stack(ikp, axis=2)
    k_sample = jnp.stack(ks_, axis=2)
    v_sample = jnp.stack(vs_, axis=2)
    idx_k_sample = jnp.stack(iks, axis=2)
    dn_S_prompt = jnp.stack(Sp, axis=1)
    dn_conv_prompt = jnp.stack(cp, axis=1)
    dn_S_sample = jnp.stack(Ss, axis=1)
    dn_conv_sample = jnp.stack(cs, axis=1)
    return (y_p, y_s, k_prompt, v_prompt, idx_k_prompt, k_sample, v_sample, idx_k_sample,
            dn_S_prompt, dn_conv_prompt, dn_S_sample, dn_conv_sample)
```

```python
import functools

import jax
import jax.numpy as jnp
from jax import lax
from jax.experimental import pallas as pl
from jax.experimental.pallas import tpu as pltpu

F32 = jnp.float32
BF16 = jnp.bfloat16
I32 = jnp.int32

EPS = 1e-6
LANES = 128
SUBLANES = 8
VMEM_LIMIT = 56 * 1024 * 1024

D_MODEL = 1024
N_HEADS = 8
HEAD_DIM = 128
N_KV_HEADS = 2
GROUP = N_HEADS // N_KV_HEADS
IDX_HEADS = 8
IDX_DIM = 64
TOPK_MAX = 256
PAGE_SIZE = 128
DN_HEADS = 8
DN_DK = 128
DN_DV = 128
CONV_W = 4
KV_W = N_KV_HEADS * HEAD_DIM
CONV_DIM = 3 * DN_HEADS * DN_DK

A_Q, A_K, A_V, A_IQ, A_IKW, A_GATE, A_END = 0, 1024, 1280, 1536, 2048, 2176, 3200
N_Z, N_BA, N_END = 3072, 4096, 4224

INT_MIN = -2147483648
INT_MAX = 2147483647
NEG_BIG = -1e30
NT_DIMS = (((1,), (1,)), ((), ()))
HI = lax.Precision.HIGHEST


def _dot(a, b, precision=None):
    return jnp.dot(a, b, preferred_element_type=F32, precision=precision)


def _dot_nt(a, b, precision=None):
    return lax.dot_general(a, b, NT_DIMS, preferred_element_type=F32, precision=precision)


def _rms(x, n):
    return x * lax.rsqrt(jnp.sum(x * x, axis=-1, keepdims=True) * (1.0 / n) + EPS)


def _ordinal_to_float(u):
    key = u ^ jnp.int32(INT_MIN)
    return pltpu.bitcast(jnp.where(key < 0, key ^ jnp.int32(INT_MAX), key), F32)


def _att_proj_kernel(x_ref, g_ref, w_ref, qg_ref, kg_ref, ikg_ref,
                     q_ref, k_ref, kb_ref, v_ref, vt_ref, iq_ref, ikw_ref, ikb_ref, iwt_ref, sg_ref):
    x = x_ref[...]
    xn = _rms(x, D_MODEL) * g_ref[...]
    p = _dot(xn.astype(BF16), w_ref[...])
    for h in range(N_HEADS):
        qh = _rms(p[:, A_Q + h * HEAD_DIM:A_Q + (h + 1) * HEAD_DIM], HEAD_DIM) * qg_ref[...]
        q_ref[:, h * HEAD_DIM:(h + 1) * HEAD_DIM] = (qh * (HEAD_DIM ** -0.5)).astype(BF16)
    for g in range(N_KV_HEADS):
        kh = _rms(p[:, A_K + g * HEAD_DIM:A_K + (g + 1) * HEAD_DIM], HEAD_DIM) * kg_ref[...]
        k_ref[:, g * HEAD_DIM:(g + 1) * HEAD_DIM] = kh
        kb_ref[:, g * HEAD_DIM:(g + 1) * HEAD_DIM] = kh.astype(BF16)
    v = p[:, A_V:A_V + KV_W]
    v_ref[...] = v
    vt_ref[...] = v.T.astype(BF16)
    lane = lax.broadcasted_iota(I32, (1, LANES), 1)
    is_ik = lane < IDX_DIM
    for h in range(IDX_HEADS):
        iqh = p[:, A_IQ + (h // 2) * LANES:A_IQ + (h // 2 + 1) * LANES]
        if h % 2:
            iqh = pltpu.roll(iqh, IDX_DIM, axis=1)
        iq_ref[h] = jnp.where(is_ik, iqh, 0.0).astype(BF16)
    ikw = p[:, A_IKW:A_IKW + LANES]
    ik = jnp.where(is_ik, ikw, 0.0)
    ikn = _rms(ik, IDX_DIM) * ikg_ref[...]
    out = jnp.where(is_ik, ikn, ikw * (IDX_HEADS ** -0.5 * IDX_DIM ** -0.5))
    ikw_ref[...] = out
    ikb_ref[...] = ikn.astype(BF16)
    iwt_ref[...] = out.T[IDX_DIM:IDX_DIM + IDX_HEADS, :]
    gate = p[:, A_GATE:A_END]
    sg_ref[...] = (gate * jax.nn.sigmoid(gate)).astype(BF16)


def _att_proj(x, norm_g, w_packed, q_gain, k_gain, ik_gain_pad, tm):
    n = x.shape[0]
    assert n % tm == 0 and tm % LANES == 0
    row = lambda i: (i, 0)
    const = lambda i: (0, 0)
    out_shape = (
        jax.ShapeDtypeStruct((n, D_MODEL), BF16),
        jax.ShapeDtypeStruct((n, KV_W), F32),
        jax.ShapeDtypeStruct((n, KV_W), BF16),
        jax.ShapeDtypeStruct((n, KV_W), F32),
        jax.ShapeDtypeStruct((KV_W, n), BF16),
        jax.ShapeDtypeStruct((IDX_HEADS, n, LANES), BF16),
        jax.ShapeDtypeStruct((n, LANES), F32),
        jax.ShapeDtypeStruct((n, LANES), BF16),
        jax.ShapeDtypeStruct((IDX_HEADS, n), F32),
        jax.ShapeDtypeStruct((n, D_MODEL), BF16),
    )
    out_specs = (
        pl.BlockSpec((tm, D_MODEL), row),
        pl.BlockSpec((tm, KV_W), row),
        pl.BlockSpec((tm, KV_W), row),
        pl.BlockSpec((tm, KV_W), row),
        pl.BlockSpec((KV_W, tm), lambda i: (0, i)),
        pl.BlockSpec((IDX_HEADS, tm, LANES), lambda i: (0, i, 0)),
        pl.BlockSpec((tm, LANES), row),
        pl.BlockSpec((tm, LANES), row),
        pl.BlockSpec((IDX_HEADS, tm), lambda i: (0, i)),
        pl.BlockSpec((tm, D_MODEL), row),
    )
    return pl.pallas_call(
        _att_proj_kernel,
        grid=(n // tm,),
        in_specs=[
            pl.BlockSpec((tm, D_MODEL), row),
            pl.BlockSpec((1, D_MODEL), const),
            pl.BlockSpec((D_MODEL, A_END), const),
            pl.BlockSpec((1, HEAD_DIM), const),
            pl.BlockSpec((1, HEAD_DIM), const),
            pl.BlockSpec((1, LANES), const),
        ],
        out_specs=out_specs,
        out_shape=out_shape,
        compiler_params=pltpu.CompilerParams(dimension_semantics=("parallel",), vmem_limit_bytes=VMEM_LIMIT),
        name="att_proj",
    )(x, norm_g, w_packed, q_gain, k_gain, ik_gain_pad)


def _count(mask, axis):
    return jnp.sum(jnp.where(mask, 1.0, 0.0), axis=axis, keepdims=True)


def _select_bias(sc_ref, lo_ref, n_keys, key_idx, topk, axis):
    sl = (slice(0, n_keys), slice(None)) if axis == 0 else (slice(None), slice(0, n_keys))
    vec = (1, sc_ref.shape[1]) if axis == 0 else (sc_ref.shape[0], 1)
    kf = float(topk)

    def bit_step(i, prefix):
        cand = prefix | (jnp.int32(1) << (31 - i))
        cnt = _count(sc_ref[sl] >= _ordinal_to_float(cand), axis)
        return jnp.where(cnt >= kf, cand, prefix)

    prefix = lax.fori_loop(0, 32, bit_step, jnp.zeros(vec, I32))
    thr = _ordinal_to_float(prefix)
    sc = sc_ref[sl]
    few = jnp.logical_not(_count(sc >= thr, axis) >= kf)
    thr = jnp.where(few, -jnp.inf, thr)
    need = kf - _count(sc > thr, axis)
    excess = (_count(sc >= thr, axis) > kf) & jnp.logical_not(few)
    lo_ref[...] = jnp.full(vec, INT_MAX, I32)

    @pl.when(jnp.max(jnp.where(excess, 1.0, 0.0)) > 0.0)
    def _():
        nbits = max(1, (n_keys - 1).bit_length())

        def idx_step(i, lo):
            cand = lo + (jnp.int32(1) << (nbits - 1 - i))
            cnt = _count((sc_ref[sl] == thr) & (key_idx < cand), axis)
            return jnp.where(cnt < need, cand, lo)

        lo_ref[...] = lax.fori_loop(0, nbits, idx_step, jnp.zeros(vec, I32))

    lo = jnp.where(few, -1, lo_ref[...])
    sel = (sc > thr) | ((sc == thr) & (key_idx <= lo))
    return jnp.where(sel, 0.0, NEG_BIG)


def _att_prompt_body(n_keys, topk, q_ref, kb_ref, vt_ref, ikb_ref, iq_ref, iwt_ref, o_ref,
                     sc_ref, bias_ref, lo_ref):
    qb = q_ref.shape[0]
    t0 = pl.program_id(1) * qb
    ik = ikb_ref[0:n_keys, :]
    iwt = iwt_ref[...]
    score = jnp.zeros((n_keys, qb), F32)
    for h in range(IDX_HEADS):
        s = _dot_nt(ik, iq_ref[h])
        score = score + jnp.maximum(s, 0.0) * iwt[h:h + 1, :]
    key_idx = lax.broadcasted_iota(I32, (n_keys, qb), 0)
    q_pos = t0 + lax.broadcasted_iota(I32, (n_keys, qb), 1)
    sc_ref[0:n_keys, :] = jnp.where(key_idx <= q_pos, score, -jnp.inf)
    bias_ref[0:n_keys, :] = _select_bias(sc_ref, lo_ref, n_keys, key_idx, topk, axis=0)
    for h in range(N_HEADS):
        g = h // GROUP
        s = _dot_nt(kb_ref[0:n_keys, g * HEAD_DIM:(g + 1) * HEAD_DIM],
                    q_ref[:, h * HEAD_DIM:(h + 1) * HEAD_DIM])
        s = s + bias_ref[0:n_keys, :]
        m = jnp.max(s, axis=0, keepdims=True)
        p = jnp.exp(s - m)
        l = jnp.sum(p, axis=0, keepdims=True)
        ot = _dot(vt_ref[g * HEAD_DIM:(g + 1) * HEAD_DIM, 0:n_keys], p.astype(BF16))
        o_ref[:, h * HEAD_DIM:(h + 1) * HEAD_DIM] = (ot / l).T


def _att_prompt_kernel(q_ref, kb_ref, vt_ref, ikb_ref, iq_ref, iwt_ref, o_ref,
                       sc_ref, bias_ref, lo_ref, *, cls_len, n_cls, topk):
    qb = q_ref.shape[0]
    cls = (pl.program_id(1) * qb) // cls_len
    for c in range(n_cls):
        pl.when(cls == c)(functools.partial(
            _att_prompt_body, cls_len * (c + 1), topk, q_ref, kb_ref, vt_ref, ikb_ref, iq_ref, iwt_ref,
            o_ref, sc_ref, bias_ref, lo_ref))


def _att_prompt(q, kb, vt, ikb, iq, iwt, batch, seq, qb=LANES):
    nb = seq // qb
    n_cls = min(4, nb)
    assert seq % (n_cls * qb) == 0
    topk = min(TOPK_MAX, seq // 4)
    kern = functools.partial(_att_prompt_kernel, cls_len=seq // n_cls, n_cls=n_cls, topk=topk)
    return pl.pallas_call(
        kern,
        grid=(batch, nb),
        in_specs=[
            pl.BlockSpec((qb, D_MODEL), lambda b, j: (b * nb + j, 0)),
            pl.BlockSpec((seq, KV_W), lambda b, j: (b, 0)),
            pl.BlockSpec((KV_W, seq), lambda b, j: (0, b)),
            pl.BlockSpec((seq, LANES), lambda b, j: (b, 0)),
            pl.BlockSpec((IDX_HEADS, qb, LANES), lambda b, j: (0, b * nb + j, 0)),
            pl.BlockSpec((IDX_HEADS, qb), lambda b, j: (0, b * nb + j)),
        ],
        out_specs=pl.BlockSpec((qb, D_MODEL), lambda b, j: (b * nb + j, 0)),
        out_shape=jax.ShapeDtypeStruct((batch * seq, D_MODEL), F32),
        scratch_shapes=[
            pltpu.VMEM((seq, qb), F32),
            pltpu.VMEM((seq, qb), F32),
            pltpu.VMEM((1, qb), I32),
        ],
        compiler_params=pltpu.CompilerParams(dimension_semantics=("parallel", "arbitrary"),
                                             vmem_limit_bytes=VMEM_LIMIT),
        name="att_prompt",
    )(q, kb, vt, ikb, iq, iwt)


def _att_sample_kernel(pt_ref, q_ref, iq_ref, ikw_ref, ikn_ref, kn_ref, vn_ref,
                       cik_hbm, ck_hbm, cv_hbm, o_ref,
                       ikbuf, kbuf, vbuf, sc_ref, bias_ref, lo_ref, sem_i, sem_kv,
                       *, n_pages, chunk_pages, topk):
    b = pl.program_id(0)
    t_new = q_ref.shape[0]
    past = n_pages * PAGE_SIZE
    n_keys = past + PAGE_SIZE
    n_chunks = n_pages // chunk_pages
    ck = chunk_pages * PAGE_SIZE

    def idx_copy(p):
        return pltpu.make_async_copy(cik_hbm.at[pt_ref[b, p]], ikbuf.at[pl.ds(p * PAGE_SIZE, PAGE_SIZE)], sem_i)

    def kv_copies(c, p):
        page = pt_ref[b, c * chunk_pages + p]
        dst = pl.ds(p * PAGE_SIZE, PAGE_SIZE)
        return (pltpu.make_async_copy(ck_hbm.at[page], kbuf.at[c % 2, dst], sem_kv.at[0, c % 2]),
                pltpu.make_async_copy(cv_hbm.at[page], vbuf.at[c % 2, dst], sem_kv.at[1, c % 2]))

    def start_chunk(c):
        def go(p, carry):
            for cp in kv_copies(c, p):
                cp.start()
            return carry
        lax.fori_loop(0, chunk_pages, go, 0)

    def wait_chunk(c):
        def go(p, carry):
            for cp in kv_copies(c, p):
                cp.wait()
            return carry
        lax.fori_loop(0, chunk_pages, go, 0)

    def idx_start(p, carry):
        idx_copy(p).start()
        return carry

    def idx_wait(p, carry):
        idx_copy(p).wait()
        return carry

    lax.fori_loop(0, n_pages, idx_start, 0)
    start_chunk(0)
    lax.fori_loop(0, n_pages, idx_wait, 0)

    iq2 = iq_ref[...].reshape(IDX_HEADS * t_new, LANES).astype(BF16)
    ikw = ikw_ref[...]
    lane1 = lax.broadcasted_iota(I32, (1, LANES), 1)

    def scores(ik_rows):
        s = _dot_nt(iq2, ik_rows)
        acc = jnp.zeros((t_new, ik_rows.shape[0]), F32)
        for h in range(IDX_HEADS):
            acc = acc + jnp.maximum(s[h * t_new:(h + 1) * t_new, :], 0.0) * ikw[:, IDX_DIM + h:IDX_DIM + h + 1]
        return acc

    iq2_past = iq2[:, 0:IDX_DIM]
    for c in range(n_chunks):
        ikc = ikbuf[c * ck:(c + 1) * ck, :].astype(BF16)
        s = _dot_nt(iq2_past, ikc)
        acc = jnp.zeros((t_new, ck), F32)
        for h in range(IDX_HEADS):
            acc = acc + jnp.maximum(s[h * t_new:(h + 1) * t_new, :], 0.0) * ikw[:, IDX_DIM + h:IDX_DIM + h + 1]
        sc_ref[:, c * ck:(c + 1) * ck] = acc
    new_idx = lax.broadcasted_iota(I32, (t_new, PAGE_SIZE), 1)
    tok = lax.broadcasted_iota(I32, (t_new, PAGE_SIZE), 0)
    sc_ref[:, past:n_keys] = jnp.where(new_idx <= tok, scores(ikn_ref[...]), -jnp.inf)
    key_idx = lax.broadcasted_iota(I32, (t_new, n_keys), 1)
    bias_ref[...] = _select_bias(sc_ref, lo_ref, n_keys, key_idx, topk, axis=1)

    rows = GROUP * t_new
    qs = [jnp.concatenate([q_ref[:, (g * GROUP + r) * HEAD_DIM:(g * GROUP + r + 1) * HEAD_DIM]
                           for r in range(GROUP)], axis=0).astype(BF16) for g in range(N_KV_HEADS)]
    m = [jnp.full((rows, 1), NEG_BIG, F32) for _ in range(N_KV_HEADS)]
    l = [jnp.zeros((rows, 1), F32) for _ in range(N_KV_HEADS)]
    acc = [jnp.zeros((rows, HEAD_DIM), F32) for _ in range(N_KV_HEADS)]

    def attend(g, k_rows, v_rows, bias):
        s = _dot_nt(qs[g], k_rows) + jnp.concatenate([bias] * GROUP, axis=0)
        m_new = jnp.maximum(m[g], jnp.max(s, axis=1, keepdims=True))
        alpha = jnp.exp(m[g] - m_new)
        p = jnp.exp(s - m_new)
        l[g] = alpha * l[g] + jnp.sum(p, axis=1, keepdims=True)
        acc[g] = alpha * acc[g] + _dot(p.astype(BF16), v_rows)
        m[g] = m_new

    for g in range(N_KV_HEADS):
        hd = slice(g * HEAD_DIM, (g + 1) * HEAD_DIM)
        attend(g, kn_ref[:, hd].astype(BF16), vn_ref[:, hd].astype(BF16), bias_ref[:, past:n_keys])
    for c in range(n_chunks):
        if c + 1 < n_chunks:
            start_chunk(c + 1)
        wait_chunk(c)
        for g in range(N_KV_HEADS):
            hd = slice(g * HEAD_DIM, (g + 1) * HEAD_DIM)
            attend(g, kbuf[c % 2, :, hd].astype(BF16), vbuf[c % 2, :, hd].astype(BF16),
                   bias_ref[:, c * ck:(c + 1) * ck])
    for g in range(N_KV_HEADS):
        og = acc[g] / l[g]
        for r in range(GROUP):
            h = g * GROUP + r
            o_ref[:, h * HEAD_DIM:(h + 1) * HEAD_DIM] = og[r * t_new:(r + 1) * t_new, :]


def _att_sample(page_table, q, iq, ikw, ikn_pad, kn_pad, vn_pad, cache_ik, cache_k, cache_v, t_new):
    nseq, n_pages = page_table.shape
    assert t_new == SUBLANES
    chunk_pages = min(32, n_pages)
    assert n_pages % chunk_pages == 0
    past = n_pages * PAGE_SIZE
    n_keys = past + PAGE_SIZE
    topk = min(TOPK_MAX, (past + t_new) // 4)
    ck = chunk_pages * PAGE_SIZE
    kern = functools.partial(_att_sample_kernel, n_pages=n_pages, chunk_pages=chunk_pages, topk=topk)
    grid_spec = pltpu.PrefetchScalarGridSpec(
        num_scalar_prefetch=1,
        grid=(nseq,),
        in_specs=[
            pl.BlockSpec((t_new, D_MODEL), lambda b, pt: (b, 0)),
            pl.BlockSpec((IDX_HEADS, t_new, LANES), lambda b, pt: (0, b, 0)),
            pl.BlockSpec((t_new, LANES), lambda b, pt: (b, 0)),
            pl.BlockSpec((None, PAGE_SIZE, LANES), lambda b, pt: (b, 0, 0)),
            pl.BlockSpec((None, PAGE_SIZE, KV_W), lambda b, pt: (b, 0, 0)),
            pl.BlockSpec((None, PAGE_SIZE, KV_W), lambda b, pt: (b, 0, 0)),
            pl.BlockSpec(memory_space=pl.ANY),
            pl.BlockSpec(memory_space=pl.ANY),
            pl.BlockSpec(memory_space=pl.ANY),
        ],
        out_specs=pl.BlockSpec((t_new, D_MODEL), lambda b, pt: (b, 0)),
        scratch_shapes=[
            pltpu.VMEM((past, IDX_DIM), F32),
            pltpu.VMEM((2, ck, KV_W), F32),
            pltpu.VMEM((2, ck, KV_W), F32),
            pltpu.VMEM((t_new, n_keys), F32),
            pltpu.VMEM((t_new, n_keys), F32),
            pltpu.VMEM((t_new, 1), I32),
            pltpu.SemaphoreType.DMA(()),
            pltpu.SemaphoreType.DMA((2, 2)),
        ],
    )
    return pl.pallas_call(
        kern,
        grid_spec=grid_spec,
        out_shape=jax.ShapeDtypeStruct((nseq * t_new, D_MODEL), F32),
        compiler_params=pltpu.CompilerParams(dimension_semantics=("arbitrary",), vmem_limit_bytes=VMEM_LIMIT),
        name="att_sample",
    )(page_table, q, iq, ikw, ikn_pad, kn_pad, vn_pad, cache_ik, cache_k, cache_v)


def _out_proj_kernel(o_ref, sg_ref, gain_ref, w_ref, x_ref, y_ref, *, head_norm):
    o = o_ref[...]
    if head_norm:
        o = jnp.concatenate(
            [_rms(o[:, h * DN_DV:(h + 1) * DN_DV], DN_DV) * gain_ref[...] for h in range(DN_HEADS)], axis=1)
    a = (o * sg_ref[...].astype(F32)).astype(BF16)
    y_ref[...] = x_ref[...] + _dot(a, w_ref[...])


def _out_proj(o, sg, gain, w_out, x, tm, head_norm):
    n = o.shape[0]
    row = lambda i: (i, 0)
    const = lambda i: (0, 0)
    return pl.pallas_call(
        functools.partial(_out_proj_kernel, head_norm=head_norm),
        grid=(n // tm,),
        in_specs=[
            pl.BlockSpec((tm, D_MODEL), row),
            pl.BlockSpec((tm, D_MODEL), row),
            pl.BlockSpec((1, DN_DV), const),
            pl.BlockSpec((D_MODEL, D_MODEL), const),
            pl.BlockSpec((tm, D_MODEL), row),
        ],
        out_specs=pl.BlockSpec((tm, D_MODEL), row),
        out_shape=jax.ShapeDtypeStruct((n, D_MODEL), F32),
        compiler_params=pltpu.CompilerParams(dimension_semantics=("parallel",), vmem_limit_bytes=VMEM_LIMIT),
        name="out_proj",
    )(o, sg, gain, w_out, x)


def _dn_proj_kernel(x_ref, g_ref, w_ref, cw_ref, buf_ref, alog_ref, dtb_ref,
                    q_ref, k_ref, v_ref, sz_ref, gb_ref, cbuf_ref, xp_ref, *, n_seq, pad):
    tm = x_ref.shape[0]

    @pl.when(pl.program_id(1) == 0)
    def _():
        xp_ref[0:pad, :] = buf_ref[...]

    xn = _rms(x_ref[...], D_MODEL) * g_ref[...]
    p = _dot(xn.astype(BF16), w_ref[...])
    xp_ref[pad:pad + tm, :] = p[:, 0:CONV_DIM]
    conv = xp_ref[pad:pad + tm, :] * cw_ref[CONV_W - 1:CONV_W, :]
    for i in range(1, CONV_W):
        conv = conv + xp_ref[pad - i * n_seq:pad - i * n_seq + tm, :] * cw_ref[CONV_W - 1 - i:CONV_W - i, :]
    act = conv * jax.nn.sigmoid(conv)
    for h in range(DN_HEADS):
        sl = slice(h * DN_DK, (h + 1) * DN_DK)
        qh = act[:, sl]
        q_ref[:, sl] = qh * lax.rsqrt(jnp.sum(qh * qh, axis=-1, keepdims=True) + EPS) * (DN_DK ** -0.5)
        kh = act[:, DN_HEADS * DN_DK + h * DN_DK:DN_HEADS * DN_DK + (h + 1) * DN_DK]
        k_ref[:, sl] = kh * lax.rsqrt(jnp.sum(kh * kh, axis=-1, keepdims=True) + EPS)
    v_ref[...] = act[:, 2 * DN_HEADS * DN_DK:CONV_DIM]
    z = p[:, N_Z:N_BA]
    sz_ref[...] = (z * jax.nn.sigmoid(z)).astype(BF16)
    ba = p[:, N_BA:N_END]
    sp_in = ba + dtb_ref[...]
    softplus = jnp.maximum(sp_in, 0.0) + jnp.log1p(jnp.exp(-jnp.abs(sp_in)))
    lane = lax.broadcasted_iota(I32, (1, LANES), 1)
    gb_ref[...] = jnp.where(lane < DN_HEADS, jax.nn.sigmoid(ba), -jnp.exp(alog_ref[...]) * softplus)
    cbuf_ref[...] = xp_ref[tm:tm + pad, :]
    xp_ref[0:pad, :] = xp_ref[tm:tm + pad, :]


def _dn_proj(x, norm_g, w_packed, conv_w, buf, alog_pad, dtb_pad, n_groups, n_seq, tm):
    n = x.shape[0]
    pad = buf.shape[1]
    tiles = n // n_groups // tm
    assert n == n_groups * tiles * tm and pad % SUBLANES == 0 and pad >= (CONV_W - 1) * n_seq
    row = lambda s, i: (s * tiles + i, 0)
    const = lambda s, i: (0, 0)
    big = lambda d: pl.BlockSpec((tm, d), row)
    return pl.pallas_call(
        functools.partial(_dn_proj_kernel, n_seq=n_seq, pad=pad),
        grid=(n_groups, tiles),
        in_specs=[
            big(D_MODEL),
            pl.BlockSpec((1, D_MODEL), const),
            pl.BlockSpec((D_MODEL, N_END), const),
            pl.BlockSpec((CONV_W, CONV_DIM), const),
            pl.BlockSpec((None, pad, CONV_DIM), lambda s, i: (s, 0, 0)),
            pl.BlockSpec((1, LANES), const),
            pl.BlockSpec((1, LANES), const),
        ],
        out_specs=(big(D_MODEL), big(D_MODEL), big(D_MODEL), big(D_MODEL), big(LANES),
                   pl.BlockSpec((None, pad, CONV_DIM), lambda s, i: (s, 0, 0))),
        out_shape=(
            jax.ShapeDtypeStruct((n, D_MODEL), F32),
            jax.ShapeDtypeStruct((n, D_MODEL), F32),
            jax.ShapeDtypeStruct((n, D_MODEL), F32),
            jax.ShapeDtypeStruct((n, D_MODEL), BF16),
            jax.ShapeDtypeStruct((n, LANES), F32),
            jax.ShapeDtypeStruct((n_groups, pad, CONV_DIM), F32),
        ),
        scratch_shapes=[pltpu.VMEM((pad + tm, CONV_DIM), F32)],
        compiler_params=pltpu.CompilerParams(dimension_semantics=("parallel", "arbitrary"),
                                             vmem_limit_bytes=VMEM_LIMIT),
        name="dn_proj",
    )(x, norm_g, w_packed, conv_w, buf, alog_pad, dtb_pad)


def _dn_rec_kernel(q_ref, k_ref, v_ref, gb_ref, s0_ref, o_ref, sout_ref, s_scr):
    c_len = q_ref.shape[0]

    @pl.when(pl.program_id(1) == 0)
    def _():
        s_scr[...] = s0_ref[...]

    ri = lax.broadcasted_iota(I32, (c_len, c_len), 0)
    ci = lax.broadcasted_iota(I32, (c_len, c_len), 1)
    incl = ri >= ci
    strict = ri > ci
    tril = jnp.where(incl, 1.0, 0.0)
    eye_c = jnp.where(ri == ci, 1.0, 0.0)
    eye_k = jnp.where(lax.broadcasted_iota(I32, (DN_DK, DN_DK), 0) == lax.broadcasted_iota(I32, (DN_DK, DN_DK), 1),
                      1.0, 0.0)
    lane = lax.broadcasted_iota(I32, (c_len, LANES), 1)
    n_sq = max(0, (c_len - 1).bit_length() - 1)

    gbc = gb_ref[...]
    gcum = _dot(tril, gbc, HI)
    for h in range(DN_HEADS):
        hs = slice(h * DN_DK, (h + 1) * DN_DK)
        qh = q_ref[:, hs]
        kh = k_ref[:, hs]
        vh = v_ref[:, hs]
        beta = gbc[:, h:h + 1]
        gcol = gcum[:, DN_HEADS + h:DN_HEADS + h + 1]
        grow = _dot_nt(jnp.where(lane == DN_HEADS + h, 1.0, 0.0), gcum, HI)
        glast = gcol[c_len - 1:c_len, :]
        decay = jnp.where(incl, jnp.exp(jnp.where(incl, gcol - grow, 0.0)), 0.0)
        eg = jnp.exp(gcol)
        a_mat = beta * _dot_nt(kh, kh) * jnp.where(strict, decay, 0.0)
        pw = -a_mat
        t_inv = eye_c + pw
        for _ in range(n_sq):
            pw = _dot(pw, pw, HI)
            t_inv = t_inv + _dot(t_inv, pw, HI)
        w = _dot(t_inv, beta * eg * kh, HI)
        u0 = _dot(t_inv, beta * vh, HI)
        s_old = s_scr[h]
        u = u0 - _dot(w, s_old)
        qk = _dot_nt(qh, kh) * decay
        o_ref[:, hs] = eg * _dot(qh, s_old) + _dot(qk, u)
        kdt = _dot_nt(eye_k, kh * jnp.exp(glast - gcol), HI)
        s_scr[h] = jnp.exp(glast) * s_old + _dot(kdt, u)

    @pl.when(pl.program_id(1) == pl.num_programs(1) - 1)
    def _():
        sout_ref[...] = s_scr[...]


def _dn_rec(q, k, v, gb, s0, n_seq, t_len, chunk):
    assert t_len % chunk == 0
    n_chunks = t_len // chunk
    row = lambda b, c: (b * n_chunks + c, 0)
    st = pl.BlockSpec((None, None, DN_HEADS, DN_DK, DN_DV), lambda b, c: (b, 0, 0, 0, 0))
    return pl.pallas_call(
        _dn_rec_kernel,
        grid=(n_seq, n_chunks),
        in_specs=[pl.BlockSpec((chunk, D_MODEL), row)] * 3 + [pl.BlockSpec((chunk, LANES), row), st],
        out_specs=(pl.BlockSpec((chunk, D_MODEL), row), st),
        out_shape=(jax.ShapeDtypeStruct((n_seq * t_len, D_MODEL), F32),
                   jax.ShapeDtypeStruct((n_seq, 1, DN_HEADS, DN_DK, DN_DV), F32)),
        scratch_shapes=[pltpu.VMEM((DN_HEADS, DN_DK, DN_DV), F32)],
        compiler_params=pltpu.CompilerParams(dimension_semantics=("parallel", "arbitrary"),
                                             vmem_limit_bytes=VMEM_LIMIT),
        name="dn_rec",
    )(q, k, v, gb, s0)


def _pad_lanes(v, offset=0):
    return jnp.zeros((1, LANES), F32).at[0, offset:offset + v.shape[0]].set(v)


def _pack_att_w(w):
    d = w.shape[0]
    gate0 = 2 * D_MODEL + 2 * KV_W + IDX_HEADS * IDX_DIM + IDX_DIM + IDX_HEADS - D_MODEL
    return jnp.concatenate(
        [w[:, :gate0], jnp.zeros((d, A_GATE - gate0), w.dtype), w[:, gate0:]], axis=1).astype(BF16)


def _pack_dn_w(w):
    d = w.shape[0]
    return jnp.concatenate([w, jnp.zeros((d, N_END - w.shape[1]), w.dtype)], axis=1).astype(BF16)


def kernel(x_prompt, x_sample, cache_k, cache_v, cache_idx_k, state_dn_S, state_dn_conv, page_table,
           att_norm, att_w_in, att_q_gain, att_k_gain, att_ik_gain, att_w_out,
           dn_norm, dn_w_in, dn_conv_w, dn_A_log, dn_dt_bias, dn_o_gain, dn_w_out):
    bp, seq, d = x_prompt.shape
    bs, t_new, _ = x_sample.shape
    n_pool = cache_k.shape[0]
    np_rows, ns_rows = bp * seq, bs * t_new
    tm = 256

    xp = x_prompt.reshape(np_rows, d)
    xs = x_sample.reshape(ns_rows, d)

    w_att = _pack_att_w(att_w_in[0])
    a_norm = att_norm[0][None, :]
    qg, kg = att_q_gain[0][None, :], att_k_gain[0][None, :]
    ikg = _pad_lanes(att_ik_gain[0])
    w_ao = att_w_out[0].astype(BF16)
    ones_gain = jnp.ones((1, DN_DV), F32)

    q_p, k_p, kb_p, v_p, vt_p, iq_p, ikw_p, ikb_p, iwt_p, sg_p = _att_proj(xp, a_norm, w_att, qg, kg, ikg, tm)
    o_p = _att_prompt(q_p, kb_p, vt_p, ikb_p, iq_p, iwt_p, bp, seq)
    y1_p = _out_proj(o_p, sg_p, ones_gain, w_ao, xp, tm, head_norm=False)

    q_s, k_s, _, v_s, _, iq_s, ikw_s, ikb_s, _, sg_s = _att_proj(xs, a_norm, w_att, qg, kg, ikg, min(tm, ns_rows))
    pad_new = lambda a: jnp.pad(a.reshape(bs, t_new, a.shape[-1]), ((0, 0), (0, PAGE_SIZE - t_new), (0, 0)))
    o_s = _att_sample(page_table, q_s.astype(F32), iq_s.astype(F32), ikw_s, pad_new(ikb_s), pad_new(k_s), pad_new(v_s),
                      cache_idx_k.reshape(n_pool, PAGE_SIZE, IDX_DIM),
                      cache_k.reshape(n_pool, PAGE_SIZE, KV_W),
                      cache_v.reshape(n_pool, PAGE_SIZE, KV_W), t_new)
    y1_s = _out_proj(o_s, sg_s, ones_gain, w_ao, xs, min(tm, ns_rows), head_norm=False)

    w_dn = _pack_dn_w(dn_w_in[0])
    d_norm = dn_norm[0][None, :]
    alog = _pad_lanes(dn_A_log[0], DN_HEADS)
    dtb = _pad_lanes(dn_dt_bias[0], DN_HEADS)
    o_gain = dn_o_gain[0][None, :]
    w_do = dn_w_out[0].astype(BF16)
    hist = CONV_W - 1

    buf_p = jnp.zeros((bp, SUBLANES, CONV_DIM), F32)
    dq_p, dk_p, dv_p, sz_p, gb_p, cb_p = _dn_proj(y1_p, d_norm, w_dn, dn_conv_w[0], buf_p, alog, dtb,
                                                   n_groups=bp, n_seq=1, tm=tm)
    s0_p = jnp.zeros((bp, 1, DN_HEADS, DN_DK, DN_DV), F32)
    do_p, s_p = _dn_rec(dq_p, dk_p, dv_p, gb_p, s0_p, bp, seq, chunk=LANES)
    y2_p = _out_proj(do_p, sz_p, o_gain, w_do, y1_p, tm, head_norm=True)

    to_tm = lambda a: a.reshape(bs, t_new, -1).transpose(1, 0, 2).reshape(ns_rows, -1)
    to_bm = lambda a: a.reshape(t_new, bs, -1).transpose(1, 0, 2).reshape(ns_rows, -1)
    buf_s = state_dn_conv[:, 0].transpose(1, 0, 2).reshape(1, hist * bs, CONV_DIM)
    dq_s, dk_s, dv_s, sz_s, gb_s, cb_s = _dn_proj(to_tm(y1_s), d_norm, w_dn, dn_conv_w[0], buf_s, alog, dtb,
                                                   n_groups=1, n_seq=bs, tm=ns_rows)
    do_s, s_s = _dn_rec(to_bm(dq_s), to_bm(dk_s), to_bm(dv_s), to_bm(gb_s), state_dn_S, bs, t_new, chunk=t_new)
    y2_s = _out_proj(do_s, to_bm(sz_s), o_gain, w_do, y1_s, min(tm, ns_rows), head_norm=True)

    return (
        y2_p.reshape(bp, seq, d),
        y2_s.reshape(bs, t_new, d),
        k_p.reshape(bp, seq, 1, N_KV_HEADS, HEAD_DIM),
        v_p.reshape(bp, seq, 1, N_KV_HEADS, HEAD_DIM),
        ikw_p[:, :IDX_DIM].reshape(bp, seq, 1, IDX_DIM),
        k_s.reshape(bs, t_new, 1, N_KV_HEADS, HEAD_DIM),
        v_s.reshape(bs, t_new, 1, N_KV_HEADS, HEAD_DIM),
        ikw_s[:, :IDX_DIM].reshape(bs, t_new, 1, IDX_DIM),
        s_p,
        cb_p[:, SUBLANES - hist:, :].reshape(bp, 1, hist, CONV_DIM),
        s_s,
        cb_s.reshape(hist, bs, CONV_DIM).transpose(1, 0, 2).reshape(bs, 1, hist, CONV_DIM),
    )
```

```python
import functools

import jax
import jax.numpy as jnp
from jax import lax
from jax.experimental import pallas as pl
from jax.experimental.pallas import tpu as pltpu

F32 = jnp.float32
BF16 = jnp.bfloat16
I32 = jnp.int32

EPS = 1e-6
LANES = 128
SUBLANES = 8
VMEM_LIMIT = 56 * 1024 * 1024

D_MODEL = 1024
N_HEADS = 8
HEAD_DIM = 128
N_KV_HEADS = 2
GROUP = N_HEADS // N_KV_HEADS
IDX_HEADS = 8
IDX_DIM = 64
TOPK_MAX = 256
PAGE_SIZE = 128
DN_HEADS = 8
DN_DK = 128
DN_DV = 128
CONV_W = 4
KV_W = N_KV_HEADS * HEAD_DIM
CONV_DIM = 3 * DN_HEADS * DN_DK

A_Q, A_K, A_V, A_IQ, A_IKW, A_GATE, A_END = 0, 1024, 1280, 1536, 2048, 2176, 3200
N_Z, N_BA, N_END = 3072, 4096, 4224

INT_MIN = -2147483648
INT_MAX = 2147483647
NEG_BIG = -1e30
NT_DIMS = (((1,), (1,)), ((), ()))
HI = lax.Precision.HIGHEST


def _dot(a, b, precision=None):
    return jnp.dot(a, b, preferred_element_type=F32, precision=precision)


def _dot_nt(a, b, precision=None):
    return lax.dot_general(a, b, NT_DIMS, preferred_element_type=F32, precision=precision)


def _split_bf16(x):
    hi = x.astype(BF16)
    return hi, (x - hi.astype(F32)).astype(BF16)


def _dot3(a, b):
    (ah, al), (bh, bl) = a, b
    return _dot(ah, bh) + (_dot(ah, bl) + _dot(al, bh))


def _rms(x, n):
    return x * lax.rsqrt(jnp.sum(x * x, axis=-1, keepdims=True) * (1.0 / n) + EPS)


def _ordinal_to_float(u):
    key = u ^ jnp.int32(INT_MIN)
    return pltpu.bitcast(jnp.where(key < 0, key ^ jnp.int32(INT_MAX), key), F32)


def _att_proj_kernel(x_ref, g_ref, w_ref, qg_ref, kg_ref, ikg_ref,
                     q_ref, k_ref, kb_ref, v_ref, vt_ref, iq_ref, ikw_ref, ikb_ref, iwt_ref, sg_ref):
    x = x_ref[...]
    xn = _rms(x, D_MODEL) * g_ref[...]
    p = _dot(xn.astype(BF16), w_ref[...])
    for h in range(N_HEADS):
        qh = _rms(p[:, A_Q + h * HEAD_DIM:A_Q + (h + 1) * HEAD_DIM], HEAD_DIM) * qg_ref[...]
        q_ref[:, h * HEAD_DIM:(h + 1) * HEAD_DIM] = (qh * (HEAD_DIM ** -0.5)).astype(BF16)
    for g in range(N_KV_HEADS):
        kh = _rms(p[:, A_K + g * HEAD_DIM:A_K + (g + 1) * HEAD_DIM], HEAD_DIM) * kg_ref[...]
        k_ref[:, g * HEAD_DIM:(g + 1) * HEAD_DIM] = kh
        kb_ref[:, g * HEAD_DIM:(g + 1) * HEAD_DIM] = kh.astype(BF16)
    v = p[:, A_V:A_V + KV_W]
    v_ref[...] = v
    vt_ref[...] = v.T.astype(BF16)
    lane = lax.broadcasted_iota(I32, (1, LANES), 1)
    is_ik = lane < IDX_DIM
    for h in range(IDX_HEADS):
        iqh = p[:, A_IQ + (h // 2) * LANES:A_IQ + (h // 2 + 1) * LANES]
        if h % 2:
            iqh = pltpu.roll(iqh, IDX_DIM, axis=1)
        iq_ref[h] = jnp.where(is_ik, iqh, 0.0).astype(BF16)
    ikw = p[:, A_IKW:A_IKW + LANES]
    ik = jnp.where(is_ik, ikw, 0.0)
    ikn = _rms(ik, IDX_DIM) * ikg_ref[...]
    out = jnp.where(is_ik, ikn, ikw * (IDX_HEADS ** -0.5 * IDX_DIM ** -0.5))
    ikw_ref[...] = out
    ikb_ref[...] = ikn.astype(BF16)
    iwt_ref[...] = out.T[IDX_DIM:IDX_DIM + IDX_HEADS, :]
    gate = p[:, A_GATE:A_END]
    sg_ref[...] = (gate * jax.nn.sigmoid(gate)).astype(BF16)


def _att_proj(x, norm_g, w_packed, q_gain, k_gain, ik_gain_pad, tm):
    n = x.shape[0]
    assert n % tm == 0 and tm % LANES == 0
    row = lambda i: (i, 0)
    const = lambda i: (0, 0)
    out_shape = (
        jax.ShapeDtypeStruct((n, D_MODEL), BF16),
        jax.ShapeDtypeStruct((n, KV_W), F32),
        jax.ShapeDtypeStruct((n, KV_W), BF16),
        jax.ShapeDtypeStruct((n, KV_W), F32),
        jax.ShapeDtypeStruct((KV_W, n), BF16),
        jax.ShapeDtypeStruct((IDX_HEADS, n, LANES), BF16),
        jax.ShapeDtypeStruct((n, LANES), F32),
        jax.ShapeDtypeStruct((n, LANES), BF16),
        jax.ShapeDtypeStruct((IDX_HEADS, n), F32),
        jax.ShapeDtypeStruct((n, D_MODEL), BF16),
    )
    out_specs = (
        pl.BlockSpec((tm, D_MODEL), row),
        pl.BlockSpec((tm, KV_W), row),
        pl.BlockSpec((tm, KV_W), row),
        pl.BlockSpec((tm, KV_W), row),
        pl.BlockSpec((KV_W, tm), lambda i: (0, i)),
        pl.BlockSpec((IDX_HEADS, tm, LANES), lambda i: (0, i, 0)),
        pl.BlockSpec((tm, LANES), row),
        pl.BlockSpec((tm, LANES), row),
        pl.BlockSpec((IDX_HEADS, tm), lambda i: (0, i)),
        pl.BlockSpec((tm, D_MODEL), row),
    )
    return pl.pallas_call(
        _att_proj_kernel,
        grid=(n // tm,),
        in_specs=[
            pl.BlockSpec((tm, D_MODEL), row),
            pl.BlockSpec((1, D_MODEL), const),
            pl.BlockSpec((D_MODEL, A_END), const),
            pl.BlockSpec((1, HEAD_DIM), const),
            pl.BlockSpec((1, HEAD_DIM), const),
            pl.BlockSpec((1, LANES), const),
        ],
        out_specs=out_specs,
        out_shape=out_shape,
        compiler_params=pltpu.CompilerParams(dimension_semantics=("parallel",), vmem_limit_bytes=VMEM_LIMIT),
        name="att_proj",
    )(x, norm_g, w_packed, q_gain, k_gain, ik_gain_pad)


def _reduce(x, axis, combine, finish):
    n = x.shape[axis]
    unit = 8 * (SUBLANES if axis == 0 else LANES)
    full = n // unit
    if full < 2:
        return finish(x, axis=axis, keepdims=True)
    cut = (lambda i, j: x[i:j, :]) if axis == 0 else (lambda i, j: x[:, i:j])
    acc = cut(0, unit)
    for i in range(1, full):
        acc = combine(acc, cut(i * unit, (i + 1) * unit))
    out = finish(acc, axis=axis, keepdims=True)
    if n % unit:
        out = combine(out, finish(cut(full * unit, n), axis=axis, keepdims=True))
    return out


def _count(mask, axis):
    return _reduce(jnp.where(mask, 1.0, 0.0), axis, jnp.add, jnp.sum)


def _select_bias(sc_ref, lo_ref, n_keys, key_idx, topk, axis):
    sl = (slice(0, n_keys), slice(None)) if axis == 0 else (slice(None), slice(0, n_keys))
    vec = (1, sc_ref.shape[1]) if axis == 0 else (sc_ref.shape[0], 1)
    kf = float(topk)

    def bit_step(i, prefix):
        cand = prefix | (jnp.int32(1) << (31 - i))
        cnt = _count(sc_ref[sl] >= _ordinal_to_float(cand), axis)
        return jnp.where(cnt >= kf, cand, prefix)

    prefix = lax.fori_loop(0, 32, bit_step, jnp.zeros(vec, I32))
    thr = _ordinal_to_float(prefix)
    sc = sc_ref[sl]
    few = jnp.logical_not(_count(sc >= thr, axis) >= kf)
    thr = jnp.where(few, -jnp.inf, thr)
    need = kf - _count(sc > thr, axis)
    excess = (_count(sc >= thr, axis) > kf) & jnp.logical_not(few)
    lo_ref[...] = jnp.full(vec, INT_MAX, I32)

    @pl.when(jnp.max(jnp.where(excess, 1.0, 0.0)) > 0.0)
    def _():
        nbits = max(1, (n_keys - 1).bit_length())

        def idx_step(i, lo):
            cand = lo + (jnp.int32(1) << (nbits - 1 - i))
            cnt = _count((sc_ref[sl] == thr) & (key_idx < cand), axis)
            return jnp.where(cnt < need, cand, lo)

        lo_ref[...] = lax.fori_loop(0, nbits, idx_step, jnp.zeros(vec, I32))

    lo = jnp.where(few, -1, lo_ref[...])
    sel = (sc > thr) | ((sc == thr) & (key_idx <= lo))
    return jnp.where(sel, 0.0, NEG_BIG)


def _att_prompt_body(n_keys, topk, q_ref, kb_ref, vt_ref, ikb_ref, iq_ref, iwt_ref, o_ref,
                     sc_ref, bias_ref, lo_ref):
    qb = q_ref.shape[0]
    t0 = pl.program_id(1) * qb
    ik = ikb_ref[0:n_keys, :]
    iwt = iwt_ref[...]
    score = jnp.zeros((n_keys, qb), F32)
    for h in range(IDX_HEADS):
        s = _dot_nt(ik, iq_ref[h])
        score = score + jnp.maximum(s, 0.0) * iwt[h:h + 1, :]
    key_idx = lax.broadcasted_iota(I32, (n_keys, qb), 0)
    q_pos = t0 + lax.broadcasted_iota(I32, (n_keys, qb), 1)
    sc_ref[0:n_keys, :] = jnp.where(key_idx <= q_pos, score, -jnp.inf)
    bias_ref[0:n_keys, :] = _select_bias(sc_ref, lo_ref, n_keys, key_idx, topk, axis=0)
    for h in range(N_HEADS):
        g = h // GROUP
        s = _dot_nt(kb_ref[0:n_keys, g * HEAD_DIM:(g + 1) * HEAD_DIM],
                    q_ref[:, h * HEAD_DIM:(h + 1) * HEAD_DIM])
        s = s + bias_ref[0:n_keys, :]
        m = _reduce(s, 0, jnp.maximum, jnp.max)
        p = jnp.exp(s - m)
        l = _reduce(p, 0, jnp.add, jnp.sum)
        ot = _dot(vt_ref[g * HEAD_DIM:(g + 1) * HEAD_DIM, 0:n_keys], p.astype(BF16))
        o_ref[:, h * HEAD_DIM:(h + 1) * HEAD_DIM] = (ot / l).T


def _att_prompt_kernel(q_ref, kb_ref, vt_ref, ikb_ref, iq_ref, iwt_ref, o_ref,
                       sc_ref, bias_ref, lo_ref, *, cls_len, n_cls, topk):
    qb = q_ref.shape[0]
    cls = (pl.program_id(1) * qb) // cls_len
    for c in range(n_cls):
        pl.when(cls == c)(functools.partial(
            _att_prompt_body, cls_len * (c + 1), topk, q_ref, kb_ref, vt_ref, ikb_ref, iq_ref, iwt_ref,
            o_ref, sc_ref, bias_ref, lo_ref))


def _att_prompt(q, kb, vt, ikb, iq, iwt, batch, seq, qb=LANES):
    nb = seq // qb
    n_cls = min(4, nb)
    assert seq % (n_cls * qb) == 0
    topk = min(TOPK_MAX, seq // 4)
    kern = functools.partial(_att_prompt_kernel, cls_len=seq // n_cls, n_cls=n_cls, topk=topk)
    return pl.pallas_call(
        kern,
        grid=(batch, nb),
        in_specs=[
            pl.BlockSpec((qb, D_MODEL), lambda b, j: (b * nb + j, 0)),
            pl.BlockSpec((seq, KV_W), lambda b, j: (b, 0)),
            pl.BlockSpec((KV_W, seq), lambda b, j: (0, b)),
            pl.BlockSpec((seq, LANES), lambda b, j: (b, 0)),
            pl.BlockSpec((IDX_HEADS, qb, LANES), lambda b, j: (0, b * nb + j, 0)),
            pl.BlockSpec((IDX_HEADS, qb), lambda b, j: (0, b * nb + j)),
        ],
        out_specs=pl.BlockSpec((qb, D_MODEL), lambda b, j: (b * nb + j, 0)),
        out_shape=jax.ShapeDtypeStruct((batch * seq, D_MODEL), F32),
        scratch_shapes=[
            pltpu.VMEM((seq, qb), F32),
            pltpu.VMEM((seq, qb), F32),
            pltpu.VMEM((1, qb), I32),
        ],
        compiler_params=pltpu.CompilerParams(dimension_semantics=("parallel", "arbitrary"),
                                             vmem_limit_bytes=VMEM_LIMIT),
        name="att_prompt",
    )(q, kb, vt, ikb, iq, iwt)


def _att_sample_kernel(pt_ref, q_ref, iq_ref, ikw_ref, ikn_ref, kn_ref, vn_ref,
                       cik_hbm, ck_hbm, cv_hbm, o_ref,
                       ikbuf, kbuf, vbuf, sc_ref, bias_ref, lo_ref, sem_i, sem_kv,
                       *, n_pages, chunk_pages, topk):
    b = pl.program_id(0)
    t_new = q_ref.shape[0]
    past = n_pages * PAGE_SIZE
    n_keys = past + PAGE_SIZE
    n_chunks = n_pages // chunk_pages
    ck = chunk_pages * PAGE_SIZE

    def idx_copy(p):
        return pltpu.make_async_copy(cik_hbm.at[pt_ref[b, p]], ikbuf.at[:, pl.ds(p * PAGE_SIZE, PAGE_SIZE)], sem_i)

    def kv_copies(c, p):
        page = pt_ref[b, c * chunk_pages + p]
        dst = pl.ds(p * PAGE_SIZE * N_KV_HEADS, PAGE_SIZE * N_KV_HEADS)
        return (pltpu.make_async_copy(ck_hbm.at[page], kbuf.at[c % 2, dst], sem_kv.at[0, c % 2]),
                pltpu.make_async_copy(cv_hbm.at[page], vbuf.at[c % 2, dst], sem_kv.at[1, c % 2]))

    def start_chunk(c):
        def go(p, carry):
            for cp in kv_copies(c, p):
                cp.start()
            return carry
        lax.fori_loop(0, chunk_pages, go, 0)

    def wait_chunk(c):
        def go(p, carry):
            for cp in kv_copies(c, p):
                cp.wait()
            return carry
        lax.fori_loop(0, chunk_pages, go, 0)

    def idx_start(p, carry):
        idx_copy(p).start()
        return carry

    def idx_wait(p, carry):
        idx_copy(p).wait()
        return carry

    lax.fori_loop(0, n_pages, idx_start, 0)
    start_chunk(0)
    lax.fori_loop(0, n_pages, idx_wait, 0)

    iq2 = iq_ref[...].reshape(IDX_HEADS * t_new, LANES).astype(BF16)
    ikw = ikw_ref[...]
    lane1 = lax.broadcasted_iota(I32, (1, LANES), 1)

    def scores(ik_rows):
        s = _dot_nt(iq2, ik_rows)
        acc = jnp.zeros((t_new, ik_rows.shape[0]), F32)
        for h in range(IDX_HEADS):
            acc = acc + jnp.maximum(s[h * t_new:(h + 1) * t_new, :], 0.0) * ikw[:, IDX_DIM + h:IDX_DIM + h + 1]
        return acc

    iq2_past = iq2[:, 0:IDX_DIM]
    for c in range(n_chunks):
        s = _dot(iq2_past, ikbuf[:, c * ck:(c + 1) * ck].astype(BF16))
        acc = jnp.zeros((t_new, ck), F32)
        for h in range(IDX_HEADS):
            acc = acc + jnp.maximum(s[h * t_new:(h + 1) * t_new, :], 0.0) * ikw[:, IDX_DIM + h:IDX_DIM + h + 1]
        sc_ref[:, c * ck:(c + 1) * ck] = acc
    new_idx = lax.broadcasted_iota(I32, (t_new, PAGE_SIZE), 1)
    tok = lax.broadcasted_iota(I32, (t_new, PAGE_SIZE), 0)
    sc_ref[:, past:n_keys] = jnp.where(new_idx <= tok, scores(ikn_ref[...]), -jnp.inf)
    key_idx = lax.broadcasted_iota(I32, (t_new, n_keys), 1)
    bias_ref[...] = _select_bias(sc_ref, lo_ref, n_keys, key_idx, topk, axis=1)

    rows = GROUP * t_new
    qs = [jnp.concatenate([q_ref[:, (g * GROUP + r) * HEAD_DIM:(g * GROUP + r + 1) * HEAD_DIM]
                           for r in range(GROUP)], axis=0).astype(BF16) for g in range(N_KV_HEADS)]
    m = [jnp.full((rows, 1), NEG_BIG, F32) for _ in range(N_KV_HEADS)]
    l = [jnp.zeros((rows, 1), F32) for _ in range(N_KV_HEADS)]
    acc = [jnp.zeros((rows, HEAD_DIM), F32) for _ in range(N_KV_HEADS)]

    def attend(g, k_rows, v_rows, bias):
        s = _dot_nt(qs[g], k_rows) + jnp.concatenate([bias] * GROUP, axis=0)
        m_new = jnp.maximum(m[g], _reduce(s, 1, jnp.maximum, jnp.max))
        alpha = jnp.exp(m[g] - m_new)
        p = jnp.exp(s - m_new)
        l[g] = alpha * l[g] + _reduce(p, 1, jnp.add, jnp.sum)
        acc[g] = alpha * acc[g] + _dot(p.astype(BF16), v_rows)
        m[g] = m_new

    for g in range(N_KV_HEADS):
        hd = slice(g * HEAD_DIM, (g + 1) * HEAD_DIM)
        attend(g, kn_ref[:, hd].astype(BF16), vn_ref[:, hd].astype(BF16), bias_ref[:, past:n_keys])
    for c in range(n_chunks):
        if c + 1 < n_chunks:
            start_chunk(c + 1)
        wait_chunk(c)
        for g in range(N_KV_HEADS):
            head_rows = pl.ds(g, ck, stride=N_KV_HEADS)
            attend(g, kbuf[c % 2, head_rows, :].astype(BF16), vbuf[c % 2, head_rows, :].astype(BF16),
                   bias_ref[:, c * ck:(c + 1) * ck])
    for g in range(N_KV_HEADS):
        og = acc[g] / l[g]
        for r in range(GROUP):
            h = g * GROUP + r
            o_ref[:, h * HEAD_DIM:(h + 1) * HEAD_DIM] = og[r * t_new:(r + 1) * t_new, :]


def _att_sample(page_table, q, iq, ikw, ikn_pad, kn_pad, vn_pad, cache_ik, cache_k, cache_v, t_new):
    nseq, n_pages = page_table.shape
    assert t_new == SUBLANES
    chunk_pages = min(32, n_pages)
    assert n_pages % chunk_pages == 0
    past = n_pages * PAGE_SIZE
    n_keys = past + PAGE_SIZE
    topk = min(TOPK_MAX, (past + t_new) // 4)
    ck = chunk_pages * PAGE_SIZE
    kern = functools.partial(_att_sample_kernel, n_pages=n_pages, chunk_pages=chunk_pages, topk=topk)
    grid_spec = pltpu.PrefetchScalarGridSpec(
        num_scalar_prefetch=1,
        grid=(nseq,),
        in_specs=[
            pl.BlockSpec((t_new, D_MODEL), lambda b, pt: (b, 0)),
            pl.BlockSpec((IDX_HEADS, t_new, LANES), lambda b, pt: (0, b, 0)),
            pl.BlockSpec((t_new, LANES), lambda b, pt: (b, 0)),
            pl.BlockSpec((None, PAGE_SIZE, LANES), lambda b, pt: (b, 0, 0)),
            pl.BlockSpec((None, PAGE_SIZE, KV_W), lambda b, pt: (b, 0, 0)),
            pl.BlockSpec((None, PAGE_SIZE, KV_W), lambda b, pt: (b, 0, 0)),
            pl.BlockSpec(memory_space=pl.ANY),
            pl.BlockSpec(memory_space=pl.ANY),
            pl.BlockSpec(memory_space=pl.ANY),
        ],
        out_specs=pl.BlockSpec((t_new, D_MODEL), lambda b, pt: (b, 0)),
        scratch_shapes=[
            pltpu.VMEM((IDX_DIM, past), F32),
            pltpu.VMEM((2, ck * N_KV_HEADS, HEAD_DIM), F32),
            pltpu.VMEM((2, ck * N_KV_HEADS, HEAD_DIM), F32),
            pltpu.VMEM((t_new, n_keys), F32),
            pltpu.VMEM((t_new, n_keys), F32),
            pltpu.VMEM((t_new, 1), I32),
            pltpu.SemaphoreType.DMA(()),
            pltpu.SemaphoreType.DMA((2, 2)),
        ],
    )
    return pl.pallas_call(
        kern,
        grid_spec=grid_spec,
        out_shape=jax.ShapeDtypeStruct((nseq * t_new, D_MODEL), F32),
        compiler_params=pltpu.CompilerParams(dimension_semantics=("arbitrary",), vmem_limit_bytes=VMEM_LIMIT),
        name="att_sample",
    )(page_table, q, iq, ikw, ikn_pad, kn_pad, vn_pad, cache_ik, cache_k, cache_v)


def _out_proj_kernel(o_ref, sg_ref, gain_ref, w_ref, x_ref, y_ref, *, head_norm):
    o = o_ref[...]
    if head_norm:
        o = jnp.concatenate(
            [_rms(o[:, h * DN_DV:(h + 1) * DN_DV], DN_DV) * gain_ref[...] for h in range(DN_HEADS)], axis=1)
    a = (o * sg_ref[...].astype(F32)).astype(BF16)
    y_ref[...] = x_ref[...] + _dot(a, w_ref[...])


def _out_proj(o, sg, gain, w_out, x, tm, head_norm):
    n = o.shape[0]
    row = lambda i: (i, 0)
    const = lambda i: (0, 0)
    return pl.pallas_call(
        functools.partial(_out_proj_kernel, head_norm=head_norm),
        grid=(n // tm,),
        in_specs=[
            pl.BlockSpec((tm, D_MODEL), row),
            pl.BlockSpec((tm, D_MODEL), row),
            pl.BlockSpec((1, DN_DV), const),
            pl.BlockSpec((D_MODEL, D_MODEL), const),
            pl.BlockSpec((tm, D_MODEL), row),
        ],
        out_specs=pl.BlockSpec((tm, D_MODEL), row),
        out_shape=jax.ShapeDtypeStruct((n, D_MODEL), F32),
        compiler_params=pltpu.CompilerParams(dimension_semantics=("parallel",), vmem_limit_bytes=VMEM_LIMIT),
        name="out_proj",
    )(o, sg, gain, w_out, x)


def _dn_proj_kernel(x_ref, g_ref, w_ref, cw_ref, buf_ref, alog_ref, dtb_ref,
                    q_ref, k_ref, v_ref, sz_ref, gb_ref, cbuf_ref, xp_ref, *, n_seq, pad):
    tm = x_ref.shape[0]

    @pl.when(pl.program_id(1) == 0)
    def _():
        xp_ref[0:pad, :] = buf_ref[...]

    xn = _rms(x_ref[...], D_MODEL) * g_ref[...]
    p = _dot(xn.astype(BF16), w_ref[...])
    xp_ref[pad:pad + tm, :] = p[:, 0:CONV_DIM]
    conv = xp_ref[pad:pad + tm, :] * cw_ref[CONV_W - 1:CONV_W, :]
    for i in range(1, CONV_W):
        conv = conv + xp_ref[pad - i * n_seq:pad - i * n_seq + tm, :] * cw_ref[CONV_W - 1 - i:CONV_W - i, :]
    act = conv * jax.nn.sigmoid(conv)
    for h in range(DN_HEADS):
        sl = slice(h * DN_DK, (h + 1) * DN_DK)
        qh = act[:, sl]
        q_ref[:, sl] = qh * lax.rsqrt(jnp.sum(qh * qh, axis=-1, keepdims=True) + EPS) * (DN_DK ** -0.5)
        kh = act[:, DN_HEADS * DN_DK + h * DN_DK:DN_HEADS * DN_DK + (h + 1) * DN_DK]
        k_ref[:, sl] = kh * lax.rsqrt(jnp.sum(kh * kh, axis=-1, keepdims=True) + EPS)
    v_ref[...] = act[:, 2 * DN_HEADS * DN_DK:CONV_DIM]
    z = p[:, N_Z:N_BA]
    sz_ref[...] = (z * jax.nn.sigmoid(z)).astype(BF16)
    ba = p[:, N_BA:N_END]
    sp_in = ba + dtb_ref[...]
    softplus = jnp.maximum(sp_in, 0.0) + jnp.log1p(jnp.exp(-jnp.abs(sp_in)))
    lane = lax.broadcasted_iota(I32, (1, LANES), 1)
    gb_ref[...] = jnp.where(lane < DN_HEADS, jax.nn.sigmoid(ba), -jnp.exp(alog_ref[...]) * softplus)
    cbuf_ref[...] = xp_ref[tm:tm + pad, :]
    xp_ref[0:pad, :] = xp_ref[tm:tm + pad, :]


def _dn_proj(x, norm_g, w_packed, conv_w, buf, alog_pad, dtb_pad, n_groups, n_seq, tm):
    n = x.shape[0]
    pad = buf.shape[1]
    tiles = n // n_groups // tm
    assert n == n_groups * tiles * tm and pad % SUBLANES == 0 and pad >= (CONV_W - 1) * n_seq
    row = lambda s, i: (s * tiles + i, 0)
    const = lambda s, i: (0, 0)
    big = lambda d: pl.BlockSpec((tm, d), row)
    return pl.pallas_call(
        functools.partial(_dn_proj_kernel, n_seq=n_seq, pad=pad),
        grid=(n_groups, tiles),
        in_specs=[
            big(D_MODEL),
            pl.BlockSpec((1, D_MODEL), const),
            pl.BlockSpec((D_MODEL, N_END), const),
            pl.BlockSpec((CONV_W, CONV_DIM), const),
            pl.BlockSpec((None, pad, CONV_DIM), lambda s, i: (s, 0, 0)),
            pl.BlockSpec((1, LANES), const),
            pl.BlockSpec((1, LANES), const),
        ],
        out_specs=(big(D_MODEL), big(D_MODEL), big(D_MODEL), big(D_MODEL), big(LANES),
                   pl.BlockSpec((None, pad, CONV_DIM), lambda s, i: (s, 0, 0))),
        out_shape=(
            jax.ShapeDtypeStruct((n, D_MODEL), F32),
            jax.ShapeDtypeStruct((n, D_MODEL), F32),
            jax.ShapeDtypeStruct((n, D_MODEL), F32),
            jax.ShapeDtypeStruct((n, D_MODEL), BF16),
            jax.ShapeDtypeStruct((n, LANES), F32),
            jax.ShapeDtypeStruct((n_groups, pad, CONV_DIM), F32),
        ),
        scratch_shapes=[pltpu.VMEM((pad + tm, CONV_DIM), F32)],
        compiler_params=pltpu.CompilerParams(dimension_semantics=("parallel", "arbitrary"),
                                             vmem_limit_bytes=VMEM_LIMIT),
        name="dn_proj",
    )(x, norm_g, w_packed, conv_w, buf, alog_pad, dtb_pad)


def _dn_rec_kernel(q_ref, k_ref, v_ref, gb_ref, s0_ref, o_ref, sout_ref, s_scr):
    c_len = q_ref.shape[0]

    @pl.when(pl.program_id(1) == 0)
    def _():
        s_scr[...] = s0_ref[...]

    ri = lax.broadcasted_iota(I32, (c_len, c_len), 0)
    ci = lax.broadcasted_iota(I32, (c_len, c_len), 1)
    incl = ri >= ci
    strict = ri > ci
    tril = jnp.where(incl, 1.0, 0.0)
    eye_c = jnp.where(ri == ci, 1.0, 0.0)
    eye_k = jnp.where(lax.broadcasted_iota(I32, (DN_DK, DN_DK), 0) == lax.broadcasted_iota(I32, (DN_DK, DN_DK), 1),
                      1.0, 0.0)
    lane = lax.broadcasted_iota(I32, (c_len, LANES), 1)
    n_sq = max(0, (c_len - 1).bit_length() - 1)

    tile_aligned = c_len % LANES == 0
    gbc = gb_ref[...]
    gcum = _dot(tril, gbc, HI)
    gcum_t = gcum.T if tile_aligned else None
    heads = range(DN_HEADS)
    hs = [slice(h * DN_DK, (h + 1) * DN_DK) for h in heads]
    beta = [gbc[:, h:h + 1] for h in heads]
    gcol = [gcum[:, DN_HEADS + h:DN_HEADS + h + 1] for h in heads]
    if tile_aligned:
        grow = [gcum_t[DN_HEADS + h:DN_HEADS + h + 1, :] for h in heads]
    else:
        grow = [_dot_nt(jnp.where(lane == DN_HEADS + h, 1.0, 0.0), gcum, HI) for h in heads]
    glast = [gcol[h][c_len - 1:c_len, :] for h in heads]
    decay = [jnp.where(incl, jnp.exp(jnp.where(incl, gcol[h] - grow[h], 0.0)), 0.0) for h in heads]
    eg = [jnp.exp(gcol[h]) for h in heads]
    a_mat = [beta[h] * _dot_nt(k_ref[:, hs[h]], k_ref[:, hs[h]]) * jnp.where(strict, decay[h], 0.0) for h in heads]
    pw = [_split_bf16(-a_mat[h]) for h in heads]
    t_inv = [eye_c - a_mat[h] for h in heads]
    for _ in range(n_sq):
        pw = [_split_bf16(_dot3(pw[h], pw[h])) for h in heads]
        t_inv = [t_inv[h] + _dot3(_split_bf16(t_inv[h]), pw[h]) for h in heads]
    wu = [_dot3(_split_bf16(t_inv[h]),
                _split_bf16(jnp.concatenate([beta[h] * eg[h] * k_ref[:, hs[h]], beta[h] * v_ref[:, hs[h]]], axis=1)))
          for h in heads]
    qk = [_dot_nt(q_ref[:, hs[h]], k_ref[:, hs[h]]) * decay[h] for h in heads]
    kd = [k_ref[:, hs[h]] * jnp.exp(glast[h] - gcol[h]) for h in heads]
    kdt = [kd[h].T if tile_aligned else _dot_nt(eye_k, kd[h], HI) for h in heads]
    for h in heads:
        s_old = s_scr[h]
        u = wu[h][:, DN_DK:DN_DK + DN_DV] - _dot(wu[h][:, 0:DN_DK], s_old)
        o_ref[:, hs[h]] = eg[h] * _dot(q_ref[:, hs[h]], s_old) + _dot(qk[h], u)
        s_scr[h] = jnp.exp(glast[h]) * s_old + _dot(kdt[h], u)

    @pl.when(pl.program_id(1) == pl.num_programs(1) - 1)
    def _():
        sout_ref[...] = s_scr[...]


def _dn_rec(q, k, v, gb, s0, n_seq, t_len, chunk):
    assert t_len % chunk == 0
    n_chunks = t_len // chunk
    row = lambda b, c: (b * n_chunks + c, 0)
    st = pl.BlockSpec((None, None, DN_HEADS, DN_DK, DN_DV), lambda b, c: (b, 0, 0, 0, 0))
    return pl.pallas_call(
        _dn_rec_kernel,
        grid=(n_seq, n_chunks),
        in_specs=[pl.BlockSpec((chunk, D_MODEL), row)] * 3 + [pl.BlockSpec((chunk, LANES), row), st],
        out_specs=(pl.BlockSpec((chunk, D_MODEL), row), st),
        out_shape=(jax.ShapeDtypeStruct((n_seq * t_len, D_MODEL), F32),
                   jax.ShapeDtypeStruct((n_seq, 1, DN_HEADS, DN_DK, DN_DV), F32)),
        scratch_shapes=[pltpu.VMEM((DN_HEADS, DN_DK, DN_DV), F32)],
        compiler_params=pltpu.CompilerParams(dimension_semantics=("parallel", "arbitrary"),
                                             vmem_limit_bytes=VMEM_LIMIT),
        name="dn_rec",
    )(q, k, v, gb, s0)


def _pad_lanes(v, offset=0):
    return jnp.zeros((1, LANES), F32).at[0, offset:offset + v.shape[0]].set(v)


def _pack_att_w(w):
    d = w.shape[0]
    gate0 = 2 * D_MODEL + 2 * KV_W + IDX_HEADS * IDX_DIM + IDX_DIM + IDX_HEADS - D_MODEL
    return jnp.concatenate(
        [w[:, :gate0], jnp.zeros((d, A_GATE - gate0), w.dtype), w[:, gate0:]], axis=1).astype(BF16)


def _pack_dn_w(w):
    d = w.shape[0]
    return jnp.concatenate([w, jnp.zeros((d, N_END - w.shape[1]), w.dtype)], axis=1).astype(BF16)


def kernel(x_prompt, x_sample, cache_k, cache_v, cache_idx_k, state_dn_S, state_dn_conv, page_table,
           att_norm, att_w_in, att_q_gain, att_k_gain, att_ik_gain, att_w_out,
           dn_norm, dn_w_in, dn_conv_w, dn_A_log, dn_dt_bias, dn_o_gain, dn_w_out):
    bp, seq, d = x_prompt.shape
    bs, t_new, _ = x_sample.shape
    n_pool = cache_k.shape[0]
    np_rows, ns_rows = bp * seq, bs * t_new
    tm = 256

    xp = x_prompt.reshape(np_rows, d)
    xs = x_sample.reshape(ns_rows, d)

    w_att = _pack_att_w(att_w_in[0])
    a_norm = att_norm[0][None, :]
    qg, kg = att_q_gain[0][None, :], att_k_gain[0][None, :]
    ikg = _pad_lanes(att_ik_gain[0])
    w_ao = att_w_out[0].astype(BF16)
    ones_gain = jnp.ones((1, DN_DV), F32)

    q_p, k_p, kb_p, v_p, vt_p, iq_p, ikw_p, ikb_p, iwt_p, sg_p = _att_proj(xp, a_norm, w_att, qg, kg, ikg, tm)
    o_p = _att_prompt(q_p, kb_p, vt_p, ikb_p, iq_p, iwt_p, bp, seq)
    y1_p = _out_proj(o_p, sg_p, ones_gain, w_ao, xp, tm, head_norm=False)

    q_s, k_s, _, v_s, _, iq_s, ikw_s, ikb_s, _, sg_s = _att_proj(xs, a_norm, w_att, qg, kg, ikg, min(tm, ns_rows))
    pad_new = lambda a: jnp.pad(a.reshape(bs, t_new, a.shape[-1]), ((0, 0), (0, PAGE_SIZE - t_new), (0, 0)))
    o_s = _att_sample(page_table, q_s.astype(F32), iq_s.astype(F32), ikw_s, pad_new(ikb_s), pad_new(k_s), pad_new(v_s),
                      jnp.transpose(cache_idx_k, (0, 2, 3, 1)).reshape(n_pool, IDX_DIM, PAGE_SIZE),
                      cache_k.reshape(n_pool, PAGE_SIZE * N_KV_HEADS, HEAD_DIM),
                      cache_v.reshape(n_pool, PAGE_SIZE * N_KV_HEADS, HEAD_DIM), t_new)
    y1_s = _out_proj(o_s, sg_s, ones_gain, w_ao, xs, min(tm, ns_rows), head_norm=False)

    w_dn = _pack_dn_w(dn_w_in[0])
    d_norm = dn_norm[0][None, :]
    alog = _pad_lanes(dn_A_log[0], DN_HEADS)
    dtb = _pad_lanes(dn_dt_bias[0], DN_HEADS)
    o_gain = dn_o_gain[0][None, :]
    w_do = dn_w_out[0].astype(BF16)
    hist = CONV_W - 1

    buf_p = jnp.zeros((bp, SUBLANES, CONV_DIM), F32)
    dq_p, dk_p, dv_p, sz_p, gb_p, cb_p = _dn_proj(y1_p, d_norm, w_dn, dn_conv_w[0], buf_p, alog, dtb,
                                                   n_groups=bp, n_seq=1, tm=tm)
    s0_p = jnp.zeros((bp, 1, DN_HEADS, DN_DK, DN_DV), F32)
    do_p, s_p = _dn_rec(dq_p, dk_p, dv_p, gb_p, s0_p, bp, seq, chunk=LANES)
    y2_p = _out_proj(do_p, sz_p, o_gain, w_do, y1_p, tm, head_norm=True)

    to_tm = lambda a: a.reshape(bs, t_new, -1).transpose(1, 0, 2).reshape(ns_rows, -1)
    to_bm = lambda a: a.reshape(t_new, bs, -1).transpose(1, 0, 2).reshape(ns_rows, -1)
    buf_s = state_dn_conv[:, 0].transpose(1, 0, 2).reshape(1, hist * bs, CONV_DIM)
    dq_s, dk_s, dv_s, sz_s, gb_s, cb_s = _dn_proj(to_tm(y1_s), d_norm, w_dn, dn_conv_w[0], buf_s, alog, dtb,
                                                   n_groups=1, n_seq=bs, tm=ns_rows)
    do_s, s_s = _dn_rec(to_bm(dq_s), to_bm(dk_s), to_bm(dv_s), to_bm(gb_s), state_dn_S, bs, t_new, chunk=t_new)
    y2_s = _out_proj(do_s, to_bm(sz_s), o_gain, w_do, y1_s, min(tm, ns_rows), head_norm=True)

    return (
        y2_p.reshape(bp, seq, d),
        y2_s.reshape(bs, t_new, d),
        k_p.reshape(bp, seq, 1, N_KV_HEADS, HEAD_DIM),
        v_p.reshape(bp, seq, 1, N_KV_HEADS, HEAD_DIM),
        ikw_p[:, :IDX_DIM].reshape(bp, seq, 1, IDX_DIM),
        k_s.reshape(bs, t_new, 1, N_KV_HEADS, HEAD_DIM),
        v_s.reshape(bs, t_new, 1, N_KV_HEADS, HEAD_DIM),
        ikw_s[:, :IDX_DIM].reshape(bs, t_new, 1, IDX_DIM),
        s_p,
        cb_p[:, SUBLANES - hist:, :].reshape(bp, 1, hist, CONV_DIM),
        s_s,
        cb_s.reshape(hist, bs, CONV_DIM).transpose(1, 0, 2).reshape(bs, 1, hist, CONV_DIM),
    )
```

```python
import functools

import jax
import jax.numpy as jnp
from jax import lax
from jax.experimental import pallas as pl
from jax.experimental.pallas import tpu as pltpu

F32 = jnp.float32
BF16 = jnp.bfloat16
I32 = jnp.int32

EPS = 1e-6
LANES = 128
SUBLANES = 8
VMEM_LIMIT = 56 * 1024 * 1024

D_MODEL = 1024
N_HEADS = 8
HEAD_DIM = 128
N_KV_HEADS = 2
GROUP = N_HEADS // N_KV_HEADS
IDX_HEADS = 8
IDX_DIM = 64
TOPK_MAX = 256
PAGE_SIZE = 128
DN_HEADS = 8
DN_DK = 128
DN_DV = 128
CONV_W = 4
KV_W = N_KV_HEADS * HEAD_DIM
CONV_DIM = 3 * DN_HEADS * DN_DK

A_Q, A_K, A_V, A_IQ, A_IKW, A_GATE, A_END = 0, 1024, 1280, 1536, 2048, 2176, 3200
N_Z, N_BA, N_END = 3072, 4096, 4224

INT_MIN = -2147483648
INT_MAX = 2147483647
NEG_BIG = -1e30
NT_DIMS = (((1,), (1,)), ((), ()))
HI = lax.Precision.HIGHEST


def _dot(a, b, precision=None):
    return jnp.dot(a, b, preferred_element_type=F32, precision=precision)


def _dot_nt(a, b, precision=None):
    return lax.dot_general(a, b, NT_DIMS, preferred_element_type=F32, precision=precision)


def _split_bf16(x):
    hi = x.astype(BF16)
    return hi, (x - hi.astype(F32)).astype(BF16)


def _dot3(a, b):
    (ah, al), (bh, bl) = a, b
    return _dot(ah, bh) + (_dot(ah, bl) + _dot(al, bh))


def _rms(x, n):
    return x * lax.rsqrt(jnp.sum(x * x, axis=-1, keepdims=True) * (1.0 / n) + EPS)


def _ordinal_to_float(u):
    key = u ^ jnp.int32(INT_MIN)
    return pltpu.bitcast(jnp.where(key < 0, key ^ jnp.int32(INT_MAX), key), F32)


def _att_proj_kernel(x_ref, g_ref, w_ref, qg_ref, kg_ref, ikg_ref,
                     q_ref, k_ref, kb_ref, v_ref, vt_ref, iq_ref, ikw_ref, ikb_ref, iwt_ref, sg_ref):
    x = x_ref[...]
    xn = _rms(x, D_MODEL) * g_ref[...]
    p = _dot(xn.astype(BF16), w_ref[...])
    for h in range(N_HEADS):
        qh = _rms(p[:, A_Q + h * HEAD_DIM:A_Q + (h + 1) * HEAD_DIM], HEAD_DIM) * qg_ref[...]
        q_ref[:, h * HEAD_DIM:(h + 1) * HEAD_DIM] = (qh * (HEAD_DIM ** -0.5)).astype(BF16)
    for g in range(N_KV_HEADS):
        kh = _rms(p[:, A_K + g * HEAD_DIM:A_K + (g + 1) * HEAD_DIM], HEAD_DIM) * kg_ref[...]
        k_ref[:, g * HEAD_DIM:(g + 1) * HEAD_DIM] = kh
        kb_ref[:, g * HEAD_DIM:(g + 1) * HEAD_DIM] = kh.astype(BF16)
    v = p[:, A_V:A_V + KV_W]
    v_ref[...] = v
    vt_ref[...] = v.T.astype(BF16)
    lane = lax.broadcasted_iota(I32, (1, LANES), 1)
    is_ik = lane < IDX_DIM
    for h in range(IDX_HEADS):
        iqh = p[:, A_IQ + (h // 2) * LANES:A_IQ + (h // 2 + 1) * LANES]
        if h % 2:
            iqh = pltpu.roll(iqh, IDX_DIM, axis=1)
        iq_ref[h] = jnp.where(is_ik, iqh, 0.0).astype(BF16)
    ikw = p[:, A_IKW:A_IKW + LANES]
    ik = jnp.where(is_ik, ikw, 0.0)
    ikn = _rms(ik, IDX_DIM) * ikg_ref[...]
    out = jnp.where(is_ik, ikn, ikw * (IDX_HEADS ** -0.5 * IDX_DIM ** -0.5))
    ikw_ref[...] = out
    ikb_ref[...] = ikn.astype(BF16)
    iwt_ref[...] = out.T[IDX_DIM:IDX_DIM + IDX_HEADS, :]
    gate = p[:, A_GATE:A_END]
    sg_ref[...] = (gate * jax.nn.sigmoid(gate)).astype(BF16)


def _att_proj(x, norm_g, w_packed, q_gain, k_gain, ik_gain_pad, tm):
    n = x.shape[0]
    assert n % tm == 0 and tm % LANES == 0
    row = lambda i: (i, 0)
    const = lambda i: (0, 0)
    out_shape = (
        jax.ShapeDtypeStruct((n, D_MODEL), BF16),
        jax.ShapeDtypeStruct((n, KV_W), F32),
        jax.ShapeDtypeStruct((n, KV_W), BF16),
        jax.ShapeDtypeStruct((n, KV_W), F32),
        jax.ShapeDtypeStruct((KV_W, n), BF16),
        jax.ShapeDtypeStruct((IDX_HEADS, n, LANES), BF16),
        jax.ShapeDtypeStruct((n, LANES), F32),
        jax.ShapeDtypeStruct((n, LANES), BF16),
        jax.ShapeDtypeStruct((IDX_HEADS, n), F32),
        jax.ShapeDtypeStruct((n, D_MODEL), BF16),
    )
    out_specs = (
        pl.BlockSpec((tm, D_MODEL), row),
        pl.BlockSpec((tm, KV_W), row),
        pl.BlockSpec((tm, KV_W), row),
        pl.BlockSpec((tm, KV_W), row),
        pl.BlockSpec((KV_W, tm), lambda i: (0, i)),
        pl.BlockSpec((IDX_HEADS, tm, LANES), lambda i: (0, i, 0)),
        pl.BlockSpec((tm, LANES), row),
        pl.BlockSpec((tm, LANES), row),
        pl.BlockSpec((IDX_HEADS, tm), lambda i: (0, i)),
        pl.BlockSpec((tm, D_MODEL), row),
    )
    return pl.pallas_call(
        _att_proj_kernel,
        grid=(n // tm,),
        in_specs=[
            pl.BlockSpec((tm, D_MODEL), row),
            pl.BlockSpec((1, D_MODEL), const),
            pl.BlockSpec((D_MODEL, A_END), const),
            pl.BlockSpec((1, HEAD_DIM), const),
            pl.BlockSpec((1, HEAD_DIM), const),
            pl.BlockSpec((1, LANES), const),
        ],
        out_specs=out_specs,
        out_shape=out_shape,
        compiler_params=pltpu.CompilerParams(dimension_semantics=("parallel",), vmem_limit_bytes=VMEM_LIMIT),
        name="att_proj",
    )(x, norm_g, w_packed, q_gain, k_gain, ik_gain_pad)


def _reduce(x, axis, combine, finish):
    n = x.shape[axis]
    unit = 8 * (SUBLANES if axis == 0 else LANES)
    full = n // unit
    if full < 2:
        return finish(x, axis=axis, keepdims=True)
    cut = (lambda i, j: x[i:j, :]) if axis == 0 else (lambda i, j: x[:, i:j])
    acc = cut(0, unit)
    for i in range(1, full):
        acc = combine(acc, cut(i * unit, (i + 1) * unit))
    out = finish(acc, axis=axis, keepdims=True)
    if n % unit:
        out = combine(out, finish(cut(full * unit, n), axis=axis, keepdims=True))
    return out


def _count(mask, axis):
    return _reduce(jnp.where(mask, 1.0, 0.0), axis, jnp.add, jnp.sum)


def _select_bias(sc_ref, lo_ref, n_keys, key_idx, topk, axis):
    sl = (slice(0, n_keys), slice(None)) if axis == 0 else (slice(None), slice(0, n_keys))
    vec = (1, sc_ref.shape[1]) if axis == 0 else (sc_ref.shape[0], 1)
    kf = float(topk)

    def bit_step(i, prefix):
        cand = prefix | (jnp.int32(1) << (31 - i))
        cnt = _count(sc_ref[sl] >= _ordinal_to_float(cand), axis)
        return jnp.where(cnt >= kf, cand, prefix)

    prefix = lax.fori_loop(0, 32, bit_step, jnp.zeros(vec, I32))
    thr = _ordinal_to_float(prefix)
    sc = sc_ref[sl]
    few = jnp.logical_not(_count(sc >= thr, axis) >= kf)
    thr = jnp.where(few, -jnp.inf, thr)
    need = kf - _count(sc > thr, axis)
    excess = (_count(sc >= thr, axis) > kf) & jnp.logical_not(few)
    lo_ref[...] = jnp.full(vec, INT_MAX, I32)

    @pl.when(jnp.max(jnp.where(excess, 1.0, 0.0)) > 0.0)
    def _():
        nbits = max(1, (n_keys - 1).bit_length())

        def idx_step(i, lo):
            cand = lo + (jnp.int32(1) << (nbits - 1 - i))
            cnt = _count((sc_ref[sl] == thr) & (key_idx < cand), axis)
            return jnp.where(cnt < need, cand, lo)

        lo_ref[...] = lax.fori_loop(0, nbits, idx_step, jnp.zeros(vec, I32))

    lo = jnp.where(few, -1, lo_ref[...])
    sel = (sc > thr) | ((sc == thr) & (key_idx <= lo))
    return jnp.where(sel, 0.0, NEG_BIG)


def _att_prompt_body(n_keys, topk, q_ref, kb_ref, vt_ref, ikb_ref, iq_ref, iwt_ref, o_ref,
                     sc_ref, bias_ref, lo_ref):
    qb = q_ref.shape[0]
    t0 = pl.program_id(1) * qb
    ik = ikb_ref[0:n_keys, :]
    iwt = iwt_ref[...]
    score = jnp.zeros((n_keys, qb), F32)
    for h in range(IDX_HEADS):
        s = _dot_nt(ik, iq_ref[h])
        score = score + jnp.maximum(s, 0.0) * iwt[h:h + 1, :]
    key_idx = lax.broadcasted_iota(I32, (n_keys, qb), 0)
    q_pos = t0 + lax.broadcasted_iota(I32, (n_keys, qb), 1)
    sc_ref[0:n_keys, :] = jnp.where(key_idx <= q_pos, score, -jnp.inf)
    bias_ref[0:n_keys, :] = _select_bias(sc_ref, lo_ref, n_keys, key_idx, topk, axis=0)
    for h in range(N_HEADS):
        g = h // GROUP
        s = _dot_nt(kb_ref[0:n_keys, g * HEAD_DIM:(g + 1) * HEAD_DIM],
                    q_ref[:, h * HEAD_DIM:(h + 1) * HEAD_DIM])
        s = s + bias_ref[0:n_keys, :]
        m = _reduce(s, 0, jnp.maximum, jnp.max)
        p = jnp.exp(s - m)
        l = _reduce(p, 0, jnp.add, jnp.sum)
        ot = _dot(vt_ref[g * HEAD_DIM:(g + 1) * HEAD_DIM, 0:n_keys], p.astype(BF16))
        o_ref[:, h * HEAD_DIM:(h + 1) * HEAD_DIM] = (ot / l).T


def _att_prompt_kernel(q_ref, kb_ref, vt_ref, ikb_ref, iq_ref, iwt_ref, o_ref,
                       sc_ref, bias_ref, lo_ref, *, cls_len, n_cls, topk):
    qb = q_ref.shape[0]
    cls = (pl.program_id(1) * qb) // cls_len
    for c in range(n_cls):
        pl.when(cls == c)(functools.partial(
            _att_prompt_body, cls_len * (c + 1), topk, q_ref, kb_ref, vt_ref, ikb_ref, iq_ref, iwt_ref,
            o_ref, sc_ref, bias_ref, lo_ref))


def _att_prompt(q, kb, vt, ikb, iq, iwt, batch, seq, qb=LANES):
    nb = seq // qb
    n_cls = min(8, nb)
    assert seq % (n_cls * qb) == 0
    topk = min(TOPK_MAX, seq // 4)
    kern = functools.partial(_att_prompt_kernel, cls_len=seq // n_cls, n_cls=n_cls, topk=topk)
    return pl.pallas_call(
        kern,
        grid=(batch, nb),
        in_specs=[
            pl.BlockSpec((qb, D_MODEL), lambda b, j: (b * nb + j, 0)),
            pl.BlockSpec((seq, KV_W), lambda b, j: (b, 0)),
            pl.BlockSpec((KV_W, seq), lambda b, j: (0, b)),
            pl.BlockSpec((seq, LANES), lambda b, j: (b, 0)),
            pl.BlockSpec((IDX_HEADS, qb, LANES), lambda b, j: (0, b * nb + j, 0)),
            pl.BlockSpec((IDX_HEADS, qb), lambda b, j: (0, b * nb + j)),
        ],
        out_specs=pl.BlockSpec((qb, D_MODEL), lambda b, j: (b * nb + j, 0)),
        out_shape=jax.ShapeDtypeStruct((batch * seq, D_MODEL), F32),
        scratch_shapes=[
            pltpu.VMEM((seq, qb), F32),
            pltpu.VMEM((seq, qb), F32),
            pltpu.VMEM((1, qb), I32),
        ],
        compiler_params=pltpu.CompilerParams(dimension_semantics=("parallel", "arbitrary"),
                                             vmem_limit_bytes=VMEM_LIMIT),
        name="att_prompt",
    )(q, kb, vt, ikb, iq, iwt)


def _att_sample_kernel(pt_ref, q_ref, iq_ref, ikw_ref, ikn_ref, kn_ref, vn_ref,
                       cik_hbm, ck_hbm, cv_hbm, o_ref,
                       ikbuf, kbuf, vbuf, sc_ref, bias_ref, lo_ref, sem_i, sem_kv,
                       *, n_pages, chunk_pages, topk):
    b = pl.program_id(0)
    t_new = q_ref.shape[0]
    past = n_pages * PAGE_SIZE
    n_keys = past + PAGE_SIZE
    n_chunks = n_pages // chunk_pages
    ck = chunk_pages * PAGE_SIZE

    n_seq = pl.num_programs(0)
    islot = b % 2
    assert n_chunks % 2 == 0

    def idx_copy(seq, slot, p):
        return pltpu.make_async_copy(cik_hbm.at[pt_ref[seq, p]],
                                     ikbuf.at[slot, :, pl.ds(p * PAGE_SIZE, PAGE_SIZE)], sem_i.at[slot])

    def kv_copies(seq, c, p):
        page = pt_ref[seq, c * chunk_pages + p]
        dst = pl.ds(p * PAGE_SIZE * N_KV_HEADS, PAGE_SIZE * N_KV_HEADS)
        return (pltpu.make_async_copy(ck_hbm.at[page], kbuf.at[c % 2, dst], sem_kv.at[0, c % 2]),
                pltpu.make_async_copy(cv_hbm.at[page], vbuf.at[c % 2, dst], sem_kv.at[1, c % 2]))

    def start_idx(seq, slot):
        def go(p, carry):
            idx_copy(seq, slot, p).start()
            return carry
        lax.fori_loop(0, n_pages, go, 0, unroll=8)

    def wait_idx(seq, slot):
        def go(p, carry):
            idx_copy(seq, slot, p).wait()
            return carry
        lax.fori_loop(0, n_pages, go, 0, unroll=8)

    def start_chunk(seq, c):
        def go(p, carry):
            for cp in kv_copies(seq, c, p):
                cp.start()
            return carry
        lax.fori_loop(0, chunk_pages, go, 0, unroll=8)

    def wait_chunk(seq, c):
        def go(p, carry):
            for cp in kv_copies(seq, c, p):
                cp.wait()
            return carry
        lax.fori_loop(0, chunk_pages, go, 0, unroll=8)

    @pl.when(b == 0)
    def _():
        start_idx(b, islot)
        start_chunk(b, 0)

    @pl.when(b + 1 < n_seq)
    def _():
        start_idx(b + 1, 1 - islot)

    wait_idx(b, islot)

    iq2 = iq_ref[...].reshape(IDX_HEADS * t_new, LANES).astype(BF16)
    ikw = ikw_ref[...]

    def scores(ik_rows):
        s = _dot_nt(iq2, ik_rows)
        acc = jnp.zeros((t_new, ik_rows.shape[0]), F32)
        for h in range(IDX_HEADS):
            acc = acc + jnp.maximum(s[h * t_new:(h + 1) * t_new, :], 0.0) * ikw[:, IDX_DIM + h:IDX_DIM + h + 1]
        return acc

    iq2_past = iq2[:, 0:IDX_DIM]
    for c in range(n_chunks):
        s = _dot(iq2_past, ikbuf[islot, :, c * ck:(c + 1) * ck].astype(BF16))
        acc = jnp.zeros((t_new, ck), F32)
        for h in range(IDX_HEADS):
            acc = acc + jnp.maximum(s[h * t_new:(h + 1) * t_new, :], 0.0) * ikw[:, IDX_DIM + h:IDX_DIM + h + 1]
        sc_ref[:, c * ck:(c + 1) * ck] = acc
    new_idx = lax.broadcasted_iota(I32, (t_new, PAGE_SIZE), 1)
    tok = lax.broadcasted_iota(I32, (t_new, PAGE_SIZE), 0)
    sc_ref[:, past:n_keys] = jnp.where(new_idx <= tok, scores(ikn_ref[...]), -jnp.inf)
    key_idx = lax.broadcasted_iota(I32, (t_new, n_keys), 1)
    bias_ref[...] = _select_bias(sc_ref, lo_ref, n_keys, key_idx, topk, axis=1)

    rows = GROUP * t_new
    qs = [jnp.concatenate([q_ref[:, (g * GROUP + r) * HEAD_DIM:(g * GROUP + r + 1) * HEAD_DIM]
                           for r in range(GROUP)], axis=0).astype(BF16) for g in range(N_KV_HEADS)]
    m = [jnp.full((rows, 1), NEG_BIG, F32) for _ in range(N_KV_HEADS)]
    l = [jnp.zeros((rows, 1), F32) for _ in range(N_KV_HEADS)]
    acc = [jnp.zeros((rows, HEAD_DIM), F32) for _ in range(N_KV_HEADS)]

    def attend(g, k_rows, v_rows, bias):
        s = _dot_nt(qs[g], k_rows) + jnp.concatenate([bias] * GROUP, axis=0)
        m_new = jnp.maximum(m[g], _reduce(s, 1, jnp.maximum, jnp.max))
        alpha = jnp.exp(m[g] - m_new)
        p = jnp.exp(s - m_new)
        l[g] = alpha * l[g] + _reduce(p, 1, jnp.add, jnp.sum)
        acc[g] = alpha * acc[g] + _dot(p.astype(BF16), v_rows)
        m[g] = m_new

    for g in range(N_KV_HEADS):
        hd = slice(g * HEAD_DIM, (g + 1) * HEAD_DIM)
        attend(g, kn_ref[:, hd].astype(BF16), vn_ref[:, hd].astype(BF16), bias_ref[:, past:n_keys])
    for c in range(n_chunks):
        if c + 1 < n_chunks:
            start_chunk(b, c + 1)
        else:
            pl.when(b + 1 < n_seq)(functools.partial(start_chunk, b + 1, 0))
        wait_chunk(b, c)
        for g in range(N_KV_HEADS):
            head_rows = pl.ds(g, ck, stride=N_KV_HEADS)
            attend(g, kbuf[c % 2, head_rows, :].astype(BF16), vbuf[c % 2, head_rows, :].astype(BF16),
                   bias_ref[:, c * ck:(c + 1) * ck])
    for g in range(N_KV_HEADS):
        og = acc[g] / l[g]
        for r in range(GROUP):
            h = g * GROUP + r
            o_ref[:, h * HEAD_DIM:(h + 1) * HEAD_DIM] = og[r * t_new:(r + 1) * t_new, :]


def _att_sample(page_table, q, iq, ikw, ikn_pad, kn_pad, vn_pad, cache_ik, cache_k, cache_v, t_new):
    nseq, n_pages = page_table.shape
    assert t_new == SUBLANES
    chunk_pages = min(32, n_pages // 2)
    assert n_pages % chunk_pages == 0
    past = n_pages * PAGE_SIZE
    n_keys = past + PAGE_SIZE
    topk = min(TOPK_MAX, (past + t_new) // 4)
    ck = chunk_pages * PAGE_SIZE
    kern = functools.partial(_att_sample_kernel, n_pages=n_pages, chunk_pages=chunk_pages, topk=topk)
    grid_spec = pltpu.PrefetchScalarGridSpec(
        num_scalar_prefetch=1,
        grid=(nseq,),
        in_specs=[
            pl.BlockSpec((t_new, D_MODEL), lambda b, pt: (b, 0)),
            pl.BlockSpec((IDX_HEADS, t_new, LANES), lambda b, pt: (0, b, 0)),
            pl.BlockSpec((t_new, LANES), lambda b, pt: (b, 0)),
            pl.BlockSpec((None, PAGE_SIZE, LANES), lambda b, pt: (b, 0, 0)),
            pl.BlockSpec((None, PAGE_SIZE, KV_W), lambda b, pt: (b, 0, 0)),
            pl.BlockSpec((None, PAGE_SIZE, KV_W), lambda b, pt: (b, 0, 0)),
            pl.BlockSpec(memory_space=pl.ANY),
            pl.BlockSpec(memory_space=pl.ANY),
            pl.BlockSpec(memory_space=pl.ANY),
        ],
        out_specs=pl.BlockSpec((t_new, D_MODEL), lambda b, pt: (b, 0)),
        scratch_shapes=[
            pltpu.VMEM((2, IDX_DIM, past), F32),
            pltpu.VMEM((2, ck * N_KV_HEADS, HEAD_DIM), F32),
            pltpu.VMEM((2, ck * N_KV_HEADS, HEAD_DIM), F32),
            pltpu.VMEM((t_new, n_keys), F32),
            pltpu.VMEM((t_new, n_keys), F32),
            pltpu.VMEM((t_new, 1), I32),
            pltpu.SemaphoreType.DMA((2,)),
            pltpu.SemaphoreType.DMA((2, 2)),
        ],
    )
    return pl.pallas_call(
        kern,
        grid_spec=grid_spec,
        out_shape=jax.ShapeDtypeStruct((nseq * t_new, D_MODEL), F32),
        compiler_params=pltpu.CompilerParams(dimension_semantics=("arbitrary",), vmem_limit_bytes=VMEM_LIMIT),
        name="att_sample",
    )(page_table, q, iq, ikw, ikn_pad, kn_pad, vn_pad, cache_ik, cache_k, cache_v)


def _out_proj_kernel(o_ref, sg_ref, gain_ref, w_ref, x_ref, y_ref, *, head_norm):
    o = o_ref[...]
    if head_norm:
        o = jnp.concatenate(
            [_rms(o[:, h * DN_DV:(h + 1) * DN_DV], DN_DV) * gain_ref[...] for h in range(DN_HEADS)], axis=1)
    a = (o * sg_ref[...].astype(F32)).astype(BF16)
    y_ref[...] = x_ref[...] + _dot(a, w_ref[...])


def _out_proj(o, sg, gain, w_out, x, tm, head_norm):
    n = o.shape[0]
    row = lambda i: (i, 0)
    const = lambda i: (0, 0)
    return pl.pallas_call(
        functools.partial(_out_proj_kernel, head_norm=head_norm),
        grid=(n // tm,),
        in_specs=[
            pl.BlockSpec((tm, D_MODEL), row),
            pl.BlockSpec((tm, D_MODEL), row),
            pl.BlockSpec((1, DN_DV), const),
            pl.BlockSpec((D_MODEL, D_MODEL), const),
            pl.BlockSpec((tm, D_MODEL), row),
        ],
        out_specs=pl.BlockSpec((tm, D_MODEL), row),
        out_shape=jax.ShapeDtypeStruct((n, D_MODEL), F32),
        compiler_params=pltpu.CompilerParams(dimension_semantics=("parallel",), vmem_limit_bytes=VMEM_LIMIT),
        name="out_proj",
    )(o, sg, gain, w_out, x)


def _dn_proj_kernel(x_ref, g_ref, w_ref, cw_ref, buf_ref, alog_ref, dtb_ref,
                    q_ref, k_ref, v_ref, sz_ref, gb_ref, cbuf_ref, xp_ref, *, n_seq, pad):
    tm = x_ref.shape[0]

    @pl.when(pl.program_id(1) == 0)
    def _():
        xp_ref[0:pad, :] = buf_ref[...]

    xn = _rms(x_ref[...], D_MODEL) * g_ref[...]
    p = _dot(xn.astype(BF16), w_ref[...])
    xp_ref[pad:pad + tm, :] = p[:, 0:CONV_DIM]
    conv = xp_ref[pad:pad + tm, :] * cw_ref[CONV_W - 1:CONV_W, :]
    for i in range(1, CONV_W):
        conv = conv + xp_ref[pad - i * n_seq:pad - i * n_seq + tm, :] * cw_ref[CONV_W - 1 - i:CONV_W - i, :]
    act = conv * jax.nn.sigmoid(conv)
    for h in range(DN_HEADS):
        sl = slice(h * DN_DK, (h + 1) * DN_DK)
        qh = act[:, sl]
        q_ref[:, sl] = qh * lax.rsqrt(jnp.sum(qh * qh, axis=-1, keepdims=True) + EPS) * (DN_DK ** -0.5)
        kh = act[:, DN_HEADS * DN_DK + h * DN_DK:DN_HEADS * DN_DK + (h + 1) * DN_DK]
        k_ref[:, sl] = kh * lax.rsqrt(jnp.sum(kh * kh, axis=-1, keepdims=True) + EPS)
    v_ref[...] = act[:, 2 * DN_HEADS * DN_DK:CONV_DIM]
    z = p[:, N_Z:N_BA]
    sz_ref[...] = (z * jax.nn.sigmoid(z)).astype(BF16)
    ba = p[:, N_BA:N_END]
    sp_in = ba + dtb_ref[...]
    softplus = jnp.maximum(sp_in, 0.0) + jnp.log1p(jnp.exp(-jnp.abs(sp_in)))
    lane = lax.broadcasted_iota(I32, (1, LANES), 1)
    gb_ref[...] = jnp.where(lane < DN_HEADS, jax.nn.sigmoid(ba), -jnp.exp(alog_ref[...]) * softplus)
    cbuf_ref[...] = xp_ref[tm:tm + pad, :]
    xp_ref[0:pad, :] = xp_ref[tm:tm + pad, :]


def _dn_proj(x, norm_g, w_packed, conv_w, buf, alog_pad, dtb_pad, n_groups, n_seq, tm):
    n = x.shape[0]
    pad = buf.shape[1]
    tiles = n // n_groups // tm
    assert n == n_groups * tiles * tm and pad % SUBLANES == 0 and pad >= (CONV_W - 1) * n_seq
    row = lambda s, i: (s * tiles + i, 0)
    const = lambda s, i: (0, 0)
    big = lambda d: pl.BlockSpec((tm, d), row)
    return pl.pallas_call(
        functools.partial(_dn_proj_kernel, n_seq=n_seq, pad=pad),
        grid=(n_groups, tiles),
        in_specs=[
            big(D_MODEL),
            pl.BlockSpec((1, D_MODEL), const),
            pl.BlockSpec((D_MODEL, N_END), const),
            pl.BlockSpec((CONV_W, CONV_DIM), const),
            pl.BlockSpec((None, pad, CONV_DIM), lambda s, i: (s, 0, 0)),
            pl.BlockSpec((1, LANES), const),
            pl.BlockSpec((1, LANES), const),
        ],
        out_specs=(big(D_MODEL), big(D_MODEL), big(D_MODEL), big(D_MODEL), big(LANES),
                   pl.BlockSpec((None, pad, CONV_DIM), lambda s, i: (s, 0, 0))),
        out_shape=(
            jax.ShapeDtypeStruct((n, D_MODEL), F32),
            jax.ShapeDtypeStruct((n, D_MODEL), F32),
            jax.ShapeDtypeStruct((n, D_MODEL), F32),
            jax.ShapeDtypeStruct((n, D_MODEL), BF16),
            jax.ShapeDtypeStruct((n, LANES), F32),
            jax.ShapeDtypeStruct((n_groups, pad, CONV_DIM), F32),
        ),
        scratch_shapes=[pltpu.VMEM((pad + tm, CONV_DIM), F32)],
        compiler_params=pltpu.CompilerParams(dimension_semantics=("parallel", "arbitrary"),
                                             vmem_limit_bytes=VMEM_LIMIT),
        name="dn_proj",
    )(x, norm_g, w_packed, conv_w, buf, alog_pad, dtb_pad)


def _dn_rec_kernel(q_ref, k_ref, v_ref, gb_ref, s0_ref, o_ref, sout_ref, s_scr):
    c_len = q_ref.shape[0]

    @pl.when(pl.program_id(1) == 0)
    def _():
        s_scr[...] = s0_ref[...]

    ri = lax.broadcasted_iota(I32, (c_len, c_len), 0)
    ci = lax.broadcasted_iota(I32, (c_len, c_len), 1)
    incl = ri >= ci
    strict = ri > ci
    tril = jnp.where(incl, 1.0, 0.0)
    eye_c = jnp.where(ri == ci, 1.0, 0.0)
    eye_k = jnp.where(lax.broadcasted_iota(I32, (DN_DK, DN_DK), 0) == lax.broadcasted_iota(I32, (DN_DK, DN_DK), 1),
                      1.0, 0.0)
    lane = lax.broadcasted_iota(I32, (c_len, LANES), 1)
    n_sq = max(0, (c_len - 1).bit_length() - 1)

    tile_aligned = c_len % LANES == 0
    gbc = gb_ref[...]
    gcum = _dot(tril, gbc, HI)
    gcum_t = gcum.T if tile_aligned else None
    heads = range(DN_HEADS)
    hs = [slice(h * DN_DK, (h + 1) * DN_DK) for h in heads]
    beta = [gbc[:, h:h + 1] for h in heads]
    gcol = [gcum[:, DN_HEADS + h:DN_HEADS + h + 1] for h in heads]
    if tile_aligned:
        grow = [gcum_t[DN_HEADS + h:DN_HEADS + h + 1, :] for h in heads]
    else:
        grow = [_dot_nt(jnp.where(lane == DN_HEADS + h, 1.0, 0.0), gcum, HI) for h in heads]
    glast = [gcol[h][c_len - 1:c_len, :] for h in heads]
    decay = [jnp.where(incl, jnp.exp(jnp.where(incl, gcol[h] - grow[h], 0.0)), 0.0) for h in heads]
    eg = [jnp.exp(gcol[h]) for h in heads]
    a_mat = [beta[h] * _dot_nt(k_ref[:, hs[h]], k_ref[:, hs[h]]) * jnp.where(strict, decay[h], 0.0) for h in heads]
    pw = [_split_bf16(-a_mat[h]) for h in heads]
    t_inv = [eye_c - a_mat[h] for h in heads]
    for _ in range(n_sq):
        pw = [_split_bf16(_dot3(pw[h], pw[h])) for h in heads]
        t_inv = [t_inv[h] + _dot3(_split_bf16(t_inv[h]), pw[h]) for h in heads]
    wu = [_dot3(_split_bf16(t_inv[h]),
                _split_bf16(jnp.concatenate([beta[h] * eg[h] * k_ref[:, hs[h]], beta[h] * v_ref[:, hs[h]]], axis=1)))
          for h in heads]
    qk = [_dot_nt(q_ref[:, hs[h]], k_ref[:, hs[h]]) * decay[h] for h in heads]
    kd = [k_ref[:, hs[h]] * jnp.exp(glast[h] - gcol[h]) for h in heads]
    kdt = [kd[h].T if tile_aligned else _dot_nt(eye_k, kd[h], HI) for h in heads]
    for h in heads:
        s_old = s_scr[h]
        u = wu[h][:, DN_DK:DN_DK + DN_DV] - _dot(wu[h][:, 0:DN_DK], s_old)
        o_ref[:, hs[h]] = eg[h] * _dot(q_ref[:, hs[h]], s_old) + _dot(qk[h], u)
        s_scr[h] = jnp.exp(glast[h]) * s_old + _dot(kdt[h], u)

    @pl.when(pl.program_id(1) == pl.num_programs(1) - 1)
    def _():
        sout_ref[...] = s_scr[...]


def _dn_rec(q, k, v, gb, s0, n_seq, t_len, chunk):
    assert t_len % chunk == 0
    n_chunks = t_len // chunk
    row = lambda b, c: (b * n_chunks + c, 0)
    st = pl.BlockSpec((None, None, DN_HEADS, DN_DK, DN_DV), lambda b, c: (b, 0, 0, 0, 0))
    return pl.pallas_call(
        _dn_rec_kernel,
        grid=(n_seq, n_chunks),
        in_specs=[pl.BlockSpec((chunk, D_MODEL), row)] * 3 + [pl.BlockSpec((chunk, LANES), row), st],
        out_specs=(pl.BlockSpec((chunk, D_MODEL), row), st),
        out_shape=(jax.ShapeDtypeStruct((n_seq * t_len, D_MODEL), F32),
                   jax.ShapeDtypeStruct((n_seq, 1, DN_HEADS, DN_DK, DN_DV), F32)),
        scratch_shapes=[pltpu.VMEM((DN_HEADS, DN_DK, DN_DV), F32)],
        compiler_params=pltpu.CompilerParams(dimension_semantics=("parallel", "arbitrary"),
                                             vmem_limit_bytes=VMEM_LIMIT),
        name="dn_rec",
    )(q, k, v, gb, s0)


def _pad_lanes(v, offset=0):
    return jnp.zeros((1, LANES), F32).at[0, offset:offset + v.shape[0]].set(v)


def _pack_att_w(w):
    d = w.shape[0]
    gate0 = 2 * D_MODEL + 2 * KV_W + IDX_HEADS * IDX_DIM + IDX_DIM + IDX_HEADS - D_MODEL
    return jnp.concatenate(
        [w[:, :gate0], jnp.zeros((d, A_GATE - gate0), w.dtype), w[:, gate0:]], axis=1).astype(BF16)


def _pack_dn_w(w):
    d = w.shape[0]
    return jnp.concatenate([w, jnp.zeros((d, N_END - w.shape[1]), w.dtype)], axis=1).astype(BF16)


def kernel(x_prompt, x_sample, cache_k, cache_v, cache_idx_k, state_dn_S, state_dn_conv, page_table,
           att_norm, att_w_in, att_q_gain, att_k_gain, att_ik_gain, att_w_out,
           dn_norm, dn_w_in, dn_conv_w, dn_A_log, dn_dt_bias, dn_o_gain, dn_w_out):
    bp, seq, d = x_prompt.shape
    bs, t_new, _ = x_sample.shape
    n_pool = cache_k.shape[0]
    np_rows, ns_rows = bp * seq, bs * t_new
    tm = 256

    xp = x_prompt.reshape(np_rows, d)
    xs = x_sample.reshape(ns_rows, d)

    w_att = _pack_att_w(att_w_in[0])
    a_norm = att_norm[0][None, :]
    qg, kg = att_q_gain[0][None, :], att_k_gain[0][None, :]
    ikg = _pad_lanes(att_ik_gain[0])
    w_ao = att_w_out[0].astype(BF16)
    ones_gain = jnp.ones((1, DN_DV), F32)

    q_p, k_p, kb_p, v_p, vt_p, iq_p, ikw_p, ikb_p, iwt_p, sg_p = _att_proj(xp, a_norm, w_att, qg, kg, ikg, tm)
    o_p = _att_prompt(q_p, kb_p, vt_p, ikb_p, iq_p, iwt_p, bp, seq)
    y1_p = _out_proj(o_p, sg_p, ones_gain, w_ao, xp, tm, head_norm=False)

    q_s, k_s, _, v_s, _, iq_s, ikw_s, ikb_s, _, sg_s = _att_proj(xs, a_norm, w_att, qg, kg, ikg, min(tm, ns_rows))
    pad_new = lambda a: jnp.pad(a.reshape(bs, t_new, a.shape[-1]), ((0, 0), (0, PAGE_SIZE - t_new), (0, 0)))
    o_s = _att_sample(page_table, q_s.astype(F32), iq_s.astype(F32), ikw_s, pad_new(ikb_s), pad_new(k_s), pad_new(v_s),
                      jnp.transpose(cache_idx_k, (0, 2, 3, 1)).reshape(n_pool, IDX_DIM, PAGE_SIZE),
                      cache_k.reshape(n_pool, PAGE_SIZE * N_KV_HEADS, HEAD_DIM),
                      cache_v.reshape(n_pool, PAGE_SIZE * N_KV_HEADS, HEAD_DIM), t_new)
    y1_s = _out_proj(o_s, sg_s, ones_gain, w_ao, xs, min(tm, ns_rows), head_norm=False)

    w_dn = _pack_dn_w(dn_w_in[0])
    d_norm = dn_norm[0][None, :]
    alog = _pad_lanes(dn_A_log[0], DN_HEADS)
    dtb = _pad_lanes(dn_dt_bias[0], DN_HEADS)
    o_gain = dn_o_gain[0][None, :]
    w_do = dn_w_out[0].astype(BF16)
    hist = CONV_W - 1

    buf_p = jnp.zeros((bp, SUBLANES, CONV_DIM), F32)
    dq_p, dk_p, dv_p, sz_p, gb_p, cb_p = _dn_proj(y1_p, d_norm, w_dn, dn_conv_w[0], buf_p, alog, dtb,
                                                   n_groups=bp, n_seq=1, tm=tm)
    s0_p = jnp.zeros((bp, 1, DN_HEADS, DN_DK, DN_DV), F32)
    do_p, s_p = _dn_rec(dq_p, dk_p, dv_p, gb_p, s0_p, bp, seq, chunk=LANES)
    y2_p = _out_proj(do_p, sz_p, o_gain, w_do, y1_p, tm, head_norm=True)

    to_tm = lambda a: a.reshape(bs, t_new, -1).transpose(1, 0, 2).reshape(ns_rows, -1)
    to_bm = lambda a: a.reshape(t_new, bs, -1).transpose(1, 0, 2).reshape(ns_rows, -1)
    buf_s = state_dn_conv[:, 0].transpose(1, 0, 2).reshape(1, hist * bs, CONV_DIM)
    dq_s, dk_s, dv_s, sz_s, gb_s, cb_s = _dn_proj(to_tm(y1_s), d_norm, w_dn, dn_conv_w[0], buf_s, alog, dtb,
                                                   n_groups=1, n_seq=bs, tm=ns_rows)
    do_s, s_s = _dn_rec(to_bm(dq_s), to_bm(dk_s), to_bm(dv_s), to_bm(gb_s), state_dn_S, bs, t_new, chunk=t_new)
    y2_s = _out_proj(do_s, to_bm(sz_s), o_gain, w_do, y1_s, min(tm, ns_rows), head_norm=True)

    return (
        y2_p.reshape(bp, seq, d),
        y2_s.reshape(bs, t_new, d),
        k_p.reshape(bp, seq, 1, N_KV_HEADS, HEAD_DIM),
        v_p.reshape(bp, seq, 1, N_KV_HEADS, HEAD_DIM),
        ikw_p[:, :IDX_DIM].reshape(bp, seq, 1, IDX_DIM),
        k_s.reshape(bs, t_new, 1, N_KV_HEADS, HEAD_DIM),
        v_s.reshape(bs, t_new, 1, N_KV_HEADS, HEAD_DIM),
        ikw_s[:, :IDX_DIM].reshape(bs, t_new, 1, IDX_DIM),
        s_p,
        cb_p[:, SUBLANES - hist:, :].reshape(bp, 1, hist, CONV_DIM),
        s_s,
        cb_s.reshape(hist, bs, CONV_DIM).transpose(1, 0, 2).reshape(bs, 1, hist, CONV_DIM),
    )
```

```python
import functools

import jax
import jax.numpy as jnp
from jax import lax
from jax.experimental import pallas as pl
from jax.experimental.pallas import tpu as pltpu

F32 = jnp.float32
BF16 = jnp.bfloat16
I32 = jnp.int32

EPS = 1e-6
LANES = 128
SUBLANES = 8
VMEM_LIMIT = 56 * 1024 * 1024

D_MODEL = 1024
N_HEADS = 8
HEAD_DIM = 128
N_KV_HEADS = 2
GROUP = N_HEADS // N_KV_HEADS
IDX_HEADS = 8
IDX_DIM = 64
TOPK_MAX = 256
PAGE_SIZE = 128
DN_HEADS = 8
DN_DK = 128
DN_DV = 128
CONV_W = 4
KV_W = N_KV_HEADS * HEAD_DIM
CONV_DIM = 3 * DN_HEADS * DN_DK

A_Q, A_K, A_V, A_IQ, A_IKW, A_GATE, A_END = 0, 1024, 1280, 1536, 2048, 2176, 3200
N_Z, N_BA, N_END = 3072, 4096, 4224

INT_MIN = -2147483648
INT_MAX = 2147483647
NEG_BIG = -1e30
NT_DIMS = (((1,), (1,)), ((), ()))
HI = lax.Precision.HIGHEST


def _dot(a, b, precision=None):
    return jnp.dot(a, b, preferred_element_type=F32, precision=precision)


def _dot_nt(a, b, precision=None):
    return lax.dot_general(a, b, NT_DIMS, preferred_element_type=F32, precision=precision)


def _split_bf16(x):
    hi = x.astype(BF16)
    return hi, (x - hi.astype(F32)).astype(BF16)


def _dot3(a, b):
    (ah, al), (bh, bl) = a, b
    return _dot(ah, bh) + (_dot(ah, bl) + _dot(al, bh))


def _rms(x, n):
    return x * lax.rsqrt(jnp.sum(x * x, axis=-1, keepdims=True) * (1.0 / n) + EPS)


def _ordinal_to_float(u):
    key = u ^ jnp.int32(INT_MIN)
    return pltpu.bitcast(jnp.where(key < 0, key ^ jnp.int32(INT_MAX), key), F32)


def _att_proj_kernel(x_ref, g_ref, w_ref, qg_ref, kg_ref, ikg_ref,
                     q_ref, k_ref, kb_ref, v_ref, vt_ref, iq_ref, ikw_ref, ikb_ref, iwt_ref, sg_ref):
    x = x_ref[...]
    xn = _rms(x, D_MODEL) * g_ref[...]
    p = _dot(xn.astype(BF16), w_ref[...])
    for h in range(N_HEADS):
        qh = _rms(p[:, A_Q + h * HEAD_DIM:A_Q + (h + 1) * HEAD_DIM], HEAD_DIM) * qg_ref[...]
        q_ref[:, h * HEAD_DIM:(h + 1) * HEAD_DIM] = (qh * (HEAD_DIM ** -0.5)).astype(BF16)
    for g in range(N_KV_HEADS):
        kh = _rms(p[:, A_K + g * HEAD_DIM:A_K + (g + 1) * HEAD_DIM], HEAD_DIM) * kg_ref[...]
        k_ref[:, g * HEAD_DIM:(g + 1) * HEAD_DIM] = kh
        kb_ref[:, g * HEAD_DIM:(g + 1) * HEAD_DIM] = kh.astype(BF16)
    v = p[:, A_V:A_V + KV_W]
    v_ref[...] = v
    vt_ref[...] = v.T.astype(BF16)
    lane = lax.broadcasted_iota(I32, (1, LANES), 1)
    is_ik = lane < IDX_DIM
    for h in range(IDX_HEADS):
        iqh = p[:, A_IQ + (h // 2) * LANES:A_IQ + (h // 2 + 1) * LANES]
        if h % 2:
            iqh = pltpu.roll(iqh, IDX_DIM, axis=1)
        iq_ref[h] = jnp.where(is_ik, iqh, 0.0).astype(BF16)
    ikw = p[:, A_IKW:A_IKW + LANES]
    ik = jnp.where(is_ik, ikw, 0.0)
    ikn = _rms(ik, IDX_DIM) * ikg_ref[...]
    out = jnp.where(is_ik, ikn, ikw * (IDX_HEADS ** -0.5 * IDX_DIM ** -0.5))
    ikw_ref[...] = out
    ikb_ref[...] = ikn.astype(BF16)
    iwt_ref[...] = out.T[IDX_DIM:IDX_DIM + IDX_HEADS, :]
    gate = p[:, A_GATE:A_END]
    sg_ref[...] = (gate * jax.nn.sigmoid(gate)).astype(BF16)


def _att_proj(x, norm_g, w_packed, q_gain, k_gain, ik_gain_pad, tm):
    n = x.shape[0]
    assert n % tm == 0 and tm % LANES == 0
    row = lambda i: (i, 0)
    const = lambda i: (0, 0)
    out_shape = (
        jax.ShapeDtypeStruct((n, D_MODEL), BF16),
        jax.ShapeDtypeStruct((n, KV_W), F32),
        jax.ShapeDtypeStruct((n, KV_W), BF16),
        jax.ShapeDtypeStruct((n, KV_W), F32),
        jax.ShapeDtypeStruct((KV_W, n), BF16),
        jax.ShapeDtypeStruct((IDX_HEADS, n, LANES), BF16),
        jax.ShapeDtypeStruct((n, LANES), F32),
        jax.ShapeDtypeStruct((n, LANES), BF16),
        jax.ShapeDtypeStruct((IDX_HEADS, n), F32),
        jax.ShapeDtypeStruct((n, D_MODEL), BF16),
    )
    out_specs = (
        pl.BlockSpec((tm, D_MODEL), row),
        pl.BlockSpec((tm, KV_W), row),
        pl.BlockSpec((tm, KV_W), row),
        pl.BlockSpec((tm, KV_W), row),
        pl.BlockSpec((KV_W, tm), lambda i: (0, i)),
        pl.BlockSpec((IDX_HEADS, tm, LANES), lambda i: (0, i, 0)),
        pl.BlockSpec((tm, LANES), row),
        pl.BlockSpec((tm, LANES), row),
        pl.BlockSpec((IDX_HEADS, tm), lambda i: (0, i)),
        pl.BlockSpec((tm, D_MODEL), row),
    )
    return pl.pallas_call(
        _att_proj_kernel,
        grid=(n // tm,),
        in_specs=[
            pl.BlockSpec((tm, D_MODEL), row),
            pl.BlockSpec((1, D_MODEL), const),
            pl.BlockSpec((D_MODEL, A_END), const),
            pl.BlockSpec((1, HEAD_DIM), const),
            pl.BlockSpec((1, HEAD_DIM), const),
            pl.BlockSpec((1, LANES), const),
        ],
        out_specs=out_specs,
        out_shape=out_shape,
        compiler_params=pltpu.CompilerParams(dimension_semantics=("parallel",), vmem_limit_bytes=VMEM_LIMIT),
        name="att_proj",
    )(x, norm_g, w_packed, q_gain, k_gain, ik_gain_pad)


def _reduce(x, axis, combine, finish):
    n = x.shape[axis]
    unit = 8 * (SUBLANES if axis == 0 else LANES)
    full = n // unit
    if full < 2:
        return finish(x, axis=axis, keepdims=True)
    cut = (lambda i, j: x[i:j, :]) if axis == 0 else (lambda i, j: x[:, i:j])
    acc = cut(0, unit)
    for i in range(1, full):
        acc = combine(acc, cut(i * unit, (i + 1) * unit))
    out = finish(acc, axis=axis, keepdims=True)
    if n % unit:
        out = combine(out, finish(cut(full * unit, n), axis=axis, keepdims=True))
    return out


def _count(mask, axis):
    return _reduce(jnp.where(mask, 1.0, 0.0), axis, jnp.add, jnp.sum)


def _select_bias(sc_ref, lo_ref, n_keys, key_idx, topk, axis):
    sl = (slice(0, n_keys), slice(None)) if axis == 0 else (slice(None), slice(0, n_keys))
    vec = (1, sc_ref.shape[1]) if axis == 0 else (sc_ref.shape[0], 1)
    kf = float(topk)

    def bit_step(i, prefix):
        cand = prefix | (jnp.int32(1) << (31 - i))
        cnt = _count(sc_ref[sl] >= _ordinal_to_float(cand), axis)
        return jnp.where(cnt >= kf, cand, prefix)

    prefix = lax.fori_loop(0, 32, bit_step, jnp.zeros(vec, I32))
    thr = _ordinal_to_float(prefix)
    sc = sc_ref[sl]
    few = jnp.logical_not(_count(sc >= thr, axis) >= kf)
    thr = jnp.where(few, -jnp.inf, thr)
    need = kf - _count(sc > thr, axis)
    excess = (_count(sc >= thr, axis) > kf) & jnp.logical_not(few)
    lo_ref[...] = jnp.full(vec, INT_MAX, I32)

    @pl.when(jnp.max(jnp.where(excess, 1.0, 0.0)) > 0.0)
    def _():
        nbits = max(1, (n_keys - 1).bit_length())

        def idx_step(i, lo):
            cand = lo + (jnp.int32(1) << (nbits - 1 - i))
            cnt = _count((sc_ref[sl] == thr) & (key_idx < cand), axis)
            return jnp.where(cnt < need, cand, lo)

        lo_ref[...] = lax.fori_loop(0, nbits, idx_step, jnp.zeros(vec, I32))

    lo = jnp.where(few, -1, lo_ref[...])
    sel = (sc > thr) | ((sc == thr) & (key_idx <= lo))
    return jnp.where(sel, 0.0, NEG_BIG)


def _att_prompt_body(n_keys, topk, q_ref, kb_ref, vt_ref, ikb_ref, iq_ref, iwt_ref, o_ref,
                     sc_ref, bias_ref, lo_ref):
    qb = q_ref.shape[0]
    t0 = pl.program_id(1) * qb
    ik = ikb_ref[0:n_keys, :]
    iwt = iwt_ref[...]
    score = jnp.zeros((n_keys, qb), F32)
    for h in range(IDX_HEADS):
        s = _dot_nt(ik, iq_ref[h])
        score = score + jnp.maximum(s, 0.0) * iwt[h:h + 1, :]
    key_idx = lax.broadcasted_iota(I32, (n_keys, qb), 0)
    q_pos = t0 + lax.broadcasted_iota(I32, (n_keys, qb), 1)
    sc_ref[0:n_keys, :] = jnp.where(key_idx <= q_pos, score, -jnp.inf)
    bias_ref[0:n_keys, :] = _select_bias(sc_ref, lo_ref, n_keys, key_idx, topk, axis=0)
    for h in range(N_HEADS):
        g = h // GROUP
        s = _dot_nt(kb_ref[0:n_keys, g * HEAD_DIM:(g + 1) * HEAD_DIM],
                    q_ref[:, h * HEAD_DIM:(h + 1) * HEAD_DIM])
        s = s + bias_ref[0:n_keys, :]
        m = _reduce(s, 0, jnp.maximum, jnp.max)
        p = jnp.exp(s - m)
        l = _reduce(p, 0, jnp.add, jnp.sum)
        ot = _dot(vt_ref[g * HEAD_DIM:(g + 1) * HEAD_DIM, 0:n_keys], p.astype(BF16))
        o_ref[:, h * HEAD_DIM:(h + 1) * HEAD_DIM] = (ot / l).T


def _att_prompt_kernel(q_ref, kb_ref, vt_ref, ikb_ref, iq_ref, iwt_ref, sg_ref, w_ref, x_ref, y_ref,
                       o_scr, sc_ref, bias_ref, lo_ref, *, cls_len, n_cls, topk):
    qb = q_ref.shape[0]
    cls = (pl.program_id(1) * qb) // cls_len
    for c in range(n_cls):
        pl.when(cls == c)(functools.partial(
            _att_prompt_body, cls_len * (c + 1), topk, q_ref, kb_ref, vt_ref, ikb_ref, iq_ref, iwt_ref,
            o_scr, sc_ref, bias_ref, lo_ref))
    a = (o_scr[...] * sg_ref[...].astype(F32)).astype(BF16)
    y_ref[...] = x_ref[...] + _dot(a, w_ref[...])


def _att_prompt(q, kb, vt, ikb, iq, iwt, sg, w_out, x, batch, seq, qb=LANES):
    nb = seq // qb
    n_cls = min(8, nb)
    assert seq % (n_cls * qb) == 0
    topk = min(TOPK_MAX, seq // 4)
    kern = functools.partial(_att_prompt_kernel, cls_len=seq // n_cls, n_cls=n_cls, topk=topk)
    return pl.pallas_call(
        kern,
        grid=(batch, nb),
        in_specs=[
            pl.BlockSpec((qb, D_MODEL), lambda b, j: (b * nb + j, 0)),
            pl.BlockSpec((seq, KV_W), lambda b, j: (b, 0)),
            pl.BlockSpec((KV_W, seq), lambda b, j: (0, b)),
            pl.BlockSpec((seq, LANES), lambda b, j: (b, 0)),
            pl.BlockSpec((IDX_HEADS, qb, LANES), lambda b, j: (0, b * nb + j, 0)),
            pl.BlockSpec((IDX_HEADS, qb), lambda b, j: (0, b * nb + j)),
            pl.BlockSpec((qb, D_MODEL), lambda b, j: (b * nb + j, 0)),
            pl.BlockSpec((D_MODEL, D_MODEL), lambda b, j: (0, 0)),
            pl.BlockSpec((qb, D_MODEL), lambda b, j: (b * nb + j, 0)),
        ],
        out_specs=pl.BlockSpec((qb, D_MODEL), lambda b, j: (b * nb + j, 0)),
        out_shape=jax.ShapeDtypeStruct((batch * seq, D_MODEL), F32),
        scratch_shapes=[
            pltpu.VMEM((qb, D_MODEL), F32),
            pltpu.VMEM((seq, qb), F32),
            pltpu.VMEM((seq, qb), F32),
            pltpu.VMEM((1, qb), I32),
        ],
        compiler_params=pltpu.CompilerParams(dimension_semantics=("parallel", "arbitrary"),
                                             vmem_limit_bytes=VMEM_LIMIT),
        name="att_prompt",
    )(q, kb, vt, ikb, iq, iwt, sg, w_out, x)


def _att_sample_kernel(pt_ref, q_ref, iq_ref, ikw_ref, ikn_ref, kn_ref, vn_ref,
                       cik_hbm, ck_hbm, cv_hbm, o_ref,
                       ikbuf, kbuf, vbuf, sc_ref, bias_ref, lo_ref, sem_i, sem_kv,
                       *, n_pages, chunk_pages, topk):
    b = pl.program_id(0)
    t_new = q_ref.shape[0]
    past = n_pages * PAGE_SIZE
    n_keys = past + PAGE_SIZE
    n_chunks = n_pages // chunk_pages
    ck = chunk_pages * PAGE_SIZE

    n_seq = pl.num_programs(0)
    islot = b % 2
    ahead = n_chunks - 1

    def idx_copy(seq, slot, p):
        return pltpu.make_async_copy(cik_hbm.at[pt_ref[seq, p]],
                                     ikbuf.at[slot, :, pl.ds(p * PAGE_SIZE, PAGE_SIZE)], sem_i.at[slot])

    def kv_copies(seq, c, p):
        page = pt_ref[seq, c * chunk_pages + p]
        dst = pl.ds(p * PAGE_SIZE * N_KV_HEADS, PAGE_SIZE * N_KV_HEADS)
        return (pltpu.make_async_copy(ck_hbm.at[page], kbuf.at[c, dst], sem_kv.at[0, c]),
                pltpu.make_async_copy(cv_hbm.at[page], vbuf.at[c, dst], sem_kv.at[1, c]))

    def start_idx(seq, slot):
        def go(p, carry):
            idx_copy(seq, slot, p).start()
            return carry
        lax.fori_loop(0, n_pages, go, 0, unroll=8)

    def wait_idx(seq, slot):
        def go(p, carry):
            idx_copy(seq, slot, p).wait()
            return carry
        lax.fori_loop(0, n_pages, go, 0, unroll=8)

    def start_chunk(seq, c):
        def go(p, carry):
            for cp in kv_copies(seq, c, p):
                cp.start()
            return carry
        lax.fori_loop(0, chunk_pages, go, 0, unroll=8)

    def wait_chunk(seq, c):
        def go(p, carry):
            for cp in kv_copies(seq, c, p):
                cp.wait()
            return carry
        lax.fori_loop(0, chunk_pages, go, 0, unroll=8)

    @pl.when(b == 0)
    def _():
        start_idx(b, islot)
        for c in range(ahead):
            start_chunk(b, c)

    @pl.when(b + 1 < n_seq)
    def _():
        start_idx(b + 1, 1 - islot)

    wait_idx(b, islot)

    iq2 = iq_ref[...].reshape(IDX_HEADS * t_new, LANES).astype(BF16)
    ikw = ikw_ref[...]

    def scores(ik_rows):
        s = _dot_nt(iq2, ik_rows)
        acc = jnp.zeros((t_new, ik_rows.shape[0]), F32)
        for h in range(IDX_HEADS):
            acc = acc + jnp.maximum(s[h * t_new:(h + 1) * t_new, :], 0.0) * ikw[:, IDX_DIM + h:IDX_DIM + h + 1]
        return acc

    iq2_past = iq2[:, 0:IDX_DIM]
    for c in range(n_chunks):
        s = _dot(iq2_past, ikbuf[islot, :, c * ck:(c + 1) * ck].astype(BF16))
        acc = jnp.zeros((t_new, ck), F32)
        for h in range(IDX_HEADS):
            acc = acc + jnp.maximum(s[h * t_new:(h + 1) * t_new, :], 0.0) * ikw[:, IDX_DIM + h:IDX_DIM + h + 1]
        sc_ref[:, c * ck:(c + 1) * ck] = acc
    new_idx = lax.broadcasted_iota(I32, (t_new, PAGE_SIZE), 1)
    tok = lax.broadcasted_iota(I32, (t_new, PAGE_SIZE), 0)
    sc_ref[:, past:n_keys] = jnp.where(new_idx <= tok, scores(ikn_ref[...]), -jnp.inf)
    key_idx = lax.broadcasted_iota(I32, (t_new, n_keys), 1)
    bias_ref[...] = _select_bias(sc_ref, lo_ref, n_keys, key_idx, topk, axis=1)

    rows = GROUP * t_new
    qs = [jnp.concatenate([q_ref[:, (g * GROUP + r) * HEAD_DIM:(g * GROUP + r + 1) * HEAD_DIM]
                           for r in range(GROUP)], axis=0).astype(BF16) for g in range(N_KV_HEADS)]
    m = [jnp.full((rows, 1), NEG_BIG, F32) for _ in range(N_KV_HEADS)]
    l = [jnp.zeros((rows, 1), F32) for _ in range(N_KV_HEADS)]
    acc = [jnp.zeros((rows, HEAD_DIM), F32) for _ in range(N_KV_HEADS)]

    def attend(g, k_rows, v_rows, bias):
        s = _dot_nt(qs[g], k_rows) + jnp.concatenate([bias] * GROUP, axis=0)
        m_new = jnp.maximum(m[g], _reduce(s, 1, jnp.maximum, jnp.max))
        alpha = jnp.exp(m[g] - m_new)
        p = jnp.exp(s - m_new)
        l[g] = alpha * l[g] + _reduce(p, 1, jnp.add, jnp.sum)
        acc[g] = alpha * acc[g] + _dot(p.astype(BF16), v_rows)
        m[g] = m_new

    for g in range(N_KV_HEADS):
        hd = slice(g * HEAD_DIM, (g + 1) * HEAD_DIM)
        attend(g, kn_ref[:, hd].astype(BF16), vn_ref[:, hd].astype(BF16), bias_ref[:, past:n_keys])
    for c in range(n_chunks):
        nxt = c + ahead
        if nxt < n_chunks:
            start_chunk(b, nxt)
        else:
            pl.when(b + 1 < n_seq)(functools.partial(start_chunk, b + 1, nxt - n_chunks))
        wait_chunk(b, c)
        for g in range(N_KV_HEADS):
            head_rows = pl.ds(g, ck, stride=N_KV_HEADS)
            attend(g, kbuf[c, head_rows, :].astype(BF16), vbuf[c, head_rows, :].astype(BF16),
                   bias_ref[:, c * ck:(c + 1) * ck])
    for g in range(N_KV_HEADS):
        og = acc[g] / l[g]
        for r in range(GROUP):
            h = g * GROUP + r
            o_ref[:, h * HEAD_DIM:(h + 1) * HEAD_DIM] = og[r * t_new:(r + 1) * t_new, :]


def _att_sample(page_table, q, iq, ikw, ikn_pad, kn_pad, vn_pad, cache_ik, cache_k, cache_v, t_new):
    nseq, n_pages = page_table.shape
    assert t_new == SUBLANES
    chunk_pages = min(32, n_pages // 2)
    assert n_pages % chunk_pages == 0
    past = n_pages * PAGE_SIZE
    n_keys = past + PAGE_SIZE
    topk = min(TOPK_MAX, (past + t_new) // 4)
    ck = chunk_pages * PAGE_SIZE
    kern = functools.partial(_att_sample_kernel, n_pages=n_pages, chunk_pages=chunk_pages, topk=topk)
    grid_spec = pltpu.PrefetchScalarGridSpec(
        num_scalar_prefetch=1,
        grid=(nseq,),
        in_specs=[
            pl.BlockSpec((t_new, D_MODEL), lambda b, pt: (b, 0)),
            pl.BlockSpec((IDX_HEADS, t_new, LANES), lambda b, pt: (0, b, 0)),
            pl.BlockSpec((t_new, LANES), lambda b, pt: (b, 0)),
            pl.BlockSpec((None, PAGE_SIZE, LANES), lambda b, pt: (b, 0, 0)),
            pl.BlockSpec((None, PAGE_SIZE, KV_W), lambda b, pt: (b, 0, 0)),
            pl.BlockSpec((None, PAGE_SIZE, KV_W), lambda b, pt: (b, 0, 0)),
            pl.BlockSpec(memory_space=pl.ANY),
            pl.BlockSpec(memory_space=pl.ANY),
            pl.BlockSpec(memory_space=pl.ANY),
        ],
        out_specs=pl.BlockSpec((t_new, D_MODEL), lambda b, pt: (b, 0)),
        scratch_shapes=[
            pltpu.VMEM((2, IDX_DIM, past), F32),
            pltpu.VMEM((n_pages // chunk_pages, ck * N_KV_HEADS, HEAD_DIM), F32),
            pltpu.VMEM((n_pages // chunk_pages, ck * N_KV_HEADS, HEAD_DIM), F32),
            pltpu.VMEM((t_new, n_keys), F32),
            pltpu.VMEM((t_new, n_keys), F32),
            pltpu.VMEM((t_new, 1), I32),
            pltpu.SemaphoreType.DMA((2,)),
            pltpu.SemaphoreType.DMA((2, n_pages // chunk_pages)),
        ],
    )
    return pl.pallas_call(
        kern,
        grid_spec=grid_spec,
        out_shape=jax.ShapeDtypeStruct((nseq * t_new, D_MODEL), F32),
        compiler_params=pltpu.CompilerParams(dimension_semantics=("arbitrary",), vmem_limit_bytes=VMEM_LIMIT),
        name="att_sample",
    )(page_table, q, iq, ikw, ikn_pad, kn_pad, vn_pad, cache_ik, cache_k, cache_v)


def _out_proj_kernel(o_ref, sg_ref, w_ref, x_ref, y_ref):
    a = (o_ref[...] * sg_ref[...].astype(F32)).astype(BF16)
    y_ref[...] = x_ref[...] + _dot(a, w_ref[...])


def _out_proj(o, sg, w_out, x, tm):
    n = o.shape[0]
    row = pl.BlockSpec((tm, D_MODEL), lambda i: (i, 0))
    return pl.pallas_call(
        _out_proj_kernel,
        grid=(n // tm,),
        in_specs=[row, row, pl.BlockSpec((D_MODEL, D_MODEL), lambda i: (0, 0)), row],
        out_specs=row,
        out_shape=jax.ShapeDtypeStruct((n, D_MODEL), F32),
        compiler_params=pltpu.CompilerParams(dimension_semantics=("parallel",), vmem_limit_bytes=VMEM_LIMIT),
        name="out_proj",
    )(o, sg, w_out, x)


def _dn_proj_kernel(x_ref, g_ref, w_ref, cw_ref, buf_ref, alog_ref, dtb_ref,
                    q_ref, k_ref, v_ref, sz_ref, gb_ref, cbuf_ref, xp_ref, *, n_seq, pad):
    tm = x_ref.shape[0]

    @pl.when(pl.program_id(1) == 0)
    def _():
        xp_ref[0:pad, :] = buf_ref[...]

    xn = _rms(x_ref[...], D_MODEL) * g_ref[...]
    p = _dot(xn.astype(BF16), w_ref[...])
    xp_ref[pad:pad + tm, :] = p[:, 0:CONV_DIM]
    conv = xp_ref[pad:pad + tm, :] * cw_ref[CONV_W - 1:CONV_W, :]
    for i in range(1, CONV_W):
        conv = conv + xp_ref[pad - i * n_seq:pad - i * n_seq + tm, :] * cw_ref[CONV_W - 1 - i:CONV_W - i, :]
    act = conv * jax.nn.sigmoid(conv)
    for h in range(DN_HEADS):
        sl = slice(h * DN_DK, (h + 1) * DN_DK)
        qh = act[:, sl]
        q_ref[:, sl] = qh * lax.rsqrt(jnp.sum(qh * qh, axis=-1, keepdims=True) + EPS) * (DN_DK ** -0.5)
        kh = act[:, DN_HEADS * DN_DK + h * DN_DK:DN_HEADS * DN_DK + (h + 1) * DN_DK]
        k_ref[:, sl] = kh * lax.rsqrt(jnp.sum(kh * kh, axis=-1, keepdims=True) + EPS)
    v_ref[...] = act[:, 2 * DN_HEADS * DN_DK:CONV_DIM]
    z = p[:, N_Z:N_BA]
    sz_ref[...] = (z * jax.nn.sigmoid(z)).astype(BF16)
    ba = p[:, N_BA:N_END]
    sp_in = ba + dtb_ref[...]
    softplus = jnp.maximum(sp_in, 0.0) + jnp.log1p(jnp.exp(-jnp.abs(sp_in)))
    lane = lax.broadcasted_iota(I32, (1, LANES), 1)
    gb_ref[...] = jnp.where(lane < DN_HEADS, jax.nn.sigmoid(ba), -jnp.exp(alog_ref[...]) * softplus)
    cbuf_ref[...] = xp_ref[tm:tm + pad, :]
    xp_ref[0:pad, :] = xp_ref[tm:tm + pad, :]


def _dn_proj(x, norm_g, w_packed, conv_w, buf, alog_pad, dtb_pad, n_groups, n_seq, tm):
    n = x.shape[0]
    pad = buf.shape[1]
    tiles = n // n_groups // tm
    assert n == n_groups * tiles * tm and pad % SUBLANES == 0 and pad >= (CONV_W - 1) * n_seq
    row = lambda s, i: (s * tiles + i, 0)
    const = lambda s, i: (0, 0)
    big = lambda d: pl.BlockSpec((tm, d), row)
    return pl.pallas_call(
        functools.partial(_dn_proj_kernel, n_seq=n_seq, pad=pad),
        grid=(n_groups, tiles),
        in_specs=[
            big(D_MODEL),
            pl.BlockSpec((1, D_MODEL), const),
            pl.BlockSpec((D_MODEL, N_END), const),
            pl.BlockSpec((CONV_W, CONV_DIM), const),
            pl.BlockSpec((None, pad, CONV_DIM), lambda s, i: (s, 0, 0)),
            pl.BlockSpec((1, LANES), const),
            pl.BlockSpec((1, LANES), const),
        ],
        out_specs=(big(D_MODEL), big(D_MODEL), big(D_MODEL), big(D_MODEL), big(LANES),
                   pl.BlockSpec((None, pad, CONV_DIM), lambda s, i: (s, 0, 0))),
        out_shape=(
            jax.ShapeDtypeStruct((n, D_MODEL), F32),
            jax.ShapeDtypeStruct((n, D_MODEL), F32),
            jax.ShapeDtypeStruct((n, D_MODEL), F32),
            jax.ShapeDtypeStruct((n, D_MODEL), BF16),
            jax.ShapeDtypeStruct((n, LANES), F32),
            jax.ShapeDtypeStruct((n_groups, pad, CONV_DIM), F32),
        ),
        scratch_shapes=[pltpu.VMEM((pad + tm, CONV_DIM), F32)],
        compiler_params=pltpu.CompilerParams(dimension_semantics=("parallel", "arbitrary"),
                                             vmem_limit_bytes=VMEM_LIMIT),
        name="dn_proj",
    )(x, norm_g, w_packed, conv_w, buf, alog_pad, dtb_pad)


def _dn_rec_kernel(q_ref, k_ref, v_ref, gb_ref, s0_ref, sz_ref, og_ref, w_ref, x_ref, y_ref, sout_ref,
                   s_scr, a_scr):
    c_len = q_ref.shape[0]

    @pl.when(pl.program_id(1) == 0)
    def _():
        s_scr[...] = s0_ref[...]

    ri = lax.broadcasted_iota(I32, (c_len, c_len), 0)
    ci = lax.broadcasted_iota(I32, (c_len, c_len), 1)
    incl = ri >= ci
    strict = ri > ci
    tril = jnp.where(incl, 1.0, 0.0)
    eye_c = jnp.where(ri == ci, 1.0, 0.0)
    eye_k = jnp.where(lax.broadcasted_iota(I32, (DN_DK, DN_DK), 0) == lax.broadcasted_iota(I32, (DN_DK, DN_DK), 1),
                      1.0, 0.0)
    lane = lax.broadcasted_iota(I32, (c_len, LANES), 1)
    n_sq = max(0, (c_len - 1).bit_length() - 1)

    tile_aligned = c_len % LANES == 0
    gbc = gb_ref[...]
    gcum = _dot(tril, gbc, HI)
    gcum_t = gcum.T if tile_aligned else None
    heads = range(DN_HEADS)
    hs = [slice(h * DN_DK, (h + 1) * DN_DK) for h in heads]
    beta = [gbc[:, h:h + 1] for h in heads]
    gcol = [gcum[:, DN_HEADS + h:DN_HEADS + h + 1] for h in heads]
    if tile_aligned:
        grow = [gcum_t[DN_HEADS + h:DN_HEADS + h + 1, :] for h in heads]
    else:
        grow = [_dot_nt(jnp.where(lane == DN_HEADS + h, 1.0, 0.0), gcum, HI) for h in heads]
    glast = [gcol[h][c_len - 1:c_len, :] for h in heads]
    decay = [jnp.where(incl, jnp.exp(jnp.where(incl, gcol[h] - grow[h], 0.0)), 0.0) for h in heads]
    eg = [jnp.exp(gcol[h]) for h in heads]
    a_mat = [beta[h] * _dot_nt(k_ref[:, hs[h]], k_ref[:, hs[h]]) * jnp.where(strict, decay[h], 0.0) for h in heads]
    pw = [_split_bf16(-a_mat[h]) for h in heads]
    t_inv = [eye_c - a_mat[h] for h in heads]
    for _ in range(n_sq):
        pw = [_split_bf16(_dot3(pw[h], pw[h])) for h in heads]
        t_inv = [t_inv[h] + _dot3(_split_bf16(t_inv[h]), pw[h]) for h in heads]
    wu = [_dot3(_split_bf16(t_inv[h]),
                _split_bf16(jnp.concatenate([beta[h] * eg[h] * k_ref[:, hs[h]], beta[h] * v_ref[:, hs[h]]], axis=1)))
          for h in heads]
    qk = [_dot_nt(q_ref[:, hs[h]], k_ref[:, hs[h]]) * decay[h] for h in heads]
    kd = [k_ref[:, hs[h]] * jnp.exp(glast[h] - gcol[h]) for h in heads]
    kdt = [kd[h].T if tile_aligned else _dot_nt(eye_k, kd[h], HI) for h in heads]
    for h in heads:
        s_old = s_scr[h]
        u = wu[h][:, DN_DK:DN_DK + DN_DV] - _dot(wu[h][:, 0:DN_DK], s_old)
        o = eg[h] * _dot(q_ref[:, hs[h]], s_old) + _dot(qk[h], u)
        a_scr[:, hs[h]] = _rms(o, DN_DV) * og_ref[...] * sz_ref[:, hs[h]].astype(F32)
        s_scr[h] = jnp.exp(glast[h]) * s_old + _dot(kdt[h], u)
    y_ref[...] = x_ref[...] + _dot(a_scr[...].astype(BF16), w_ref[...])

    @pl.when(pl.program_id(1) == pl.num_programs(1) - 1)
    def _():
        sout_ref[...] = s_scr[...]


def _dn_rec(q, k, v, gb, s0, sz, o_gain, w_out, x, n_seq, t_len, chunk):
    assert t_len % chunk == 0
    n_chunks = t_len // chunk
    row = lambda b, c: (b * n_chunks + c, 0)
    const = lambda b, c: (0, 0)
    wide = pl.BlockSpec((chunk, D_MODEL), row)
    st = pl.BlockSpec((None, None, DN_HEADS, DN_DK, DN_DV), lambda b, c: (b, 0, 0, 0, 0))
    return pl.pallas_call(
        _dn_rec_kernel,
        grid=(n_seq, n_chunks),
        in_specs=[wide, wide, wide, pl.BlockSpec((chunk, LANES), row), st, wide,
                  pl.BlockSpec((1, DN_DV), const), pl.BlockSpec((D_MODEL, D_MODEL), const), wide],
        out_specs=(wide, st),
        out_shape=(jax.ShapeDtypeStruct((n_seq * t_len, D_MODEL), F32),
                   jax.ShapeDtypeStruct((n_seq, 1, DN_HEADS, DN_DK, DN_DV), F32)),
        scratch_shapes=[pltpu.VMEM((DN_HEADS, DN_DK, DN_DV), F32), pltpu.VMEM((chunk, D_MODEL), F32)],
        compiler_params=pltpu.CompilerParams(dimension_semantics=("parallel", "arbitrary"),
                                             vmem_limit_bytes=VMEM_LIMIT),
        name="dn_rec",
    )(q, k, v, gb, s0, sz, o_gain, w_out, x)


def _pad_lanes(v, offset=0):
    return jnp.zeros((1, LANES), F32).at[0, offset:offset + v.shape[0]].set(v)


def _pack_att_w(w):
    d = w.shape[0]
    gate0 = 2 * D_MODEL + 2 * KV_W + IDX_HEADS * IDX_DIM + IDX_DIM + IDX_HEADS - D_MODEL
    return jnp.concatenate(
        [w[:, :gate0], jnp.zeros((d, A_GATE - gate0), w.dtype), w[:, gate0:]], axis=1).astype(BF16)


def _pack_dn_w(w):
    d = w.shape[0]
    return jnp.concatenate([w, jnp.zeros((d, N_END - w.shape[1]), w.dtype)], axis=1).astype(BF16)


def kernel(x_prompt, x_sample, cache_k, cache_v, cache_idx_k, state_dn_S, state_dn_conv, page_table,
           att_norm, att_w_in, att_q_gain, att_k_gain, att_ik_gain, att_w_out,
           dn_norm, dn_w_in, dn_conv_w, dn_A_log, dn_dt_bias, dn_o_gain, dn_w_out):
    bp, seq, d = x_prompt.shape
    bs, t_new, _ = x_sample.shape
    n_pool = cache_k.shape[0]
    np_rows, ns_rows = bp * seq, bs * t_new
    tm = 256

    xp = x_prompt.reshape(np_rows, d)
    xs = x_sample.reshape(ns_rows, d)

    w_att = _pack_att_w(att_w_in[0])
    a_norm = att_norm[0][None, :]
    qg, kg = att_q_gain[0][None, :], att_k_gain[0][None, :]
    ikg = _pad_lanes(att_ik_gain[0])
    w_ao = att_w_out[0].astype(BF16)

    q_p, k_p, kb_p, v_p, vt_p, iq_p, ikw_p, ikb_p, iwt_p, sg_p = _att_proj(xp, a_norm, w_att, qg, kg, ikg, tm)
    y1_p = _att_prompt(q_p, kb_p, vt_p, ikb_p, iq_p, iwt_p, sg_p, w_ao, xp, bp, seq)

    q_s, k_s, _, v_s, _, iq_s, ikw_s, ikb_s, _, sg_s = _att_proj(xs, a_norm, w_att, qg, kg, ikg, min(tm, ns_rows))
    pad_new = lambda a: jnp.pad(a.reshape(bs, t_new, a.shape[-1]), ((0, 0), (0, PAGE_SIZE - t_new), (0, 0)))
    o_s = _att_sample(page_table, q_s.astype(F32), iq_s.astype(F32), ikw_s, pad_new(ikb_s), pad_new(k_s), pad_new(v_s),
                      jnp.transpose(cache_idx_k, (0, 2, 3, 1)).reshape(n_pool, IDX_DIM, PAGE_SIZE),
                      cache_k.reshape(n_pool, PAGE_SIZE * N_KV_HEADS, HEAD_DIM),
                      cache_v.reshape(n_pool, PAGE_SIZE * N_KV_HEADS, HEAD_DIM), t_new)
    y1_s = _out_proj(o_s, sg_s, w_ao, xs, min(tm, ns_rows))

    w_dn = _pack_dn_w(dn_w_in[0])
    d_norm = dn_norm[0][None, :]
    alog = _pad_lanes(dn_A_log[0], DN_HEADS)
    dtb = _pad_lanes(dn_dt_bias[0], DN_HEADS)
    o_gain = dn_o_gain[0][None, :]
    w_do = dn_w_out[0].astype(BF16)
    hist = CONV_W - 1

    buf_p = jnp.zeros((bp, SUBLANES, CONV_DIM), F32)
    dq_p, dk_p, dv_p, sz_p, gb_p, cb_p = _dn_proj(y1_p, d_norm, w_dn, dn_conv_w[0], buf_p, alog, dtb,
                                                   n_groups=bp, n_seq=1, tm=tm)
    s0_p = jnp.zeros((bp, 1, DN_HEADS, DN_DK, DN_DV), F32)
    y2_p, s_p = _dn_rec(dq_p, dk_p, dv_p, gb_p, s0_p, sz_p, o_gain, w_do, y1_p, bp, seq, chunk=LANES)

    to_tm = lambda a: a.reshape(bs, t_new, -1).transpose(1, 0, 2).reshape(ns_rows, -1)
    to_bm = lambda a: a.reshape(t_new, bs, -1).transpose(1, 0, 2).reshape(ns_rows, -1)
    buf_s = state_dn_conv[:, 0].transpose(1, 0, 2).reshape(1, hist * bs, CONV_DIM)
    dq_s, dk_s, dv_s, sz_s, gb_s, cb_s = _dn_proj(to_tm(y1_s), d_norm, w_dn, dn_conv_w[0], buf_s, alog, dtb,
                                                   n_groups=1, n_seq=bs, tm=ns_rows)
    y2_s, s_s = _dn_rec(to_bm(dq_s), to_bm(dk_s), to_bm(dv_s), to_bm(gb_s), state_dn_S,
                        to_bm(sz_s.astype(F32)), o_gain, w_do, y1_s, bs, t_new, chunk=t_new)

    return (
        y2_p.reshape(bp, seq, d),
        y2_s.reshape(bs, t_new, d),
        k_p.reshape(bp, seq, 1, N_KV_HEADS, HEAD_DIM),
        v_p.reshape(bp, seq, 1, N_KV_HEADS, HEAD_DIM),
        ikw_p[:, :IDX_DIM].reshape(bp, seq, 1, IDX_DIM),
        k_s.reshape(bs, t_new, 1, N_KV_HEADS, HEAD_DIM),
        v_s.reshape(bs, t_new, 1, N_KV_HEADS, HEAD_DIM),
        ikw_s[:, :IDX_DIM].reshape(bs, t_new, 1, IDX_DIM),
        s_p,
        cb_p[:, SUBLANES - hist:, :].reshape(bp, 1, hist, CONV_DIM),
        s_s,
        cb_s.reshape(hist, bs, CONV_DIM).transpose(1, 0, 2).reshape(bs, 1, hist, CONV_DIM),
    )
```

```python
import functools

import jax
import jax.numpy as jnp
from jax import lax
from jax.experimental import pallas as pl
from jax.experimental.pallas import tpu as pltpu

F32 = jnp.float32
BF16 = jnp.bfloat16
I32 = jnp.int32

EPS = 1e-6
LANES = 128
SUBLANES = 8
VMEM_LIMIT = 56 * 1024 * 1024

D_MODEL = 1024
N_HEADS = 8
HEAD_DIM = 128
N_KV_HEADS = 2
GROUP = N_HEADS // N_KV_HEADS
IDX_HEADS = 8
IDX_DIM = 64
TOPK_MAX = 256
PAGE_SIZE = 128
DN_HEADS = 8
DN_DK = 128
DN_DV = 128
CONV_W = 4
KV_W = N_KV_HEADS * HEAD_DIM
CONV_DIM = 3 * DN_HEADS * DN_DK

A_Q, A_K, A_V, A_IQ, A_IKW, A_GATE, A_END = 0, 1024, 1280, 1536, 2048, 2176, 3200
N_Z, N_BA, N_END = 3072, 4096, 4224

INT_MIN = -2147483648
INT_MAX = 2147483647
NEG_BIG = -1e30
NT_DIMS = (((1,), (1,)), ((), ()))
HI = lax.Precision.HIGHEST
NEUMANN_SPLIT_LEVELS = 4


def _dot(a, b, precision=None):
    return jnp.dot(a, b, preferred_element_type=F32, precision=precision)


def _dot_nt(a, b, precision=None):
    return lax.dot_general(a, b, NT_DIMS, preferred_element_type=F32, precision=precision)


def _split_bf16(x):
    hi = x.astype(BF16)
    return hi, (x - hi.astype(F32)).astype(BF16)


def _dot3(a, b):
    (ah, al), (bh, bl) = a, b
    return _dot(ah, bh) + (_dot(ah, bl) + _dot(al, bh))


def _rms(x, n):
    return x * lax.rsqrt(jnp.sum(x * x, axis=-1, keepdims=True) * (1.0 / n) + EPS)


def _ordinal_to_float(u):
    key = u ^ jnp.int32(INT_MIN)
    return pltpu.bitcast(jnp.where(key < 0, key ^ jnp.int32(INT_MAX), key), F32)


def _att_proj_kernel(x_ref, g_ref, w_ref, qg_ref, kg_ref, ikg_ref,
                     q_ref, k_ref, kb_ref, v_ref, vt_ref, iq_ref, ikw_ref, ikb_ref, iwt_ref, sg_ref):
    x = x_ref[...]
    xn = _rms(x, D_MODEL) * g_ref[...]
    p = _dot(xn.astype(BF16), w_ref[...])
    for h in range(N_HEADS):
        qh = _rms(p[:, A_Q + h * HEAD_DIM:A_Q + (h + 1) * HEAD_DIM], HEAD_DIM) * qg_ref[...]
        q_ref[:, h * HEAD_DIM:(h + 1) * HEAD_DIM] = (qh * (HEAD_DIM ** -0.5)).astype(BF16)
    for g in range(N_KV_HEADS):
        kh = _rms(p[:, A_K + g * HEAD_DIM:A_K + (g + 1) * HEAD_DIM], HEAD_DIM) * kg_ref[...]
        k_ref[:, g * HEAD_DIM:(g + 1) * HEAD_DIM] = kh
        kb_ref[:, g * HEAD_DIM:(g + 1) * HEAD_DIM] = kh.astype(BF16)
    v = p[:, A_V:A_V + KV_W]
    v_ref[...] = v
    vt_ref[...] = v.T.astype(BF16)
    lane = lax.broadcasted_iota(I32, (1, LANES), 1)
    is_ik = lane < IDX_DIM
    for h in range(IDX_HEADS):
        iqh = p[:, A_IQ + (h // 2) * LANES:A_IQ + (h // 2 + 1) * LANES]
        if h % 2:
            iqh = pltpu.roll(iqh, IDX_DIM, axis=1)
        iq_ref[h] = jnp.where(is_ik, iqh, 0.0).astype(BF16)
    ikw = p[:, A_IKW:A_IKW + LANES]
    ik = jnp.where(is_ik, ikw, 0.0)
    ikn = _rms(ik, IDX_DIM) * ikg_ref[...]
    out = jnp.where(is_ik, ikn, ikw * (IDX_HEADS ** -0.5 * IDX_DIM ** -0.5))
    ikw_ref[...] = out
    ikb_ref[...] = ikn.astype(BF16)
    iwt_ref[...] = out.T[IDX_DIM:IDX_DIM + IDX_HEADS, :]
    gate = p[:, A_GATE:A_END]
    sg_ref[...] = (gate * jax.nn.sigmoid(gate)).astype(BF16)


def _att_proj(x, norm_g, w_packed, q_gain, k_gain, ik_gain_pad, tm):
    n = x.shape[0]
    assert n % tm == 0 and tm % LANES == 0
    row = lambda i: (i, 0)
    const = lambda i: (0, 0)
    out_shape = (
        jax.ShapeDtypeStruct((n, D_MODEL), BF16),
        jax.ShapeDtypeStruct((n, KV_W), F32),
        jax.ShapeDtypeStruct((n, KV_W), BF16),
        jax.ShapeDtypeStruct((n, KV_W), F32),
        jax.ShapeDtypeStruct((KV_W, n), BF16),
        jax.ShapeDtypeStruct((IDX_HEADS, n, LANES), BF16),
        jax.ShapeDtypeStruct((n, LANES), F32),
        jax.ShapeDtypeStruct((n, LANES), BF16),
        jax.ShapeDtypeStruct((IDX_HEADS, n), F32),
        jax.ShapeDtypeStruct((n, D_MODEL), BF16),
    )
    out_specs = (
        pl.BlockSpec((tm, D_MODEL), row),
        pl.BlockSpec((tm, KV_W), row),
        pl.BlockSpec((tm, KV_W), row),
        pl.BlockSpec((tm, KV_W), row),
        pl.BlockSpec((KV_W, tm), lambda i: (0, i)),
        pl.BlockSpec((IDX_HEADS, tm, LANES), lambda i: (0, i, 0)),
        pl.BlockSpec((tm, LANES), row),
        pl.BlockSpec((tm, LANES), row),
        pl.BlockSpec((IDX_HEADS, tm), lambda i: (0, i)),
        pl.BlockSpec((tm, D_MODEL), row),
    )
    return pl.pallas_call(
        _att_proj_kernel,
        grid=(n // tm,),
        in_specs=[
            pl.BlockSpec((tm, D_MODEL), row),
            pl.BlockSpec((1, D_MODEL), const),
            pl.BlockSpec((D_MODEL, A_END), const),
            pl.BlockSpec((1, HEAD_DIM), const),
            pl.BlockSpec((1, HEAD_DIM), const),
            pl.BlockSpec((1, LANES), const),
        ],
        out_specs=out_specs,
        out_shape=out_shape,
        compiler_params=pltpu.CompilerParams(dimension_semantics=("parallel",), vmem_limit_bytes=VMEM_LIMIT),
        name="att_proj",
    )(x, norm_g, w_packed, q_gain, k_gain, ik_gain_pad)


def _reduce(x, axis, combine, finish):
    n = x.shape[axis]
    unit = 8 * (SUBLANES if axis == 0 else LANES)
    full = n // unit
    if full < 2:
        return finish(x, axis=axis, keepdims=True)
    cut = (lambda i, j: x[i:j, :]) if axis == 0 else (lambda i, j: x[:, i:j])
    acc = cut(0, unit)
    for i in range(1, full):
        acc = combine(acc, cut(i * unit, (i + 1) * unit))
    out = finish(acc, axis=axis, keepdims=True)
    if n % unit:
        out = combine(out, finish(cut(full * unit, n), axis=axis, keepdims=True))
    return out


def _count(mask, axis):
    return _reduce(jnp.where(mask, 1.0, 0.0), axis, jnp.add, jnp.sum)


def _select_bias(sc_ref, lo_ref, n_keys, key_idx, topk, axis):
    sl = (slice(0, n_keys), slice(None)) if axis == 0 else (slice(None), slice(0, n_keys))
    vec = (1, sc_ref.shape[1]) if axis == 0 else (sc_ref.shape[0], 1)
    kf = float(topk)

    def bit_step(i, prefix):
        cand = prefix | (jnp.int32(1) << (31 - i))
        cnt = _count(sc_ref[sl] >= _ordinal_to_float(cand), axis)
        return jnp.where(cnt >= kf, cand, prefix)

    prefix = lax.fori_loop(0, 32, bit_step, jnp.zeros(vec, I32))
    thr = _ordinal_to_float(prefix)
    sc = sc_ref[sl]
    few = jnp.logical_not(_count(sc >= thr, axis) >= kf)
    thr = jnp.where(few, -jnp.inf, thr)
    need = kf - _count(sc > thr, axis)
    excess = (_count(sc >= thr, axis) > kf) & jnp.logical_not(few)
    lo_ref[...] = jnp.full(vec, INT_MAX, I32)

    @pl.when(jnp.max(jnp.where(excess, 1.0, 0.0)) > 0.0)
    def _():
        nbits = max(1, (n_keys - 1).bit_length())

        def idx_step(i, lo):
            cand = lo + (jnp.int32(1) << (nbits - 1 - i))
            cnt = _count((sc_ref[sl] == thr) & (key_idx < cand), axis)
            return jnp.where(cnt < need, cand, lo)

        lo_ref[...] = lax.fori_loop(0, nbits, idx_step, jnp.zeros(vec, I32))

    lo = jnp.where(few, -1, lo_ref[...])
    sel = (sc > thr) | ((sc == thr) & (key_idx <= lo))
    return jnp.where(sel, 0.0, NEG_BIG)


def _att_prompt_body(n_keys, topk, q_ref, kb_ref, vt_ref, ikb_ref, iq_ref, iwt_ref, o_ref,
                     sc_ref, bias_ref, lo_ref):
    qb = q_ref.shape[0]
    t0 = pl.program_id(1) * qb
    ik = ikb_ref[0:n_keys, :]
    iwt = iwt_ref[...]
    score = jnp.zeros((n_keys, qb), F32)
    for h in range(IDX_HEADS):
        s = _dot_nt(ik, iq_ref[h])
        score = score + jnp.maximum(s, 0.0) * iwt[h:h + 1, :]
    key_idx = lax.broadcasted_iota(I32, (n_keys, qb), 0)
    q_pos = t0 + lax.broadcasted_iota(I32, (n_keys, qb), 1)
    sc_ref[0:n_keys, :] = jnp.where(key_idx <= q_pos, score, -jnp.inf)
    bias_ref[0:n_keys, :] = _select_bias(sc_ref, lo_ref, n_keys, key_idx, topk, axis=0)
    for h in range(N_HEADS):
        g = h // GROUP
        s = _dot_nt(kb_ref[0:n_keys, g * HEAD_DIM:(g + 1) * HEAD_DIM],
                    q_ref[:, h * HEAD_DIM:(h + 1) * HEAD_DIM])
        s = s + bias_ref[0:n_keys, :]
        m = _reduce(s, 0, jnp.maximum, jnp.max)
        p = jnp.exp(s - m)
        l = _reduce(p, 0, jnp.add, jnp.sum)
        ot = _dot(vt_ref[g * HEAD_DIM:(g + 1) * HEAD_DIM, 0:n_keys], p.astype(BF16))
        o_ref[:, h * HEAD_DIM:(h + 1) * HEAD_DIM] = (ot / l).T


def _att_prompt_kernel(q_ref, kb_ref, vt_ref, ikb_ref, iq_ref, iwt_ref, sg_ref, w_ref, x_ref, y_ref,
                       o_scr, sc_ref, bias_ref, lo_ref, *, cls_len, n_cls, topk):
    qb = q_ref.shape[0]
    cls = (pl.program_id(1) * qb) // cls_len
    for c in range(n_cls):
        pl.when(cls == c)(functools.partial(
            _att_prompt_body, cls_len * (c + 1), topk, q_ref, kb_ref, vt_ref, ikb_ref, iq_ref, iwt_ref,
            o_scr, sc_ref, bias_ref, lo_ref))
    a = (o_scr[...] * sg_ref[...].astype(F32)).astype(BF16)
    y_ref[...] = x_ref[...] + _dot(a, w_ref[...])


def _att_prompt(q, kb, vt, ikb, iq, iwt, sg, w_out, x, batch, seq, qb=LANES):
    nb = seq // qb
    n_cls = min(8, nb)
    assert seq % (n_cls * qb) == 0
    topk = min(TOPK_MAX, seq // 4)
    kern = functools.partial(_att_prompt_kernel, cls_len=seq // n_cls, n_cls=n_cls, topk=topk)
    return pl.pallas_call(
        kern,
        grid=(batch, nb),
        in_specs=[
            pl.BlockSpec((qb, D_MODEL), lambda b, j: (b * nb + j, 0)),
            pl.BlockSpec((seq, KV_W), lambda b, j: (b, 0)),
            pl.BlockSpec((KV_W, seq), lambda b, j: (0, b)),
            pl.BlockSpec((seq, LANES), lambda b, j: (b, 0)),
            pl.BlockSpec((IDX_HEADS, qb, LANES), lambda b, j: (0, b * nb + j, 0)),
            pl.BlockSpec((IDX_HEADS, qb), lambda b, j: (0, b * nb + j)),
            pl.BlockSpec((qb, D_MODEL), lambda b, j: (b * nb + j, 0)),
            pl.BlockSpec((D_MODEL, D_MODEL), lambda b, j: (0, 0)),
            pl.BlockSpec((qb, D_MODEL), lambda b, j: (b * nb + j, 0)),
        ],
        out_specs=pl.BlockSpec((qb, D_MODEL), lambda b, j: (b * nb + j, 0)),
        out_shape=jax.ShapeDtypeStruct((batch * seq, D_MODEL), F32),
        scratch_shapes=[
            pltpu.VMEM((qb, D_MODEL), F32),
            pltpu.VMEM((seq, qb), F32),
            pltpu.VMEM((seq, qb), F32),
            pltpu.VMEM((1, qb), I32),
        ],
        compiler_params=pltpu.CompilerParams(dimension_semantics=("parallel", "arbitrary"),
                                             vmem_limit_bytes=VMEM_LIMIT),
        name="att_prompt",
    )(q, kb, vt, ikb, iq, iwt, sg, w_out, x)


def _att_sample_kernel(pt_ref, q_ref, iq_ref, ikw_ref, ikn_ref, kn_ref, vn_ref,
                       cik_hbm, ck_hbm, cv_hbm, o_ref,
                       ikbuf, kbuf, vbuf, sc_ref, bias_ref, lo_ref, sem_i, sem_kv,
                       *, n_pages, chunk_pages, topk):
    b = pl.program_id(0)
    t_new = q_ref.shape[0]
    past = n_pages * PAGE_SIZE
    n_keys = past + PAGE_SIZE
    n_chunks = n_pages // chunk_pages
    ck = chunk_pages * PAGE_SIZE

    n_seq = pl.num_programs(0)
    islot = b % 2
    ahead = n_chunks - 1

    def idx_copy(seq, slot, p):
        return pltpu.make_async_copy(cik_hbm.at[pt_ref[seq, p]],
                                     ikbuf.at[slot, :, pl.ds(p * PAGE_SIZE, PAGE_SIZE)], sem_i.at[slot])

    def kv_copies(seq, c, p):
        page = pt_ref[seq, c * chunk_pages + p]
        dst = pl.ds(p * PAGE_SIZE * N_KV_HEADS, PAGE_SIZE * N_KV_HEADS)
        return (pltpu.make_async_copy(ck_hbm.at[page], kbuf.at[c, dst], sem_kv.at[0, c]),
                pltpu.make_async_copy(cv_hbm.at[page], vbuf.at[c, dst], sem_kv.at[1, c]))

    def start_idx(seq, slot):
        def go(p, carry):
            idx_copy(seq, slot, p).start()
            return carry
        lax.fori_loop(0, n_pages, go, 0, unroll=8)

    def wait_idx(seq, slot):
        def go(p, carry):
            idx_copy(seq, slot, p).wait()
            return carry
        lax.fori_loop(0, n_pages, go, 0, unroll=8)

    def start_chunk(seq, c):
        def go(p, carry):
            for cp in kv_copies(seq, c, p):
                cp.start()
            return carry
        lax.fori_loop(0, chunk_pages, go, 0, unroll=8)

    def wait_chunk(seq, c):
        def go(p, carry):
            for cp in kv_copies(seq, c, p):
                cp.wait()
            return carry
        lax.fori_loop(0, chunk_pages, go, 0, unroll=8)

    @pl.when(b == 0)
    def _():
        start_idx(b, islot)
        for c in range(ahead):
            start_chunk(b, c)

    @pl.when(b + 1 < n_seq)
    def _():
        start_idx(b + 1, 1 - islot)

    wait_idx(b, islot)

    iq2 = iq_ref[...].reshape(IDX_HEADS * t_new, LANES).astype(BF16)
    ikw = ikw_ref[...]

    def scores(ik_rows):
        s = _dot_nt(iq2, ik_rows)
        acc = jnp.zeros((t_new, ik_rows.shape[0]), F32)
        for h in range(IDX_HEADS):
            acc = acc + jnp.maximum(s[h * t_new:(h + 1) * t_new, :], 0.0) * ikw[:, IDX_DIM + h:IDX_DIM + h + 1]
        return acc

    iq2_past = iq2[:, 0:IDX_DIM]
    for c in range(n_chunks):
        s = _dot(iq2_past, ikbuf[islot, :, c * ck:(c + 1) * ck].astype(BF16))
        acc = jnp.zeros((t_new, ck), F32)
        for h in range(IDX_HEADS):
            acc = acc + jnp.maximum(s[h * t_new:(h + 1) * t_new, :], 0.0) * ikw[:, IDX_DIM + h:IDX_DIM + h + 1]
        sc_ref[:, c * ck:(c + 1) * ck] = acc
    new_idx = lax.broadcasted_iota(I32, (t_new, PAGE_SIZE), 1)
    tok = lax.broadcasted_iota(I32, (t_new, PAGE_SIZE), 0)
    sc_ref[:, past:n_keys] = jnp.where(new_idx <= tok, scores(ikn_ref[...]), -jnp.inf)
    key_idx = lax.broadcasted_iota(I32, (t_new, n_keys), 1)
    bias_ref[...] = _select_bias(sc_ref, lo_ref, n_keys, key_idx, topk, axis=1)

    rows = GROUP * t_new
    qs = [jnp.concatenate([q_ref[:, (g * GROUP + r) * HEAD_DIM:(g * GROUP + r + 1) * HEAD_DIM]
                           for r in range(GROUP)], axis=0).astype(BF16) for g in range(N_KV_HEADS)]
    m = [jnp.full((rows, 1), NEG_BIG, F32) for _ in range(N_KV_HEADS)]
    l = [jnp.zeros((rows, 1), F32) for _ in range(N_KV_HEADS)]
    acc = [jnp.zeros((rows, HEAD_DIM), F32) for _ in range(N_KV_HEADS)]

    def attend(g, k_rows, v_rows, bias):
        s = _dot_nt(qs[g], k_rows) + jnp.concatenate([bias] * GROUP, axis=0)
        m_new = jnp.maximum(m[g], _reduce(s, 1, jnp.maximum, jnp.max))
        alpha = jnp.exp(m[g] - m_new)
        p = jnp.exp(s - m_new)
        l[g] = alpha * l[g] + _reduce(p, 1, jnp.add, jnp.sum)
        acc[g] = alpha * acc[g] + _dot(p.astype(BF16), v_rows)
        m[g] = m_new

    for g in range(N_KV_HEADS):
        hd = slice(g * HEAD_DIM, (g + 1) * HEAD_DIM)
        attend(g, kn_ref[:, hd].astype(BF16), vn_ref[:, hd].astype(BF16), bias_ref[:, past:n_keys])
    for c in range(n_chunks):
        nxt = c + ahead
        if nxt < n_chunks:
            start_chunk(b, nxt)
        else:
            pl.when(b + 1 < n_seq)(functools.partial(start_chunk, b + 1, nxt - n_chunks))
        wait_chunk(b, c)
        for g in range(N_KV_HEADS):
            head_rows = pl.ds(g, ck, stride=N_KV_HEADS)
            attend(g, kbuf[c, head_rows, :].astype(BF16), vbuf[c, head_rows, :].astype(BF16),
                   bias_ref[:, c * ck:(c + 1) * ck])
    for g in range(N_KV_HEADS):
        og = acc[g] / l[g]
        for r in range(GROUP):
            h = g * GROUP + r
            o_ref[:, h * HEAD_DIM:(h + 1) * HEAD_DIM] = og[r * t_new:(r + 1) * t_new, :]


def _att_sample(page_table, q, iq, ikw, ikn_pad, kn_pad, vn_pad, cache_ik, cache_k, cache_v, t_new):
    nseq, n_pages = page_table.shape
    assert t_new == SUBLANES
    chunk_pages = min(32, n_pages // 2)
    assert n_pages % chunk_pages == 0
    past = n_pages * PAGE_SIZE
    n_keys = past + PAGE_SIZE
    topk = min(TOPK_MAX, (past + t_new) // 4)
    ck = chunk_pages * PAGE_SIZE
    kern = functools.partial(_att_sample_kernel, n_pages=n_pages, chunk_pages=chunk_pages, topk=topk)
    grid_spec = pltpu.PrefetchScalarGridSpec(
        num_scalar_prefetch=1,
        grid=(nseq,),
        in_specs=[
            pl.BlockSpec((t_new, D_MODEL), lambda b, pt: (b, 0)),
            pl.BlockSpec((IDX_HEADS, t_new, LANES), lambda b, pt: (0, b, 0)),
            pl.BlockSpec((t_new, LANES), lambda b, pt: (b, 0)),
            pl.BlockSpec((None, PAGE_SIZE, LANES), lambda b, pt: (b, 0, 0)),
            pl.BlockSpec((None, PAGE_SIZE, KV_W), lambda b, pt: (b, 0, 0)),
            pl.BlockSpec((None, PAGE_SIZE, KV_W), lambda b, pt: (b, 0, 0)),
            pl.BlockSpec(memory_space=pl.ANY),
            pl.BlockSpec(memory_space=pl.ANY),
            pl.BlockSpec(memory_space=pl.ANY),
        ],
        out_specs=pl.BlockSpec((t_new, D_MODEL), lambda b, pt: (b, 0)),
        scratch_shapes=[
            pltpu.VMEM((2, IDX_DIM, past), F32),
            pltpu.VMEM((n_pages // chunk_pages, ck * N_KV_HEADS, HEAD_DIM), F32),
            pltpu.VMEM((n_pages // chunk_pages, ck * N_KV_HEADS, HEAD_DIM), F32),
            pltpu.VMEM((t_new, n_keys), F32),
            pltpu.VMEM((t_new, n_keys), F32),
            pltpu.VMEM((t_new, 1), I32),
            pltpu.SemaphoreType.DMA((2,)),
            pltpu.SemaphoreType.DMA((2, n_pages // chunk_pages)),
        ],
    )
    return pl.pallas_call(
        kern,
        grid_spec=grid_spec,
        out_shape=jax.ShapeDtypeStruct((nseq * t_new, D_MODEL), F32),
        compiler_params=pltpu.CompilerParams(dimension_semantics=("arbitrary",), vmem_limit_bytes=VMEM_LIMIT),
        name="att_sample",
    )(page_table, q, iq, ikw, ikn_pad, kn_pad, vn_pad, cache_ik, cache_k, cache_v)


def _out_proj_kernel(o_ref, sg_ref, w_ref, x_ref, y_ref):
    a = (o_ref[...] * sg_ref[...].astype(F32)).astype(BF16)
    y_ref[...] = x_ref[...] + _dot(a, w_ref[...])


def _out_proj(o, sg, w_out, x, tm):
    n = o.shape[0]
    row = pl.BlockSpec((tm, D_MODEL), lambda i: (i, 0))
    return pl.pallas_call(
        _out_proj_kernel,
        grid=(n // tm,),
        in_specs=[row, row, pl.BlockSpec((D_MODEL, D_MODEL), lambda i: (0, 0)), row],
        out_specs=row,
        out_shape=jax.ShapeDtypeStruct((n, D_MODEL), F32),
        compiler_params=pltpu.CompilerParams(dimension_semantics=("parallel",), vmem_limit_bytes=VMEM_LIMIT),
        name="out_proj",
    )(o, sg, w_out, x)


def _dn_proj_kernel(x_ref, g_ref, w_ref, cw_ref, buf_ref, alog_ref, dtb_ref,
                    q_ref, k_ref, v_ref, sz_ref, gb_ref, cbuf_ref, xp_ref, *, n_seq, pad):
    tm = x_ref.shape[0]

    @pl.when(pl.program_id(1) == 0)
    def _():
        xp_ref[0:pad, :] = buf_ref[...]

    xn = _rms(x_ref[...], D_MODEL) * g_ref[...]
    p = _dot(xn.astype(BF16), w_ref[...])
    xp_ref[pad:pad + tm, :] = p[:, 0:CONV_DIM]
    conv = xp_ref[pad:pad + tm, :] * cw_ref[CONV_W - 1:CONV_W, :]
    for i in range(1, CONV_W):
        conv = conv + xp_ref[pad - i * n_seq:pad - i * n_seq + tm, :] * cw_ref[CONV_W - 1 - i:CONV_W - i, :]
    act = conv * jax.nn.sigmoid(conv)
    for h in range(DN_HEADS):
        sl = slice(h * DN_DK, (h + 1) * DN_DK)
        qh = act[:, sl]
        q_ref[:, sl] = qh * lax.rsqrt(jnp.sum(qh * qh, axis=-1, keepdims=True) + EPS) * (DN_DK ** -0.5)
        kh = act[:, DN_HEADS * DN_DK + h * DN_DK:DN_HEADS * DN_DK + (h + 1) * DN_DK]
        k_ref[:, sl] = kh * lax.rsqrt(jnp.sum(kh * kh, axis=-1, keepdims=True) + EPS)
    v_ref[...] = act[:, 2 * DN_HEADS * DN_DK:CONV_DIM]
    z = p[:, N_Z:N_BA]
    sz_ref[...] = (z * jax.nn.sigmoid(z)).astype(BF16)
    ba = p[:, N_BA:N_END]
    sp_in = ba + dtb_ref[...]
    softplus = jnp.maximum(sp_in, 0.0) + jnp.log1p(jnp.exp(-jnp.abs(sp_in)))
    lane = lax.broadcasted_iota(I32, (1, LANES), 1)
    gb_ref[...] = jnp.where(lane < DN_HEADS, jax.nn.sigmoid(ba), -jnp.exp(alog_ref[...]) * softplus)
    cbuf_ref[...] = xp_ref[tm:tm + pad, :]
    xp_ref[0:pad, :] = xp_ref[tm:tm + pad, :]


def _dn_proj(x, norm_g, w_packed, conv_w, buf, alog_pad, dtb_pad, n_groups, n_seq, tm):
    n = x.shape[0]
    pad = buf.shape[1]
    tiles = n // n_groups // tm
    assert n == n_groups * tiles * tm and pad % SUBLANES == 0 and pad >= (CONV_W - 1) * n_seq
    row = lambda s, i: (s * tiles + i, 0)
    const = lambda s, i: (0, 0)
    big = lambda d: pl.BlockSpec((tm, d), row)
    return pl.pallas_call(
        functools.partial(_dn_proj_kernel, n_seq=n_seq, pad=pad),
        grid=(n_groups, tiles),
        in_specs=[
            big(D_MODEL),
            pl.BlockSpec((1, D_MODEL), const),
            pl.BlockSpec((D_MODEL, N_END), const),
            pl.BlockSpec((CONV_W, CONV_DIM), const),
            pl.BlockSpec((None, pad, CONV_DIM), lambda s, i: (s, 0, 0)),
            pl.BlockSpec((1, LANES), const),
            pl.BlockSpec((1, LANES), const),
        ],
        out_specs=(big(D_MODEL), big(D_MODEL), big(D_MODEL), big(D_MODEL), big(LANES),
                   pl.BlockSpec((None, pad, CONV_DIM), lambda s, i: (s, 0, 0))),
        out_shape=(
            jax.ShapeDtypeStruct((n, D_MODEL), F32),
            jax.ShapeDtypeStruct((n, D_MODEL), F32),
            jax.ShapeDtypeStruct((n, D_MODEL), F32),
            jax.ShapeDtypeStruct((n, D_MODEL), BF16),
            jax.ShapeDtypeStruct((n, LANES), F32),
            jax.ShapeDtypeStruct((n_groups, pad, CONV_DIM), F32),
        ),
        scratch_shapes=[pltpu.VMEM((pad + tm, CONV_DIM), F32)],
        compiler_params=pltpu.CompilerParams(dimension_semantics=("parallel", "arbitrary"),
                                             vmem_limit_bytes=VMEM_LIMIT),
        name="dn_proj",
    )(x, norm_g, w_packed, conv_w, buf, alog_pad, dtb_pad)


def _dn_rec_kernel(q_ref, k_ref, v_ref, gb_ref, s0_ref, sz_ref, og_ref, w_ref, x_ref, y_ref, sout_ref,
                   s_scr, a_scr, *, c_len):
    n_sub = q_ref.shape[0] // c_len

    @pl.when(pl.program_id(1) == 0)
    def _():
        s_scr[...] = s0_ref[...]

    ri = lax.broadcasted_iota(I32, (c_len, c_len), 0)
    ci = lax.broadcasted_iota(I32, (c_len, c_len), 1)
    incl = ri >= ci
    strict = ri > ci
    tril = jnp.where(incl, 1.0, 0.0)
    eye_c = jnp.where(ri == ci, 1.0, 0.0)
    eye_k = jnp.where(lax.broadcasted_iota(I32, (DN_DK, DN_DK), 0) == lax.broadcasted_iota(I32, (DN_DK, DN_DK), 1),
                      1.0, 0.0)
    lane = lax.broadcasted_iota(I32, (c_len, LANES), 1)
    n_sq = max(0, (c_len - 1).bit_length() - 1)

    tile_aligned = c_len % LANES == 0

    subs = range(n_sub)
    rows = [slice(s * c_len, (s + 1) * c_len) for s in subs]
    gbc = [gb_ref[rows[s], :] for s in subs]
    gcum = [_dot(tril, gbc[s], HI) for s in subs]
    gcum_t = [gcum[s].T if tile_aligned else None for s in subs]
    units = [(s, h) for s in subs for h in range(DN_HEADS)]
    n_u = range(len(units))
    at = [(rows[s], slice(h * DN_DK, (h + 1) * DN_DK)) for s, h in units]
    beta = [gbc[s][:, h:h + 1] for s, h in units]
    gcol = [gcum[s][:, DN_HEADS + h:DN_HEADS + h + 1] for s, h in units]
    if tile_aligned:
        grow = [gcum_t[s][DN_HEADS + h:DN_HEADS + h + 1, :] for s, h in units]
    else:
        grow = [_dot_nt(jnp.where(lane == DN_HEADS + h, 1.0, 0.0), gcum[s], HI) for s, h in units]
    glast = [gcol[i][c_len - 1:c_len, :] for i in n_u]
    decay = [jnp.where(incl, jnp.exp(jnp.where(incl, gcol[i] - grow[i], 0.0)), 0.0) for i in n_u]
    eg = [jnp.exp(gcol[i]) for i in n_u]
    a_mat = [beta[i] * _dot_nt(k_ref[at[i]], k_ref[at[i]]) * jnp.where(strict, decay[i], 0.0) for i in n_u]
    pw = [-a_mat[i] for i in n_u]
    t_inv = [eye_c - a_mat[i] for i in n_u]
    for level in range(n_sq):
        if level < NEUMANN_SPLIT_LEVELS:
            ps = [_split_bf16(pw[i]) for i in n_u]
            pw = [_dot3(ps[i], ps[i]) for i in n_u]
            ps = [_split_bf16(pw[i]) for i in n_u]
            t_inv = [t_inv[i] + _dot3(_split_bf16(t_inv[i]), ps[i]) for i in n_u]
        else:
            pw = [_dot(pw[i].astype(BF16), pw[i].astype(BF16)) for i in n_u]
            t_inv = [t_inv[i] + _dot(t_inv[i].astype(BF16), pw[i].astype(BF16)) for i in n_u]
    wu = [_dot(t_inv[i].astype(BF16),
               jnp.concatenate([beta[i] * eg[i] * k_ref[at[i]], beta[i] * v_ref[at[i]]], axis=1).astype(BF16))
          for i in n_u]
    qk = [_dot_nt(q_ref[at[i]], k_ref[at[i]]) * decay[i] for i in n_u]
    kd = [k_ref[at[i]] * jnp.exp(glast[i] - gcol[i]) for i in n_u]
    kdt = [kd[i].T if tile_aligned else _dot_nt(eye_k, kd[i], HI) for i in n_u]
    for i, (s, h) in enumerate(units):
        s_old = s_scr[h]
        u = wu[i][:, DN_DK:DN_DK + DN_DV] - _dot(wu[i][:, 0:DN_DK], s_old)
        o = eg[i] * _dot(q_ref[at[i]], s_old) + _dot(qk[i], u)
        a_scr[at[i]] = _rms(o, DN_DV) * og_ref[...] * sz_ref[at[i]].astype(F32)
        s_scr[h] = jnp.exp(glast[i]) * s_old + _dot(kdt[i], u)
    y_ref[...] = x_ref[...] + _dot(a_scr[...].astype(BF16), w_ref[...])

    @pl.when(pl.program_id(1) == pl.num_programs(1) - 1)
    def _():
        sout_ref[...] = s_scr[...]


def _dn_rec(q, k, v, gb, s0, sz, o_gain, w_out, x, n_seq, t_len, chunk, chunks_per_step):
    step = chunk * chunks_per_step
    assert t_len % step == 0
    n_steps = t_len // step
    row = lambda b, c: (b * n_steps + c, 0)
    const = lambda b, c: (0, 0)
    wide = pl.BlockSpec((step, D_MODEL), row)
    st = pl.BlockSpec((None, None, DN_HEADS, DN_DK, DN_DV), lambda b, c: (b, 0, 0, 0, 0))
    return pl.pallas_call(
        functools.partial(_dn_rec_kernel, c_len=chunk),
        grid=(n_seq, n_steps),
        in_specs=[wide, wide, wide, pl.BlockSpec((step, LANES), row), st, wide,
                  pl.BlockSpec((1, DN_DV), const), pl.BlockSpec((D_MODEL, D_MODEL), const), wide],
        out_specs=(wide, st),
        out_shape=(jax.ShapeDtypeStruct((n_seq * t_len, D_MODEL), F32),
                   jax.ShapeDtypeStruct((n_seq, 1, DN_HEADS, DN_DK, DN_DV), F32)),
        scratch_shapes=[pltpu.VMEM((DN_HEADS, DN_DK, DN_DV), F32), pltpu.VMEM((step, D_MODEL), F32)],
        compiler_params=pltpu.CompilerParams(dimension_semantics=("parallel", "arbitrary"),
                                             vmem_limit_bytes=VMEM_LIMIT),
        name="dn_rec",
    )(q, k, v, gb, s0, sz, o_gain, w_out, x)


def _pad_lanes(v, offset=0):
    return jnp.zeros((1, LANES), F32).at[0, offset:offset + v.shape[0]].set(v)


def _pack_att_w(w):
    d = w.shape[0]
    gate0 = 2 * D_MODEL + 2 * KV_W + IDX_HEADS * IDX_DIM + IDX_DIM + IDX_HEADS - D_MODEL
    return jnp.concatenate(
        [w[:, :gate0], jnp.zeros((d, A_GATE - gate0), w.dtype), w[:, gate0:]], axis=1).astype(BF16)


def _pack_dn_w(w):
    d = w.shape[0]
    return jnp.concatenate([w, jnp.zeros((d, N_END - w.shape[1]), w.dtype)], axis=1).astype(BF16)


def kernel(x_prompt, x_sample, cache_k, cache_v, cache_idx_k, state_dn_S, state_dn_conv, page_table,
           att_norm, att_w_in, att_q_gain, att_k_gain, att_ik_gain, att_w_out,
           dn_norm, dn_w_in, dn_conv_w, dn_A_log, dn_dt_bias, dn_o_gain, dn_w_out):
    bp, seq, d = x_prompt.shape
    bs, t_new, _ = x_sample.shape
    n_pool = cache_k.shape[0]
    np_rows, ns_rows = bp * seq, bs * t_new
    tm = 256

    xp = x_prompt.reshape(np_rows, d)
    xs = x_sample.reshape(ns_rows, d)

    w_att = _pack_att_w(att_w_in[0])
    a_norm = att_norm[0][None, :]
    qg, kg = att_q_gain[0][None, :], att_k_gain[0][None, :]
    ikg = _pad_lanes(att_ik_gain[0])
    w_ao = att_w_out[0].astype(BF16)

    q_p, k_p, kb_p, v_p, vt_p, iq_p, ikw_p, ikb_p, iwt_p, sg_p = _att_proj(xp, a_norm, w_att, qg, kg, ikg, tm)
    y1_p = _att_prompt(q_p, kb_p, vt_p, ikb_p, iq_p, iwt_p, sg_p, w_ao, xp, bp, seq)

    q_s, k_s, _, v_s, _, iq_s, ikw_s, ikb_s, _, sg_s = _att_proj(xs, a_norm, w_att, qg, kg, ikg, min(tm, ns_rows))
    pad_new = lambda a: jnp.pad(a.reshape(bs, t_new, a.shape[-1]), ((0, 0), (0, PAGE_SIZE - t_new), (0, 0)))
    o_s = _att_sample(page_table, q_s.astype(F32), iq_s.astype(F32), ikw_s, pad_new(ikb_s), pad_new(k_s), pad_new(v_s),
                      jnp.transpose(cache_idx_k, (0, 2, 3, 1)).reshape(n_pool, IDX_DIM, PAGE_SIZE),
                      cache_k.reshape(n_pool, PAGE_SIZE * N_KV_HEADS, HEAD_DIM),
                      cache_v.reshape(n_pool, PAGE_SIZE * N_KV_HEADS, HEAD_DIM), t_new)
    y1_s = _out_proj(o_s, sg_s, w_ao, xs, min(tm, ns_rows))

    w_dn = _pack_dn_w(dn_w_in[0])
    d_norm = dn_norm[0][None, :]
    alog = _pad_lanes(dn_A_log[0], DN_HEADS)
    dtb = _pad_lanes(dn_dt_bias[0], DN_HEADS)
    o_gain = dn_o_gain[0][None, :]
    w_do = dn_w_out[0].astype(BF16)
    hist = CONV_W - 1

    buf_p = jnp.zeros((bp, SUBLANES, CONV_DIM), F32)
    dq_p, dk_p, dv_p, sz_p, gb_p, cb_p = _dn_proj(y1_p, d_norm, w_dn, dn_conv_w[0], buf_p, alog, dtb,
                                                   n_groups=bp, n_seq=1, tm=tm)
    s0_p = jnp.zeros((bp, 1, DN_HEADS, DN_DK, DN_DV), F32)
    y2_p, s_p = _dn_rec(dq_p, dk_p, dv_p, gb_p, s0_p, sz_p, o_gain, w_do, y1_p, bp, seq, chunk=LANES,
                        chunks_per_step=4)

    to_tm = lambda a: a.reshape(bs, t_new, -1).transpose(1, 0, 2).reshape(ns_rows, -1)
    to_bm = lambda a: a.reshape(t_new, bs, -1).transpose(1, 0, 2).reshape(ns_rows, -1)
    buf_s = state_dn_conv[:, 0].transpose(1, 0, 2).reshape(1, hist * bs, CONV_DIM)
    dq_s, dk_s, dv_s, sz_s, gb_s, cb_s = _dn_proj(to_tm(y1_s), d_norm, w_dn, dn_conv_w[0], buf_s, alog, dtb,
                                                   n_groups=1, n_seq=bs, tm=ns_rows)
    y2_s, s_s = _dn_rec(to_bm(dq_s), to_bm(dk_s), to_bm(dv_s), to_bm(gb_s), state_dn_S,
                        to_bm(sz_s.astype(F32)), o_gain, w_do, y1_s, bs, t_new, chunk=t_new,
                        chunks_per_step=1)

    return (
        y2_p.reshape(bp, seq, d),
        y2_s.reshape(bs, t_new, d),
        k_p.reshape(bp, seq, 1, N_KV_HEADS, HEAD_DIM),
        v_p.reshape(bp, seq, 1, N_KV_HEADS, HEAD_DIM),
        ikw_p[:, :IDX_DIM].reshape(bp, seq, 1, IDX_DIM),
        k_s.reshape(bs, t_new, 1, N_KV_HEADS, HEAD_DIM),
        v_s.reshape(bs, t_new, 1, N_KV_HEADS, HEAD_DIM),
        ikw_s[:, :IDX_DIM].reshape(bs, t_new, 1, IDX_DIM),
        s_p,
        cb_p[:, SUBLANES - hist:, :].reshape(bp, 1, hist, CONV_DIM),
        s_s,
        cb_s.reshape(hist, bs, CONV_DIM).transpose(1, 0, 2).reshape(bs, 1, hist, CONV_DIM),
    )
```

```python
import functools

import jax
import jax.numpy as jnp
from jax import lax
from jax.experimental import pallas as pl
from jax.experimental.pallas import tpu as pltpu

F32 = jnp.float32
BF16 = jnp.bfloat16
I32 = jnp.int32

EPS = 1e-6
LANES = 128
SUBLANES = 8
VMEM_LIMIT = 56 * 1024 * 1024

D_MODEL = 1024
N_HEADS = 8
HEAD_DIM = 128
N_KV_HEADS = 2
GROUP = N_HEADS // N_KV_HEADS
IDX_HEADS = 8
IDX_DIM = 64
TOPK_MAX = 256
PAGE_SIZE = 128
DN_HEADS = 8
DN_DK = 128
DN_DV = 128
CONV_W = 4
KV_W = N_KV_HEADS * HEAD_DIM
CONV_DIM = 3 * DN_HEADS * DN_DK

A_Q, A_K, A_V, A_IQ, A_IKW, A_GATE, A_END = 0, 1024, 1280, 1536, 2048, 2176, 3200
N_Z, N_BA, N_END = 3072, 4096, 4224

INT_MIN = -2147483648
INT_MAX = 2147483647
NEG_BIG = -1e30
NT_DIMS = (((1,), (1,)), ((), ()))
HI = lax.Precision.HIGHEST
NEUMANN_SPLIT_LEVELS = 4


def _dot(a, b, precision=None):
    return jnp.dot(a, b, preferred_element_type=F32, precision=precision)


def _dot_nt(a, b, precision=None):
    return lax.dot_general(a, b, NT_DIMS, preferred_element_type=F32, precision=precision)


def _split_bf16(x):
    hi = x.astype(BF16)
    return hi, (x - hi.astype(F32)).astype(BF16)


def _dot3(a, b):
    (ah, al), (bh, bl) = a, b
    return _dot(ah, bh) + (_dot(ah, bl) + _dot(al, bh))


def _rms(x, n):
    return x * lax.rsqrt(jnp.sum(x * x, axis=-1, keepdims=True) * (1.0 / n) + EPS)


def _ordinal_to_float(u):
    key = u ^ jnp.int32(INT_MIN)
    return pltpu.bitcast(jnp.where(key < 0, key ^ jnp.int32(INT_MAX), key), F32)


def _att_proj_kernel(x_ref, g_ref, w_ref, qg_ref, kg_ref, ikg_ref,
                     q_ref, k_ref, kb_ref, v_ref, vt_ref, iq_ref, ikw_ref, ikb_ref, iwt_ref, sg_ref):
    x = x_ref[...]
    xn = _rms(x, D_MODEL) * g_ref[...]
    p = _dot(xn.astype(BF16), w_ref[...])
    for h in range(N_HEADS):
        qh = _rms(p[:, A_Q + h * HEAD_DIM:A_Q + (h + 1) * HEAD_DIM], HEAD_DIM) * qg_ref[...]
        q_ref[:, h * HEAD_DIM:(h + 1) * HEAD_DIM] = (qh * (HEAD_DIM ** -0.5)).astype(BF16)
    for g in range(N_KV_HEADS):
        kh = _rms(p[:, A_K + g * HEAD_DIM:A_K + (g + 1) * HEAD_DIM], HEAD_DIM) * kg_ref[...]
        k_ref[:, g * HEAD_DIM:(g + 1) * HEAD_DIM] = kh
        kb_ref[:, g * HEAD_DIM:(g + 1) * HEAD_DIM] = kh.astype(BF16)
    v = p[:, A_V:A_V + KV_W]
    v_ref[...] = v
    vt_ref[...] = v.T.astype(BF16)
    lane = lax.broadcasted_iota(I32, (1, LANES), 1)
    is_ik = lane < IDX_DIM
    for h in range(IDX_HEADS):
        iqh = p[:, A_IQ + (h // 2) * LANES:A_IQ + (h // 2 + 1) * LANES]
        if h % 2:
            iqh = pltpu.roll(iqh, IDX_DIM, axis=1)
        iq_ref[h] = jnp.where(is_ik, iqh, 0.0).astype(BF16)
    ikw = p[:, A_IKW:A_IKW + LANES]
    ik = jnp.where(is_ik, ikw, 0.0)
    ikn = _rms(ik, IDX_DIM) * ikg_ref[...]
    out = jnp.where(is_ik, ikn, ikw * (IDX_HEADS ** -0.5 * IDX_DIM ** -0.5))
    ikw_ref[...] = out
    ikb_ref[...] = ikn.astype(BF16)
    iwt_ref[...] = out.T[IDX_DIM:IDX_DIM + IDX_HEADS, :]
    gate = p[:, A_GATE:A_END]
    sg_ref[...] = (gate * jax.nn.sigmoid(gate)).astype(BF16)


def _att_proj(x, norm_g, w_packed, q_gain, k_gain, ik_gain_pad, tm):
    n = x.shape[0]
    assert n % tm == 0 and tm % LANES == 0
    row = lambda i: (i, 0)
    const = lambda i: (0, 0)
    out_shape = (
        jax.ShapeDtypeStruct((n, D_MODEL), BF16),
        jax.ShapeDtypeStruct((n, KV_W), F32),
        jax.ShapeDtypeStruct((n, KV_W), BF16),
        jax.ShapeDtypeStruct((n, KV_W), F32),
        jax.ShapeDtypeStruct((KV_W, n), BF16),
        jax.ShapeDtypeStruct((IDX_HEADS, n, LANES), BF16),
        jax.ShapeDtypeStruct((n, LANES), F32),
        jax.ShapeDtypeStruct((n, LANES), BF16),
        jax.ShapeDtypeStruct((IDX_HEADS, n), F32),
        jax.ShapeDtypeStruct((n, D_MODEL), BF16),
    )
    out_specs = (
        pl.BlockSpec((tm, D_MODEL), row),
        pl.BlockSpec((tm, KV_W), row),
        pl.BlockSpec((tm, KV_W), row),
        pl.BlockSpec((tm, KV_W), row),
        pl.BlockSpec((KV_W, tm), lambda i: (0, i)),
        pl.BlockSpec((IDX_HEADS, tm, LANES), lambda i: (0, i, 0)),
        pl.BlockSpec((tm, LANES), row),
        pl.BlockSpec((tm, LANES), row),
        pl.BlockSpec((IDX_HEADS, tm), lambda i: (0, i)),
        pl.BlockSpec((tm, D_MODEL), row),
    )
    return pl.pallas_call(
        _att_proj_kernel,
        grid=(n // tm,),
        in_specs=[
            pl.BlockSpec((tm, D_MODEL), row),
            pl.BlockSpec((1, D_MODEL), const),
            pl.BlockSpec((D_MODEL, A_END), const),
            pl.BlockSpec((1, HEAD_DIM), const),
            pl.BlockSpec((1, HEAD_DIM), const),
            pl.BlockSpec((1, LANES), const),
        ],
        out_specs=out_specs,
        out_shape=out_shape,
        compiler_params=pltpu.CompilerParams(dimension_semantics=("parallel",), vmem_limit_bytes=VMEM_LIMIT),
        name="att_proj",
    )(x, norm_g, w_packed, q_gain, k_gain, ik_gain_pad)


def _reduce(x, axis, combine, finish):
    n = x.shape[axis]
    unit = 8 * (SUBLANES if axis == 0 else LANES)
    full = n // unit
    if full < 2:
        return finish(x, axis=axis, keepdims=True)
    cut = (lambda i, j: x[i:j, :]) if axis == 0 else (lambda i, j: x[:, i:j])
    acc = cut(0, unit)
    for i in range(1, full):
        acc = combine(acc, cut(i * unit, (i + 1) * unit))
    out = finish(acc, axis=axis, keepdims=True)
    if n % unit:
        out = combine(out, finish(cut(full * unit, n), axis=axis, keepdims=True))
    return out


def _count(mask, axis):
    return _reduce(jnp.where(mask, 1.0, 0.0), axis, jnp.add, jnp.sum)


def _select_bias(sc_ref, lo_ref, n_keys, key_idx, topk, axis):
    sl = (slice(0, n_keys), slice(None)) if axis == 0 else (slice(None), slice(0, n_keys))
    vec = (1, sc_ref.shape[1]) if axis == 0 else (sc_ref.shape[0], 1)
    kf = float(topk)

    def bit_step(i, prefix):
        cand = prefix | (jnp.int32(1) << (31 - i))
        cnt = _count(sc_ref[sl] >= _ordinal_to_float(cand), axis)
        return jnp.where(cnt >= kf, cand, prefix)

    prefix = lax.fori_loop(0, 32, bit_step, jnp.zeros(vec, I32))
    thr = _ordinal_to_float(prefix)
    sc = sc_ref[sl]
    few = jnp.logical_not(_count(sc >= thr, axis) >= kf)
    thr = jnp.where(few, -jnp.inf, thr)
    need = kf - _count(sc > thr, axis)
    excess = (_count(sc >= thr, axis) > kf) & jnp.logical_not(few)
    lo_ref[...] = jnp.full(vec, INT_MAX, I32)

    @pl.when(jnp.max(jnp.where(excess, 1.0, 0.0)) > 0.0)
    def _():
        nbits = max(1, (n_keys - 1).bit_length())

        def idx_step(i, lo):
            cand = lo + (jnp.int32(1) << (nbits - 1 - i))
            cnt = _count((sc_ref[sl] == thr) & (key_idx < cand), axis)
            return jnp.where(cnt < need, cand, lo)

        lo_ref[...] = lax.fori_loop(0, nbits, idx_step, jnp.zeros(vec, I32))

    lo = jnp.where(few, -1, lo_ref[...])
    sel = (sc > thr) | ((sc == thr) & (key_idx <= lo))
    return jnp.where(sel, 0.0, NEG_BIG)


def _att_prompt_body(n_keys, topk, q_ref, kb_ref, vt_ref, ikb_ref, iq_ref, iwt_ref, o_ref,
                     sc_ref, bias_ref, lo_ref):
    qb = q_ref.shape[0]
    t0 = pl.program_id(0) * qb
    ik = ikb_ref[0:n_keys, :]
    iwt = iwt_ref[...]
    score = jnp.zeros((n_keys, qb), F32)
    for h in range(IDX_HEADS):
        s = _dot_nt(ik, iq_ref[h])
        score = score + jnp.maximum(s, 0.0) * iwt[h:h + 1, :]
    key_idx = lax.broadcasted_iota(I32, (n_keys, qb), 0)
    q_pos = t0 + lax.broadcasted_iota(I32, (n_keys, qb), 1)
    sc_ref[0:n_keys, :] = jnp.where(key_idx <= q_pos, score, -jnp.inf)

    bias_ref[0:n_keys, :] = _select_bias(sc_ref, lo_ref, n_keys, key_idx, topk, axis=0)
    for h in range(N_HEADS):
        g = h // GROUP
        s = _dot_nt(kb_ref[0:n_keys, g * HEAD_DIM:(g + 1) * HEAD_DIM],
                    q_ref[:, h * HEAD_DIM:(h + 1) * HEAD_DIM])
        s = s + bias_ref[0:n_keys, :]
        m = _reduce(s, 0, jnp.maximum, jnp.max)
        p = jnp.exp(s - m)
        l = _reduce(p, 0, jnp.add, jnp.sum)
        ot = _dot(vt_ref[g * HEAD_DIM:(g + 1) * HEAD_DIM, 0:n_keys], p.astype(BF16))
        o_ref[:, h * HEAD_DIM:(h + 1) * HEAD_DIM] = (ot / l).T


def _att_prompt_kernel(q_ref, kb_ref, vt_ref, ikb_ref, iq_ref, iwt_ref, sg_ref, w_ref, x_ref, y_ref,
                       o_scr, sc_ref, bias_ref, lo_ref, *, cls_len, n_cls, topk):
    qb = q_ref.shape[0]
    cls = (pl.program_id(0) * qb) // cls_len
    for c in range(n_cls):
        pl.when(cls == c)(functools.partial(
            _att_prompt_body, cls_len * (c + 1), topk, q_ref, kb_ref, vt_ref, ikb_ref, iq_ref, iwt_ref,
            o_scr, sc_ref, bias_ref, lo_ref))
    a = (o_scr[...] * sg_ref[...].astype(F32)).astype(BF16)
    y_ref[...] = x_ref[...] + _dot(a, w_ref[...])


def _att_prompt(q, kb, vt, ikb, iq, iwt, sg, w_out, x, batch, seq, qb=LANES):
    nb = seq // qb
    n_cls = min(8, nb)
    assert seq % (n_cls * qb) == 0
    topk = min(TOPK_MAX, seq // 4)
    kern = functools.partial(_att_prompt_kernel, cls_len=seq // n_cls, n_cls=n_cls, topk=topk)
    return pl.pallas_call(
        kern,
        grid=(nb, batch),
        in_specs=[
            pl.BlockSpec((qb, D_MODEL), lambda j, b: (b * nb + j, 0)),
            pl.BlockSpec((seq, KV_W), lambda j, b: (b, 0)),
            pl.BlockSpec((KV_W, seq), lambda j, b: (0, b)),
            pl.BlockSpec((seq, LANES), lambda j, b: (b, 0)),
            pl.BlockSpec((IDX_HEADS, qb, LANES), lambda j, b: (0, b * nb + j, 0)),
            pl.BlockSpec((IDX_HEADS, qb), lambda j, b: (0, b * nb + j)),
            pl.BlockSpec((qb, D_MODEL), lambda j, b: (b * nb + j, 0)),
            pl.BlockSpec((D_MODEL, D_MODEL), lambda j, b: (0, 0)),
            pl.BlockSpec((qb, D_MODEL), lambda j, b: (b * nb + j, 0)),
        ],
        out_specs=pl.BlockSpec((qb, D_MODEL), lambda j, b: (b * nb + j, 0)),
        out_shape=jax.ShapeDtypeStruct((batch * seq, D_MODEL), F32),
        scratch_shapes=[
            pltpu.VMEM((qb, D_MODEL), F32),
            pltpu.VMEM((seq, qb), F32),
            pltpu.VMEM((seq, qb), F32),
            pltpu.VMEM((1, qb), I32),
        ],
        compiler_params=pltpu.CompilerParams(dimension_semantics=("parallel", "arbitrary"),
                                             vmem_limit_bytes=VMEM_LIMIT),
        name="att_prompt",
    )(q, kb, vt, ikb, iq, iwt, sg, w_out, x)


def _att_sample_kernel(pt_ref, q_ref, iq_ref, ikw_ref, ikn_ref, kn_ref, vn_ref,
                       cik_hbm, ck_hbm, cv_hbm, o_ref,
                       ikbuf, kbuf, vbuf, sc_ref, bias_ref, lo_ref, sem_i, sem_kv,
                       *, n_pages, chunk_pages, topk):
    b = pl.program_id(0)
    t_new = q_ref.shape[0]
    past = n_pages * PAGE_SIZE
    n_keys = past + PAGE_SIZE
    n_chunks = n_pages // chunk_pages
    ck = chunk_pages * PAGE_SIZE

    n_seq = pl.num_programs(0)
    islot = b % 2
    ahead = n_chunks - 1

    def idx_copy(seq, slot, p):
        return pltpu.make_async_copy(cik_hbm.at[pt_ref[seq, p]],
                                     ikbuf.at[slot, :, pl.ds(p * PAGE_SIZE, PAGE_SIZE)], sem_i.at[slot])

    def kv_copies(seq, c, p):
        page = pt_ref[seq, c * chunk_pages + p]
        dst = pl.ds(p * PAGE_SIZE * N_KV_HEADS, PAGE_SIZE * N_KV_HEADS)
        return (pltpu.make_async_copy(ck_hbm.at[page], kbuf.at[c, dst], sem_kv.at[0, c]),
                pltpu.make_async_copy(cv_hbm.at[page], vbuf.at[c, dst], sem_kv.at[1, c]))

    def start_idx(seq, slot):
        def go(p, carry):
            idx_copy(seq, slot, p).start()
            return carry
        lax.fori_loop(0, n_pages, go, 0, unroll=8)

    def wait_idx(seq, slot):
        def go(p, carry):
            idx_copy(seq, slot, p).wait()
            return carry
        lax.fori_loop(0, n_pages, go, 0, unroll=8)

    def start_chunk(seq, c):
        def go(p, carry):
            for cp in kv_copies(seq, c, p):
                cp.start()
            return carry
        lax.fori_loop(0, chunk_pages, go, 0, unroll=8)

    def wait_chunk(seq, c):
        def go(p, carry):
            for cp in kv_copies(seq, c, p):
                cp.wait()
            return carry
        lax.fori_loop(0, chunk_pages, go, 0, unroll=8)

    @pl.when(b == 0)
    def _():
        start_idx(b, islot)
        for c in range(ahead):
            start_chunk(b, c)

    @pl.when(b + 1 < n_seq)
    def _():
        start_idx(b + 1, 1 - islot)

    wait_idx(b, islot)

    iq2 = iq_ref[...].reshape(IDX_HEADS * t_new, LANES).astype(BF16)
    ikw = ikw_ref[...]

    def scores(ik_rows):
        s = _dot_nt(iq2, ik_rows)
        acc = jnp.zeros((t_new, ik_rows.shape[0]), F32)
        for h in range(IDX_HEADS):
            acc = acc + jnp.maximum(s[h * t_new:(h + 1) * t_new, :], 0.0) * ikw[:, IDX_DIM + h:IDX_DIM + h + 1]
        return acc

    iq2_past = iq2[:, 0:IDX_DIM]
    for c in range(n_chunks):
        s = _dot(iq2_past, ikbuf[islot, :, c * ck:(c + 1) * ck].astype(BF16))
        acc = jnp.zeros((t_new, ck), F32)
        for h in range(IDX_HEADS):
            acc = acc + jnp.maximum(s[h * t_new:(h + 1) * t_new, :], 0.0) * ikw[:, IDX_DIM + h:IDX_DIM + h + 1]
        sc_ref[:, c * ck:(c + 1) * ck] = acc
    new_idx = lax.broadcasted_iota(I32, (t_new, PAGE_SIZE), 1)
    tok = lax.broadcasted_iota(I32, (t_new, PAGE_SIZE), 0)
    sc_ref[:, past:n_keys] = jnp.where(new_idx <= tok, scores(ikn_ref[...]), -jnp.inf)
    key_idx = lax.broadcasted_iota(I32, (t_new, n_keys), 1)
    bias_ref[...] = _select_bias(sc_ref, lo_ref, n_keys, key_idx, topk, axis=1)

    rows = GROUP * t_new
    qs = [jnp.concatenate([q_ref[:, (g * GROUP + r) * HEAD_DIM:(g * GROUP + r + 1) * HEAD_DIM]
                           for r in range(GROUP)], axis=0).astype(BF16) for g in range(N_KV_HEADS)]
    m = [jnp.full((rows, 1), NEG_BIG, F32) for _ in range(N_KV_HEADS)]
    l = [jnp.zeros((rows, 1), F32) for _ in range(N_KV_HEADS)]
    acc = [jnp.zeros((rows, HEAD_DIM), F32) for _ in range(N_KV_HEADS)]

    def attend(g, k_rows, v_rows, bias):
        s = _dot_nt(qs[g], k_rows) + jnp.concatenate([bias] * GROUP, axis=0)
        m_new = jnp.maximum(m[g], _reduce(s, 1, jnp.maximum, jnp.max))
        alpha = jnp.exp(m[g] - m_new)
        p = jnp.exp(s - m_new)
        l[g] = alpha * l[g] + _reduce(p, 1, jnp.add, jnp.sum)
        acc[g] = alpha * acc[g] + _dot(p.astype(BF16), v_rows)
        m[g] = m_new

    for g in range(N_KV_HEADS):
        hd = slice(g * HEAD_DIM, (g + 1) * HEAD_DIM)
        attend(g, kn_ref[:, hd].astype(BF16), vn_ref[:, hd].astype(BF16), bias_ref[:, past:n_keys])
    for c in range(n_chunks):
        nxt = c + ahead
        if nxt < n_chunks:
            start_chunk(b, nxt)
        else:
            pl.when(b + 1 < n_seq)(functools.partial(start_chunk, b + 1, nxt - n_chunks))
        wait_chunk(b, c)
        for g in range(N_KV_HEADS):
            head_rows = pl.ds(g, ck, stride=N_KV_HEADS)
            attend(g, kbuf[c, head_rows, :].astype(BF16), vbuf[c, head_rows, :].astype(BF16),
                   bias_ref[:, c * ck:(c + 1) * ck])
    for g in range(N_KV_HEADS):
        og = acc[g] / l[g]
        for r in range(GROUP):
            h = g * GROUP + r
            o_ref[:, h * HEAD_DIM:(h + 1) * HEAD_DIM] = og[r * t_new:(r + 1) * t_new, :]


def _att_sample(page_table, q, iq, ikw, ikn_pad, kn_pad, vn_pad, cache_ik, cache_k, cache_v, t_new):
    nseq, n_pages = page_table.shape
    assert t_new == SUBLANES
    chunk_pages = min(32, n_pages // 2)
    assert n_pages % chunk_pages == 0
    past = n_pages * PAGE_SIZE
    n_keys = past + PAGE_SIZE
    topk = min(TOPK_MAX, (past + t_new) // 4)
    ck = chunk_pages * PAGE_SIZE
    kern = functools.partial(_att_sample_kernel, n_pages=n_pages, chunk_pages=chunk_pages, topk=topk)
    grid_spec = pltpu.PrefetchScalarGridSpec(
        num_scalar_prefetch=1,
        grid=(nseq,),
        in_specs=[
            pl.BlockSpec((t_new, D_MODEL), lambda b, pt: (b, 0)),
            pl.BlockSpec((IDX_HEADS, t_new, LANES), lambda b, pt: (0, b, 0)),
            pl.BlockSpec((t_new, LANES), lambda b, pt: (b, 0)),
            pl.BlockSpec((None, PAGE_SIZE, LANES), lambda b, pt: (b, 0, 0)),
            pl.BlockSpec((None, PAGE_SIZE, KV_W), lambda b, pt: (b, 0, 0)),
            pl.BlockSpec((None, PAGE_SIZE, KV_W), lambda b, pt: (b, 0, 0)),
            pl.BlockSpec(memory_space=pl.ANY),
            pl.BlockSpec(memory_space=pl.ANY),
            pl.BlockSpec(memory_space=pl.ANY),
        ],
        out_specs=pl.BlockSpec((t_new, D_MODEL), lambda b, pt: (b, 0)),
        scratch_shapes=[
            pltpu.VMEM((2, IDX_DIM, past), F32),
            pltpu.VMEM((n_pages // chunk_pages, ck * N_KV_HEADS, HEAD_DIM), F32),
            pltpu.VMEM((n_pages // chunk_pages, ck * N_KV_HEADS, HEAD_DIM), F32),
            pltpu.VMEM((t_new, n_keys), F32),
            pltpu.VMEM((t_new, n_keys), F32),
            pltpu.VMEM((t_new, 1), I32),
            pltpu.SemaphoreType.DMA((2,)),
            pltpu.SemaphoreType.DMA((2, n_pages // chunk_pages)),
        ],
    )
    return pl.pallas_call(
        kern,
        grid_spec=grid_spec,
        out_shape=jax.ShapeDtypeStruct((nseq * t_new, D_MODEL), F32),
        compiler_params=pltpu.CompilerParams(dimension_semantics=("arbitrary",), vmem_limit_bytes=VMEM_LIMIT),
        name="att_sample",
    )(page_table, q, iq, ikw, ikn_pad, kn_pad, vn_pad, cache_ik, cache_k, cache_v)


def _out_proj_kernel(o_ref, sg_ref, w_ref, x_ref, y_ref):
    a = (o_ref[...] * sg_ref[...].astype(F32)).astype(BF16)
    y_ref[...] = x_ref[...] + _dot(a, w_ref[...])


def _out_proj(o, sg, w_out, x, tm):
    n = o.shape[0]
    row = pl.BlockSpec((tm, D_MODEL), lambda i: (i, 0))
    return pl.pallas_call(
        _out_proj_kernel,
        grid=(n // tm,),
        in_specs=[row, row, pl.BlockSpec((D_MODEL, D_MODEL), lambda i: (0, 0)), row],
        out_specs=row,
        out_shape=jax.ShapeDtypeStruct((n, D_MODEL), F32),
        compiler_params=pltpu.CompilerParams(dimension_semantics=("parallel",), vmem_limit_bytes=VMEM_LIMIT),
        name="out_proj",
    )(o, sg, w_out, x)


def _dn_proj_kernel(x_ref, g_ref, w_ref, cw_ref, buf_ref, alog_ref, dtb_ref,
                    q_ref, k_ref, v_ref, sz_ref, gb_ref, cbuf_ref, xp_ref, *, n_seq, pad):
    tm = x_ref.shape[0]

    @pl.when(pl.program_id(1) == 0)
    def _():
        xp_ref[0:pad, :] = buf_ref[...]

    xn = _rms(x_ref[...], D_MODEL) * g_ref[...]
    p = _dot(xn.astype(BF16), w_ref[...])
    xp_ref[pad:pad + tm, :] = p[:, 0:CONV_DIM]
    conv = xp_ref[pad:pad + tm, :] * cw_ref[CONV_W - 1:CONV_W, :]
    for i in range(1, CONV_W):
        conv = conv + xp_ref[pad - i * n_seq:pad - i * n_seq + tm, :] * cw_ref[CONV_W - 1 - i:CONV_W - i, :]
    act = conv * jax.nn.sigmoid(conv)
    for h in range(DN_HEADS):
        sl = slice(h * DN_DK, (h + 1) * DN_DK)
        qh = act[:, sl]
        q_ref[:, sl] = qh * lax.rsqrt(jnp.sum(qh * qh, axis=-1, keepdims=True) + EPS) * (DN_DK ** -0.5)
        kh = act[:, DN_HEADS * DN_DK + h * DN_DK:DN_HEADS * DN_DK + (h + 1) * DN_DK]
        k_ref[:, sl] = kh * lax.rsqrt(jnp.sum(kh * kh, axis=-1, keepdims=True) + EPS)
    v_ref[...] = act[:, 2 * DN_HEADS * DN_DK:CONV_DIM]
    z = p[:, N_Z:N_BA]
    sz_ref[...] = (z * jax.nn.sigmoid(z)).astype(BF16)
    ba = p[:, N_BA:N_END]
    sp_in = ba + dtb_ref[...]
    softplus = jnp.maximum(sp_in, 0.0) + jnp.log1p(jnp.exp(-jnp.abs(sp_in)))
    lane = lax.broadcasted_iota(I32, (1, LANES), 1)
    gb_ref[...] = jnp.where(lane < DN_HEADS, jax.nn.sigmoid(ba), -jnp.exp(alog_ref[...]) * softplus)
    cbuf_ref[...] = xp_ref[tm:tm + pad, :]
    xp_ref[0:pad, :] = xp_ref[tm:tm + pad, :]


def _dn_proj(x, norm_g, w_packed, conv_w, buf, alog_pad, dtb_pad, n_groups, n_seq, tm):
    n = x.shape[0]
    pad = buf.shape[1]
    tiles = n // n_groups // tm
    assert n == n_groups * tiles * tm and pad % SUBLANES == 0 and pad >= (CONV_W - 1) * n_seq
    row = lambda s, i: (s * tiles + i, 0)
    const = lambda s, i: (0, 0)
    big = lambda d: pl.BlockSpec((tm, d), row)
    return pl.pallas_call(
        functools.partial(_dn_proj_kernel, n_seq=n_seq, pad=pad),
        grid=(n_groups, tiles),
        in_specs=[
            big(D_MODEL),
            pl.BlockSpec((1, D_MODEL), const),
            pl.BlockSpec((D_MODEL, N_END), const),
            pl.BlockSpec((CONV_W, CONV_DIM), const),
            pl.BlockSpec((None, pad, CONV_DIM), lambda s, i: (s, 0, 0)),
            pl.BlockSpec((1, LANES), const),
            pl.BlockSpec((1, LANES), const),
        ],
        out_specs=(big(D_MODEL), big(D_MODEL), big(D_MODEL), big(D_MODEL), big(LANES),
                   pl.BlockSpec((None, pad, CONV_DIM), lambda s, i: (s, 0, 0))),
        out_shape=(
            jax.ShapeDtypeStruct((n, D_MODEL), F32),
            jax.ShapeDtypeStruct((n, D_MODEL), F32),
            jax.ShapeDtypeStruct((n, D_MODEL), F32),
            jax.ShapeDtypeStruct((n, D_MODEL), BF16),
            jax.ShapeDtypeStruct((n, LANES), F32),
            jax.ShapeDtypeStruct((n_groups, pad, CONV_DIM), F32),
        ),
        scratch_shapes=[pltpu.VMEM((pad + tm, CONV_DIM), F32)],
        compiler_params=pltpu.CompilerParams(dimension_semantics=("parallel", "arbitrary"),
                                             vmem_limit_bytes=VMEM_LIMIT),
        name="dn_proj",
    )(x, norm_g, w_packed, conv_w, buf, alog_pad, dtb_pad)


def _dn_rec_kernel(q_ref, k_ref, v_ref, gb_ref, s0_ref, sz_ref, og_ref, w_ref, x_ref, y_ref, sout_ref,
                   s_scr, a_scr, *, c_len):
    n_sub = q_ref.shape[0] // c_len

    @pl.when(pl.program_id(1) == 0)
    def _():
        s_scr[...] = s0_ref[...]

    ri = lax.broadcasted_iota(I32, (c_len, c_len), 0)
    ci = lax.broadcasted_iota(I32, (c_len, c_len), 1)
    incl = ri >= ci
    strict = ri > ci
    tril = jnp.where(incl, 1.0, 0.0)
    eye_c = jnp.where(ri == ci, 1.0, 0.0)
    eye_k = jnp.where(lax.broadcasted_iota(I32, (DN_DK, DN_DK), 0) == lax.broadcasted_iota(I32, (DN_DK, DN_DK), 1),
                      1.0, 0.0)
    lane = lax.broadcasted_iota(I32, (c_len, LANES), 1)
    n_sq = max(0, (c_len - 1).bit_length() - 1)

    tile_aligned = c_len % LANES == 0

    subs = range(n_sub)
    rows = [slice(s * c_len, (s + 1) * c_len) for s in subs]
    gbc = [gb_ref[rows[s], :] for s in subs]
    gcum = [_dot(tril, gbc[s], HI) for s in subs]
    gcum_t = [gcum[s].T if tile_aligned else None for s in subs]
    units = [(s, h) for s in subs for h in range(DN_HEADS)]
    n_u = range(len(units))
    at = [(rows[s], slice(h * DN_DK, (h + 1) * DN_DK)) for s, h in units]
    beta = [gbc[s][:, h:h + 1] for s, h in units]
    gcol = [gcum[s][:, DN_HEADS + h:DN_HEADS + h + 1] for s, h in units]
    if tile_aligned:
        grow = [gcum_t[s][DN_HEADS + h:DN_HEADS + h + 1, :] for s, h in units]
    else:
        grow = [_dot_nt(jnp.where(lane == DN_HEADS + h, 1.0, 0.0), gcum[s], HI) for s, h in units]
    glast = [gcol[i][c_len - 1:c_len, :] for i in n_u]
    decay = [jnp.where(incl, jnp.exp(jnp.where(incl, gcol[i] - grow[i], 0.0)), 0.0) for i in n_u]
    eg = [jnp.exp(gcol[i]) for i in n_u]
    a_mat = [beta[i] * _dot_nt(k_ref[at[i]], k_ref[at[i]]) * jnp.where(strict, decay[i], 0.0) for i in n_u]
    pw = [-a_mat[i] for i in n_u]
    t_inv = [eye_c - a_mat[i] for i in n_u]
    for level in range(n_sq):
        if level < NEUMANN_SPLIT_LEVELS:
            ps = [_split_bf16(pw[i]) for i in n_u]
            pw = [_dot3(ps[i], ps[i]) for i in n_u]
            ps = [_split_bf16(pw[i]) for i in n_u]
            t_inv = [t_inv[i] + _dot3(_split_bf16(t_inv[i]), ps[i]) for i in n_u]
        else:
            pw = [_dot(pw[i].astype(BF16), pw[i].astype(BF16)) for i in n_u]
            t_inv = [t_inv[i] + _dot(t_inv[i].astype(BF16), pw[i].astype(BF16)) for i in n_u]
    wu = [_dot(t_inv[i].astype(BF16),
               jnp.concatenate([beta[i] * eg[i] * k_ref[at[i]], beta[i] * v_ref[at[i]]], axis=1).astype(BF16))
          for i in n_u]
    qk = [_dot_nt(q_ref[at[i]], k_ref[at[i]]) * decay[i] for i in n_u]
    kd = [k_ref[at[i]] * jnp.exp(glast[i] - gcol[i]) for i in n_u]
    kdt = [kd[i].T if tile_aligned else _dot_nt(eye_k, kd[i], HI) for i in n_u]
    for i, (s, h) in enumerate(units):
        s_old = s_scr[h]
        u = wu[i][:, DN_DK:DN_DK + DN_DV] - _dot(wu[i][:, 0:DN_DK], s_old)
        o = eg[i] * _dot(q_ref[at[i]], s_old) + _dot(qk[i], u)
        a_scr[at[i]] = _rms(o, DN_DV) * og_ref[...] * sz_ref[at[i]].astype(F32)
        s_scr[h] = jnp.exp(glast[i]) * s_old + _dot(kdt[i], u)
    y_ref[...] = x_ref[...] + _dot(a_scr[...].astype(BF16), w_ref[...])

    @pl.when(pl.program_id(1) == pl.num_programs(1) - 1)
    def _():
        sout_ref[...] = s_scr[...]


def _dn_rec(q, k, v, gb, s0, sz, o_gain, w_out, x, n_seq, t_len, chunk, chunks_per_step):
    step = chunk * chunks_per_step
    assert t_len % step == 0
    n_steps = t_len // step
    row = lambda b, c: (b * n_steps + c, 0)
    const = lambda b, c: (0, 0)
    wide = pl.BlockSpec((step, D_MODEL), row)
    st = pl.BlockSpec((None, None, DN_HEADS, DN_DK, DN_DV), lambda b, c: (b, 0, 0, 0, 0))
    return pl.pallas_call(
        functools.partial(_dn_rec_kernel, c_len=chunk),
        grid=(n_seq, n_steps),
        in_specs=[wide, wide, wide, pl.BlockSpec((step, LANES), row), st, wide,
                  pl.BlockSpec((1, DN_DV), const), pl.BlockSpec((D_MODEL, D_MODEL), const), wide],
        out_specs=(wide, st),
        out_shape=(jax.ShapeDtypeStruct((n_seq * t_len, D_MODEL), F32),
                   jax.ShapeDtypeStruct((n_seq, 1, DN_HEADS, DN_DK, DN_DV), F32)),
        scratch_shapes=[pltpu.VMEM((DN_HEADS, DN_DK, DN_DV), F32), pltpu.VMEM((step, D_MODEL), F32)],
        compiler_params=pltpu.CompilerParams(dimension_semantics=("parallel", "arbitrary"),
                                             vmem_limit_bytes=VMEM_LIMIT),
        name="dn_rec",
    )(q, k, v, gb, s0, sz, o_gain, w_out, x)


def _pad_lanes(v, offset=0):
    return jnp.zeros((1, LANES), F32).at[0, offset:offset + v.shape[0]].set(v)


def _pack_att_w(w):
    d = w.shape[0]
    gate0 = 2 * D_MODEL + 2 * KV_W + IDX_HEADS * IDX_DIM + IDX_DIM + IDX_HEADS - D_MODEL
    return jnp.concatenate(
        [w[:, :gate0], jnp.zeros((d, A_GATE - gate0), w.dtype), w[:, gate0:]], axis=1).astype(BF16)


def _pack_dn_w(w):
    d = w.shape[0]
    return jnp.concatenate([w, jnp.zeros((d, N_END - w.shape[1]), w.dtype)], axis=1).astype(BF16)


def kernel(x_prompt, x_sample, cache_k, cache_v, cache_idx_k, state_dn_S, state_dn_conv, page_table,
           att_norm, att_w_in, att_q_gain, att_k_gain, att_ik_gain, att_w_out,
           dn_norm, dn_w_in, dn_conv_w, dn_A_log, dn_dt_bias, dn_o_gain, dn_w_out):
    bp, seq, d = x_prompt.shape
    bs, t_new, _ = x_sample.shape
    n_pool = cache_k.shape[0]
    np_rows, ns_rows = bp * seq, bs * t_new
    tm = 256

    xp = x_prompt.reshape(np_rows, d)
    xs = x_sample.reshape(ns_rows, d)

    w_att = _pack_att_w(att_w_in[0])
    a_norm = att_norm[0][None, :]
    qg, kg = att_q_gain[0][None, :], att_k_gain[0][None, :]
    ikg = _pad_lanes(att_ik_gain[0])
    w_ao = att_w_out[0].astype(BF16)

    q_p, k_p, kb_p, v_p, vt_p, iq_p, ikw_p, ikb_p, iwt_p, sg_p = _att_proj(xp, a_norm, w_att, qg, kg, ikg, tm)
    y1_p = _att_prompt(q_p, kb_p, vt_p, ikb_p, iq_p, iwt_p, sg_p, w_ao, xp, bp, seq)

    q_s, k_s, _, v_s, _, iq_s, ikw_s, ikb_s, _, sg_s = _att_proj(xs, a_norm, w_att, qg, kg, ikg, min(tm, ns_rows))
    pad_new = lambda a: jnp.pad(a.reshape(bs, t_new, a.shape[-1]), ((0, 0), (0, PAGE_SIZE - t_new), (0, 0)))
    o_s = _att_sample(page_table, q_s.astype(F32), iq_s.astype(F32), ikw_s, pad_new(ikb_s), pad_new(k_s), pad_new(v_s),
                      jnp.transpose(cache_idx_k, (0, 2, 3, 1)).reshape(n_pool, IDX_DIM, PAGE_SIZE),
                      cache_k.reshape(n_pool, PAGE_SIZE * N_KV_HEADS, HEAD_DIM),
                      cache_v.reshape(n_pool, PAGE_SIZE * N_KV_HEADS, HEAD_DIM), t_new)
    y1_s = _out_proj(o_s, sg_s, w_ao, xs, min(tm, ns_rows))

    w_dn = _pack_dn_w(dn_w_in[0])
    d_norm = dn_norm[0][None, :]
    alog = _pad_lanes(dn_A_log[0], DN_HEADS)
    dtb = _pad_lanes(dn_dt_bias[0], DN_HEADS)
    o_gain = dn_o_gain[0][None, :]
    w_do = dn_w_out[0].astype(BF16)
    hist = CONV_W - 1

    buf_p = jnp.zeros((bp, SUBLANES, CONV_DIM), F32)
    dq_p, dk_p, dv_p, sz_p, gb_p, cb_p = _dn_proj(y1_p, d_norm, w_dn, dn_conv_w[0], buf_p, alog, dtb,
                                                   n_groups=bp, n_seq=1, tm=tm)
    s0_p = jnp.zeros((bp, 1, DN_HEADS, DN_DK, DN_DV), F32)
    y2_p, s_p = _dn_rec(dq_p, dk_p, dv_p, gb_p, s0_p, sz_p, o_gain, w_do, y1_p, bp, seq, chunk=LANES,
                        chunks_per_step=4)

    to_tm = lambda a: a.reshape(bs, t_new, -1).transpose(1, 0, 2).reshape(ns_rows, -1)
    to_bm = lambda a: a.reshape(t_new, bs, -1).transpose(1, 0, 2).reshape(ns_rows, -1)
    buf_s = state_dn_conv[:, 0].transpose(1, 0, 2).reshape(1, hist * bs, CONV_DIM)
    dq_s, dk_s, dv_s, sz_s, gb_s, cb_s = _dn_proj(to_tm(y1_s), d_norm, w_dn, dn_conv_w[0], buf_s, alog, dtb,
                                                   n_groups=1, n_seq=bs, tm=ns_rows)
    y2_s, s_s = _dn_rec(to_bm(dq_s), to_bm(dk_s), to_bm(dv_s), to_bm(gb_s), state_dn_S,
                        to_bm(sz_s.astype(F32)), o_gain, w_do, y1_s, bs, t_new, chunk=t_new,
                        chunks_per_step=1)

    return (
        y2_p.reshape(bp, seq, d),
        y2_s.reshape(bs, t_new, d),
        k_p.reshape(bp, seq, 1, N_KV_HEADS, HEAD_DIM),
        v_p.reshape(bp, seq, 1, N_KV_HEADS, HEAD_DIM),
        ikw_p[:, :IDX_DIM].reshape(bp, seq, 1, IDX_DIM),
        k_s.reshape(bs, t_new, 1, N_KV_HEADS, HEAD_DIM),
        v_s.reshape(bs, t_new, 1, N_KV_HEADS, HEAD_DIM),
        ikw_s[:, :IDX_DIM].reshape(bs, t_new, 1, IDX_DIM),
        s_p,
        cb_p[:, SUBLANES - hist:, :].reshape(bp, 1, hist, CONV_DIM),
        s_s,
        cb_s.reshape(hist, bs, CONV_DIM).transpose(1, 0, 2).reshape(bs, 1, hist, CONV_DIM),
    )
```

```python
import functools

import jax
import jax.numpy as jnp
from jax import lax
from jax.experimental import pallas as pl
from jax.experimental.pallas import tpu as pltpu

F32 = jnp.float32
BF16 = jnp.bfloat16
I32 = jnp.int32

EPS = 1e-6
LANES = 128
SUBLANES = 8
VMEM_LIMIT = 56 * 1024 * 1024

D_MODEL = 1024
N_HEADS = 8
HEAD_DIM = 128
N_KV_HEADS = 2
GROUP = N_HEADS // N_KV_HEADS
IDX_HEADS = 8
IDX_DIM = 64
TOPK_MAX = 256
PAGE_SIZE = 128
DN_HEADS = 8
DN_DK = 128
DN_DV = 128
CONV_W = 4
KV_W = N_KV_HEADS * HEAD_DIM
CONV_DIM = 3 * DN_HEADS * DN_DK

A_Q, A_K, A_V, A_IQ, A_IKW = 0, 1024, 1280, 1536, 2048
N_Z, N_BA = 3072, 4096

INT_MIN = -2147483648
INT_MAX = 2147483647
NEG_BIG = -1e30
NT_DIMS = (((1,), (1,)), ((), ()))
HI = lax.Precision.HIGHEST
NEUMANN_SPLIT_LEVELS = 4


def _dot(a, b, precision=None):
    return jnp.dot(a, b, preferred_element_type=F32, precision=precision)


def _dot_nt(a, b, precision=None):
    return lax.dot_general(a, b, NT_DIMS, preferred_element_type=F32, precision=precision)


def _split_bf16(x):
    hi = x.astype(BF16)
    return hi, (x - hi.astype(F32)).astype(BF16)


def _dot3(a, b):
    (ah, al), (bh, bl) = a, b
    return _dot(ah, bh) + (_dot(ah, bl) + _dot(al, bh))


def _rms(x, n):
    return x * lax.rsqrt(jnp.sum(x * x, axis=-1, keepdims=True) * (1.0 / n) + EPS)


def _ordinal_to_float(u):
    key = u ^ jnp.int32(INT_MIN)
    return pltpu.bitcast(jnp.where(key < 0, key ^ jnp.int32(INT_MAX), key), F32)


def _att_proj_kernel(x_ref, g_ref, w_ref, wikw_ref, wg_ref, qg_ref, kg_ref, ikg_ref,
                     q_ref, k_ref, kb_ref, v_ref, vt_ref, iq_ref, ikw_ref, ikb_ref, iwt_ref, sg_ref):
    tm = x_ref.shape[0]
    x = x_ref[...]
    xn = (_rms(x, D_MODEL) * g_ref[...]).astype(BF16)
    p = _dot(xn, w_ref[...])
    for h in range(N_HEADS):
        qh = _rms(p[:, A_Q + h * HEAD_DIM:A_Q + (h + 1) * HEAD_DIM], HEAD_DIM) * qg_ref[...]
        q_ref[:, h * HEAD_DIM:(h + 1) * HEAD_DIM] = (qh * (HEAD_DIM ** -0.5)).astype(BF16)
    v = p[:, A_V:A_V + KV_W]
    for g in range(N_KV_HEADS):
        kh = _rms(p[:, A_K + g * HEAD_DIM:A_K + (g + 1) * HEAD_DIM], HEAD_DIM) * kg_ref[...]
        k_ref[pl.ds(g, tm, stride=N_KV_HEADS), :] = kh
        kb_ref[:, g * HEAD_DIM:(g + 1) * HEAD_DIM] = kh.astype(BF16)
        v_ref[pl.ds(g, tm, stride=N_KV_HEADS), :] = v[:, g * HEAD_DIM:(g + 1) * HEAD_DIM]
    vt_ref[...] = v.T.astype(BF16)
    lane = lax.broadcasted_iota(I32, (1, LANES), 1)
    is_ik = lane < IDX_DIM
    for h in range(IDX_HEADS):
        iqh = p[:, A_IQ + (h // 2) * LANES:A_IQ + (h // 2 + 1) * LANES]
        if h % 2:
            iqh = pltpu.roll(iqh, IDX_DIM, axis=1)
        iq_ref[h] = jnp.where(is_ik, iqh, 0.0).astype(BF16)
    ikw = _dot(xn, wikw_ref[...])
    ik = jnp.where(is_ik, ikw, 0.0)
    ikn = _rms(ik, IDX_DIM) * ikg_ref[...]
    out = jnp.where(is_ik, ikn, ikw * (IDX_HEADS ** -0.5 * IDX_DIM ** -0.5))
    ikw_ref[...] = out
    ikb_ref[...] = ikn.astype(BF16)
    iwt_ref[...] = out.T[IDX_DIM:IDX_DIM + IDX_HEADS, :]
    gate = _dot(xn, wg_ref[...])
    sg_ref[...] = (gate * jax.nn.sigmoid(gate)).astype(BF16)


def _att_proj(x, norm_g, w_main, w_ikw, w_gate, q_gain, k_gain, ik_gain_pad, tm):
    n = x.shape[0]
    assert n % tm == 0 and tm % LANES == 0
    row = lambda i: (i, 0)
    const = lambda i: (0, 0)
    out_shape = (
        jax.ShapeDtypeStruct((n, D_MODEL), BF16),
        jax.ShapeDtypeStruct((N_KV_HEADS * n, HEAD_DIM), F32),
        jax.ShapeDtypeStruct((n, KV_W), BF16),
        jax.ShapeDtypeStruct((N_KV_HEADS * n, HEAD_DIM), F32),
        jax.ShapeDtypeStruct((KV_W, n), BF16),
        jax.ShapeDtypeStruct((IDX_HEADS, n, LANES), BF16),
        jax.ShapeDtypeStruct((n, LANES), F32),
        jax.ShapeDtypeStruct((n, LANES), BF16),
        jax.ShapeDtypeStruct((IDX_HEADS, n), F32),
        jax.ShapeDtypeStruct((n, D_MODEL), BF16),
    )
    out_specs = (
        pl.BlockSpec((tm, D_MODEL), row),
        pl.BlockSpec((N_KV_HEADS * tm, HEAD_DIM), row),
        pl.BlockSpec((tm, KV_W), row),
        pl.BlockSpec((N_KV_HEADS * tm, HEAD_DIM), row),
        pl.BlockSpec((KV_W, tm), lambda i: (0, i)),
        pl.BlockSpec((IDX_HEADS, tm, LANES), lambda i: (0, i, 0)),
        pl.BlockSpec((tm, LANES), row),
        pl.BlockSpec((tm, LANES), row),
        pl.BlockSpec((IDX_HEADS, tm), lambda i: (0, i)),
        pl.BlockSpec((tm, D_MODEL), row),
    )
    return pl.pallas_call(
        _att_proj_kernel,
        grid=(n // tm,),
        in_specs=[
            pl.BlockSpec((tm, D_MODEL), row),
            pl.BlockSpec((1, D_MODEL), const),
            pl.BlockSpec((D_MODEL, A_IKW), const),
            pl.BlockSpec((D_MODEL, LANES), const),
            pl.BlockSpec((D_MODEL, D_MODEL), const),
            pl.BlockSpec((1, HEAD_DIM), const),
            pl.BlockSpec((1, HEAD_DIM), const),
            pl.BlockSpec((1, LANES), const),
        ],
        out_specs=out_specs,
        out_shape=out_shape,
        compiler_params=pltpu.CompilerParams(dimension_semantics=("parallel",), vmem_limit_bytes=VMEM_LIMIT),
        name="att_proj",
    )(x, norm_g, w_main, w_ikw, w_gate, q_gain, k_gain, ik_gain_pad)


def _reduce(x, axis, combine, finish):
    n = x.shape[axis]
    unit = 8 * (SUBLANES if axis == 0 else LANES)
    full = n // unit
    if full < 2:
        return finish(x, axis=axis, keepdims=True)
    cut = (lambda i, j: x[i:j, :]) if axis == 0 else (lambda i, j: x[:, i:j])
    acc = cut(0, unit)
    for i in range(1, full):
        acc = combine(acc, cut(i * unit, (i + 1) * unit))
    out = finish(acc, axis=axis, keepdims=True)
    if n % unit:
        out = combine(out, finish(cut(full * unit, n), axis=axis, keepdims=True))
    return out


def _count(mask, axis):
    return _reduce(jnp.where(mask, 1.0, 0.0), axis, jnp.add, jnp.sum)


def _select_bias(sc_ref, lo_ref, n_keys, key_idx, topk, axis):
    sl = (slice(0, n_keys), slice(None)) if axis == 0 else (slice(None), slice(0, n_keys))
    vec = (1, sc_ref.shape[1]) if axis == 0 else (sc_ref.shape[0], 1)
    kf = float(topk)

    def bit_step(i, prefix):
        cand = prefix | (jnp.int32(1) << (31 - i))
        cnt = _count(sc_ref[sl] >= _ordinal_to_float(cand), axis)
        return jnp.where(cnt >= kf, cand, prefix)

    prefix = lax.fori_loop(0, 32, bit_step, jnp.zeros(vec, I32))
    thr = _ordinal_to_float(prefix)
    sc = sc_ref[sl]
    few = jnp.logical_not(_count(sc >= thr, axis) >= kf)
    thr = jnp.where(few, -jnp.inf, thr)
    need = kf - _count(sc > thr, axis)
    excess = (_count(sc >= thr, axis) > kf) & jnp.logical_not(few)
    lo_ref[...] = jnp.full(vec, INT_MAX, I32)

    @pl.when(jnp.max(jnp.where(excess, 1.0, 0.0)) > 0.0)
    def _():
        nbits = max(1, (n_keys - 1).bit_length())

        def idx_step(i, lo):
            cand = lo + (jnp.int32(1) << (nbits - 1 - i))
            cnt = _count((sc_ref[sl] == thr) & (key_idx < cand), axis)
            return jnp.where(cnt < need, cand, lo)

        lo_ref[...] = lax.fori_loop(0, nbits, idx_step, jnp.zeros(vec, I32))

    lo = jnp.where(few, -1, lo_ref[...])
    sel = (sc > thr) | ((sc == thr) & (key_idx <= lo))
    return jnp.where(sel, 0.0, NEG_BIG)


def _att_prompt_body(n_keys, topk, q_ref, kb_ref, vt_ref, ikb_ref, iq_ref, iwt_ref, o_ref,
                     sc_ref, bias_ref, lo_ref):
    qb = q_ref.shape[0]
    t0 = pl.program_id(0) * qb
    ik = ikb_ref[0:n_keys, :]
    iwt = iwt_ref[...]
    score = jnp.zeros((n_keys, qb), F32)
    for h in range(IDX_HEADS):
        s = _dot_nt(ik, iq_ref[h])
        score = score + jnp.maximum(s, 0.0) * iwt[h:h + 1, :]
    key_idx = lax.broadcasted_iota(I32, (n_keys, qb), 0)
    q_pos = t0 + lax.broadcasted_iota(I32, (n_keys, qb), 1)
    sc_ref[0:n_keys, :] = jnp.where(key_idx <= q_pos, score, -jnp.inf)
    bias_ref[0:n_keys, :] = _select_bias(sc_ref, lo_ref, n_keys, key_idx, topk, axis=0)
    heads = range(N_HEADS)
    hd = [slice(h * HEAD_DIM, (h + 1) * HEAD_DIM) for h in heads]
    gd = [slice((h // GROUP) * HEAD_DIM, (h // GROUP + 1) * HEAD_DIM) for h in heads]
    s = [_dot_nt(kb_ref[0:n_keys, gd[h]], q_ref[:, hd[h]]) + bias_ref[0:n_keys, :] for h in heads]
    m = [_reduce(s[h], 0, jnp.maximum, jnp.max) for h in heads]
    p = [jnp.exp(s[h] - m[h]) for h in heads]
    l = [_reduce(p[h], 0, jnp.add, jnp.sum) for h in heads]
    ot = [_dot(vt_ref[gd[h], 0:n_keys], p[h].astype(BF16)) for h in heads]
    for h in heads:
        o_ref[:, hd[h]] = (ot[h] / l[h]).T


def _att_prompt_kernel(q_ref, kb_ref, vt_ref, ikb_ref, iq_ref, iwt_ref, sg_ref, w_ref, x_ref, y_ref,
                       o_scr, sc_ref, bias_ref, lo_ref, *, cls_len, n_cls, topk):
    qb = q_ref.shape[0]
    cls = (pl.program_id(0) * qb) // cls_len
    for c in range(n_cls):
        pl.when(cls == c)(functools.partial(
            _att_prompt_body, cls_len * (c + 1), topk, q_ref, kb_ref, vt_ref, ikb_ref, iq_ref, iwt_ref,
            o_scr, sc_ref, bias_ref, lo_ref))
    a = (o_scr[...] * sg_ref[...].astype(F32)).astype(BF16)
    y_ref[...] = x_ref[...] + _dot(a, w_ref[...])


def _att_prompt(q, kb, vt, ikb, iq, iwt, sg, w_out, x, batch, seq, qb=LANES):
    nb = seq // qb
    n_cls = min(8, nb)
    assert seq % (n_cls * qb) == 0
    topk = min(TOPK_MAX, seq // 4)
    kern = functools.partial(_att_prompt_kernel, cls_len=seq // n_cls, n_cls=n_cls, topk=topk)
    return pl.pallas_call(
        kern,
        grid=(nb, batch),
        in_specs=[
            pl.BlockSpec((qb, D_MODEL), lambda j, b: (b * nb + j, 0)),
            pl.BlockSpec((seq, KV_W), lambda j, b: (b, 0)),
            pl.BlockSpec((KV_W, seq), lambda j, b: (0, b)),
            pl.BlockSpec((seq, LANES), lambda j, b: (b, 0)),
            pl.BlockSpec((IDX_HEADS, qb, LANES), lambda j, b: (0, b * nb + j, 0)),
            pl.BlockSpec((IDX_HEADS, qb), lambda j, b: (0, b * nb + j)),
            pl.BlockSpec((qb, D_MODEL), lambda j, b: (b * nb + j, 0)),
            pl.BlockSpec((D_MODEL, D_MODEL), lambda j, b: (0, 0)),
            pl.BlockSpec((qb, D_MODEL), lambda j, b: (b * nb + j, 0)),
        ],
        out_specs=pl.BlockSpec((qb, D_MODEL), lambda j, b: (b * nb + j, 0)),
        out_shape=jax.ShapeDtypeStruct((batch * seq, D_MODEL), F32),
        scratch_shapes=[
            pltpu.VMEM((qb, D_MODEL), F32),
            pltpu.VMEM((seq, qb), F32),
            pltpu.VMEM((seq, qb), F32),
            pltpu.VMEM((1, qb), I32),
        ],
        compiler_params=pltpu.CompilerParams(dimension_semantics=("parallel", "arbitrary"),
                                             vmem_limit_bytes=VMEM_LIMIT),
        name="att_prompt",
    )(q, kb, vt, ikb, iq, iwt, sg, w_out, x)


def _att_sample_kernel(pt_ref, q_ref, iq_ref, ikw_ref, ikn_ref, kn_ref, vn_ref,
                       cik_hbm, ck_hbm, cv_hbm, o_ref,
                       ikbuf, kbuf, vbuf, sc_ref, bias_ref, lo_ref, sem_i, sem_kv,
                       *, n_pages, chunk_pages, topk):
    b = pl.program_id(0)
    t_new = q_ref.shape[0]
    past = n_pages * PAGE_SIZE
    n_keys = past + PAGE_SIZE
    n_chunks = n_pages // chunk_pages
    ck = chunk_pages * PAGE_SIZE

    n_seq = pl.num_programs(0)
    islot = b % 2
    ahead = n_chunks - 1

    def idx_copy(seq, slot, p):
        return pltpu.make_async_copy(cik_hbm.at[pt_ref[seq, p]],
                                     ikbuf.at[slot, :, pl.ds(p * PAGE_SIZE, PAGE_SIZE)], sem_i.at[slot])

    def kv_copies(seq, c, p):
        page = pt_ref[seq, c * chunk_pages + p]
        dst = pl.ds(p * PAGE_SIZE * N_KV_HEADS, PAGE_SIZE * N_KV_HEADS)
        return (pltpu.make_async_copy(ck_hbm.at[page], kbuf.at[c, dst], sem_kv.at[0, c]),
                pltpu.make_async_copy(cv_hbm.at[page], vbuf.at[c, dst], sem_kv.at[1, c]))

    def start_idx(seq, slot):
        def go(p, carry):
            idx_copy(seq, slot, p).start()
            return carry
        lax.fori_loop(0, n_pages, go, 0, unroll=8)

    def wait_idx(seq, slot):
        def go(p, carry):
            idx_copy(seq, slot, p).wait()
            return carry
        lax.fori_loop(0, n_pages, go, 0, unroll=8)

    def start_chunk(seq, c):
        def go(p, carry):
            for cp in kv_copies(seq, c, p):
                cp.start()
            return carry
        lax.fori_loop(0, chunk_pages, go, 0, unroll=8)

    def wait_chunk(seq, c):
        def go(p, carry):
            for cp in kv_copies(seq, c, p):
                cp.wait()
            return carry
        lax.fori_loop(0, chunk_pages, go, 0, unroll=8)

    @pl.when(b == 0)
    def _():
        start_idx(b, islot)
        for c in range(ahead):
            start_chunk(b, c)

    @pl.when(b + 1 < n_seq)
    def _():
        start_idx(b + 1, 1 - islot)

    wait_idx(b, islot)

    iq2 = iq_ref[...].reshape(IDX_HEADS * t_new, LANES).astype(BF16)
    ikw = ikw_ref[...]

    def scores(ik_rows):
        s = _dot_nt(iq2, ik_rows)
        acc = jnp.zeros((t_new, ik_rows.shape[0]), F32)
        for h in range(IDX_HEADS):
            acc = acc + jnp.maximum(s[h * t_new:(h + 1) * t_new, :], 0.0) * ikw[:, IDX_DIM + h:IDX_DIM + h + 1]
        return acc

    iq2_past = iq2[:, 0:IDX_DIM]
    for c in range(n_chunks):
        s = _dot(iq2_past, ikbuf[islot, :, c * ck:(c + 1) * ck].astype(BF16))
        acc = jnp.zeros((t_new, ck), F32)
        for h in range(IDX_HEADS):
            acc = acc + jnp.maximum(s[h * t_new:(h + 1) * t_new, :], 0.0) * ikw[:, IDX_DIM + h:IDX_DIM + h + 1]
        sc_ref[:, c * ck:(c + 1) * ck] = acc
    new_idx = lax.broadcasted_iota(I32, (t_new, PAGE_SIZE), 1)
    tok = lax.broadcasted_iota(I32, (t_new, PAGE_SIZE), 0)
    sc_ref[:, past:n_keys] = jnp.where(new_idx <= tok, scores(ikn_ref[...]), -jnp.inf)
    key_idx = lax.broadcasted_iota(I32, (t_new, n_keys), 1)
    bias_ref[...] = _select_bias(sc_ref, lo_ref, n_keys, key_idx, topk, axis=1)

    rows = GROUP * t_new
    qs = [jnp.concatenate([q_ref[:, (g * GROUP + r) * HEAD_DIM:(g * GROUP + r + 1) * HEAD_DIM]
                           for r in range(GROUP)], axis=0).astype(BF16) for g in range(N_KV_HEADS)]
    m = [jnp.full((rows, 1), NEG_BIG, F32) for _ in range(N_KV_HEADS)]
    l = [jnp.zeros((rows, 1), F32) for _ in range(N_KV_HEADS)]
    acc = [jnp.zeros((rows, HEAD_DIM), F32) for _ in range(N_KV_HEADS)]

    def attend(g, k_rows, v_rows, bias):
        s = _dot_nt(qs[g], k_rows) + jnp.concatenate([bias] * GROUP, axis=0)
        m_new = jnp.maximum(m[g], _reduce(s, 1, jnp.maximum, jnp.max))
        alpha = jnp.exp(m[g] - m_new)
        p = jnp.exp(s - m_new)
        l[g] = alpha * l[g] + _reduce(p, 1, jnp.add, jnp.sum)
        acc[g] = alpha * acc[g] + _dot(p.astype(BF16), v_rows)
        m[g] = m_new

    for g in range(N_KV_HEADS):
        new_rows = pl.ds(g, PAGE_SIZE, stride=N_KV_HEADS)
        attend(g, kn_ref[new_rows, :].astype(BF16), vn_ref[new_rows, :].astype(BF16), bias_ref[:, past:n_keys])
    for c in range(n_chunks):
        nxt = c + ahead
        if nxt < n_chunks:
            start_chunk(b, nxt)
        else:
            pl.when(b + 1 < n_seq)(functools.partial(start_chunk, b + 1, nxt - n_chunks))
        wait_chunk(b, c)
        for g in range(N_KV_HEADS):
            head_rows = pl.ds(g, ck, stride=N_KV_HEADS)
            attend(g, kbuf[c, head_rows, :].astype(BF16), vbuf[c, head_rows, :].astype(BF16),
                   bias_ref[:, c * ck:(c + 1) * ck])
    for g in range(N_KV_HEADS):
        og = acc[g] / l[g]
        for r in range(GROUP):
            h = g * GROUP + r
            o_ref[:, h * HEAD_DIM:(h + 1) * HEAD_DIM] = og[r * t_new:(r + 1) * t_new, :]


def _att_sample(page_table, q, iq, ikw, ikn_pad, kn_pad, vn_pad, cache_ik, cache_k, cache_v, t_new):
    nseq, n_pages = page_table.shape
    assert t_new == SUBLANES
    chunk_pages = min(32, n_pages // 2)
    assert n_pages % chunk_pages == 0
    past = n_pages * PAGE_SIZE
    n_keys = past + PAGE_SIZE
    topk = min(TOPK_MAX, (past + t_new) // 4)
    ck = chunk_pages * PAGE_SIZE
    kern = functools.partial(_att_sample_kernel, n_pages=n_pages, chunk_pages=chunk_pages, topk=topk)
    grid_spec = pltpu.PrefetchScalarGridSpec(
        num_scalar_prefetch=1,
        grid=(nseq,),
        in_specs=[
            pl.BlockSpec((t_new, D_MODEL), lambda b, pt: (b, 0)),
            pl.BlockSpec((IDX_HEADS, t_new, LANES), lambda b, pt: (0, b, 0)),
            pl.BlockSpec((t_new, LANES), lambda b, pt: (b, 0)),
            pl.BlockSpec((None, PAGE_SIZE, LANES), lambda b, pt: (b, 0, 0)),
            pl.BlockSpec((None, PAGE_SIZE * N_KV_HEADS, HEAD_DIM), lambda b, pt: (b, 0, 0)),
            pl.BlockSpec((None, PAGE_SIZE * N_KV_HEADS, HEAD_DIM), lambda b, pt: (b, 0, 0)),
            pl.BlockSpec(memory_space=pl.ANY),
            pl.BlockSpec(memory_space=pl.ANY),
            pl.BlockSpec(memory_space=pl.ANY),
        ],
        out_specs=pl.BlockSpec((t_new, D_MODEL), lambda b, pt: (b, 0)),
        scratch_shapes=[
            pltpu.VMEM((2, IDX_DIM, past), F32),
            pltpu.VMEM((n_pages // chunk_pages, ck * N_KV_HEADS, HEAD_DIM), F32),
            pltpu.VMEM((n_pages // chunk_pages, ck * N_KV_HEADS, HEAD_DIM), F32),
            pltpu.VMEM((t_new, n_keys), F32),
            pltpu.VMEM((t_new, n_keys), F32),
            pltpu.VMEM((t_new, 1), I32),
            pltpu.SemaphoreType.DMA((2,)),
            pltpu.SemaphoreType.DMA((2, n_pages // chunk_pages)),
        ],
    )
    return pl.pallas_call(
        kern,
        grid_spec=grid_spec,
        out_shape=jax.ShapeDtypeStruct((nseq * t_new, D_MODEL), F32),
        compiler_params=pltpu.CompilerParams(dimension_semantics=("arbitrary",), vmem_limit_bytes=VMEM_LIMIT),
        name="att_sample",
    )(page_table, q, iq, ikw, ikn_pad, kn_pad, vn_pad, cache_ik, cache_k, cache_v)


def _out_proj_kernel(o_ref, sg_ref, w_ref, x_ref, y_ref):
    a = (o_ref[...] * sg_ref[...].astype(F32)).astype(BF16)
    y_ref[...] = x_ref[...] + _dot(a, w_ref[...])


def _out_proj(o, sg, w_out, x, tm):
    n = o.shape[0]
    row = pl.BlockSpec((tm, D_MODEL), lambda i: (i, 0))
    return pl.pallas_call(
        _out_proj_kernel,
        grid=(n // tm,),
        in_specs=[row, row, pl.BlockSpec((D_MODEL, D_MODEL), lambda i: (0, 0)), row],
        out_specs=row,
        out_shape=jax.ShapeDtypeStruct((n, D_MODEL), F32),
        compiler_params=pltpu.CompilerParams(dimension_semantics=("parallel",), vmem_limit_bytes=VMEM_LIMIT),
        name="out_proj",
    )(o, sg, w_out, x)


def _dn_proj_kernel(x_ref, g_ref, w_ref, wba_ref, cw_ref, buf_ref, alog_ref, dtb_ref,
                    q_ref, k_ref, v_ref, sz_ref, gb_ref, cbuf_ref, xp_ref, *, n_seq, pad):
    tm = x_ref.shape[0]

    @pl.when(pl.program_id(1) == 0)
    def _():
        xp_ref[0:pad, :] = buf_ref[...]

    xn = (_rms(x_ref[...], D_MODEL) * g_ref[...]).astype(BF16)
    p = _dot(xn, w_ref[...])
    xp_ref[pad:pad + tm, :] = p[:, 0:CONV_DIM]
    conv = xp_ref[pad:pad + tm, :] * cw_ref[CONV_W - 1:CONV_W, :]
    for i in range(1, CONV_W):
        conv = conv + xp_ref[pad - i * n_seq:pad - i * n_seq + tm, :] * cw_ref[CONV_W - 1 - i:CONV_W - i, :]
    act = conv * jax.nn.sigmoid(conv)
    for h in range(DN_HEADS):
        sl = slice(h * DN_DK, (h + 1) * DN_DK)
        qh = act[:, sl]
        q_ref[:, sl] = qh * lax.rsqrt(jnp.sum(qh * qh, axis=-1, keepdims=True) + EPS) * (DN_DK ** -0.5)
        kh = act[:, DN_HEADS * DN_DK + h * DN_DK:DN_HEADS * DN_DK + (h + 1) * DN_DK]
        k_ref[:, sl] = kh * lax.rsqrt(jnp.sum(kh * kh, axis=-1, keepdims=True) + EPS)
    v_ref[...] = act[:, 2 * DN_HEADS * DN_DK:CONV_DIM]
    z = p[:, N_Z:N_BA]
    sz_ref[...] = (z * jax.nn.sigmoid(z)).astype(BF16)
    ba = _dot(xn, wba_ref[...])
    sp_in = ba + dtb_ref[...]
    softplus = jnp.maximum(sp_in, 0.0) + jnp.log1p(jnp.exp(-jnp.abs(sp_in)))
    lane = lax.broadcasted_iota(I32, (1, LANES), 1)
    gb_ref[...] = jnp.where(lane < DN_HEADS, jax.nn.sigmoid(ba), -jnp.exp(alog_ref[...]) * softplus)
    cbuf_ref[...] = xp_ref[tm:tm + pad, :]
    xp_ref[0:pad, :] = xp_ref[tm:tm + pad, :]


def _dn_proj(x, norm_g, w_main, w_ba, conv_w, buf, alog_pad, dtb_pad, n_groups, n_seq, tm):
    n = x.shape[0]
    pad = buf.shape[1]
    tiles = n // n_groups // tm
    assert n == n_groups * tiles * tm and pad % SUBLANES == 0 and pad >= (CONV_W - 1) * n_seq
    row = lambda s, i: (s * tiles + i, 0)
    const = lambda s, i: (0, 0)
    big = lambda d: pl.BlockSpec((tm, d), row)
    return pl.pallas_call(
        functools.partial(_dn_proj_kernel, n_seq=n_seq, pad=pad),
        grid=(n_groups, tiles),
        in_specs=[
            big(D_MODEL),
            pl.BlockSpec((1, D_MODEL), const),
            pl.BlockSpec((D_MODEL, N_BA), const),
            pl.BlockSpec((D_MODEL, LANES), const),
            pl.BlockSpec((CONV_W, CONV_DIM), const),
            pl.BlockSpec((None, pad, CONV_DIM), lambda s, i: (s, 0, 0)),
            pl.BlockSpec((1, LANES), const),
            pl.BlockSpec((1, LANES), const),
        ],
        out_specs=(big(D_MODEL), big(D_MODEL), big(D_MODEL), big(D_MODEL), big(LANES),
                   pl.BlockSpec((None, pad, CONV_DIM), lambda s, i: (s, 0, 0))),
        out_shape=(
            jax.ShapeDtypeStruct((n, D_MODEL), F32),
            jax.ShapeDtypeStruct((n, D_MODEL), F32),
            jax.ShapeDtypeStruct((n, D_MODEL), F32),
            jax.ShapeDtypeStruct((n, D_MODEL), BF16),
            jax.ShapeDtypeStruct((n, LANES), F32),
            jax.ShapeDtypeStruct((n_groups, pad, CONV_DIM), F32),
        ),
        scratch_shapes=[pltpu.VMEM((pad + tm, CONV_DIM), F32)],
        compiler_params=pltpu.CompilerParams(dimension_semantics=("parallel", "arbitrary"),
                                             vmem_limit_bytes=VMEM_LIMIT),
        name="dn_proj",
    )(x, norm_g, w_main, w_ba, conv_w, buf, alog_pad, dtb_pad)


def _dn_rec_kernel(q_ref, k_ref, v_ref, gb_ref, s0_ref, sz_ref, og_ref, w_ref, x_ref, y_ref, sout_ref,
                   s_scr, a_scr, *, c_len):
    n_sub = q_ref.shape[0] // c_len

    @pl.when(pl.program_id(1) == 0)
    def _():
        s_scr[...] = s0_ref[...]

    ri = lax.broadcasted_iota(I32, (c_len, c_len), 0)
    ci = lax.broadcasted_iota(I32, (c_len, c_len), 1)
    incl = ri >= ci
    strict = ri > ci
    tril = jnp.where(incl, 1.0, 0.0)
    eye_c = jnp.where(ri == ci, 1.0, 0.0)
    eye_k = jnp.where(lax.broadcasted_iota(I32, (DN_DK, DN_DK), 0) == lax.broadcasted_iota(I32, (DN_DK, DN_DK), 1),
                      1.0, 0.0)
    lane = lax.broadcasted_iota(I32, (c_len, LANES), 1)
    n_sq = max(0, (c_len - 1).bit_length() - 1)

    tile_aligned = c_len % LANES == 0

    subs = range(n_sub)
    rows = [slice(s * c_len, (s + 1) * c_len) for s in subs]
    gbc = [gb_ref[rows[s], :] for s in subs]
    gcum = [_dot(tril, gbc[s], HI) for s in subs]
    gcum_t = [gcum[s].T if tile_aligned else None for s in subs]
    units = [(s, h) for s in subs for h in range(DN_HEADS)]
    n_u = range(len(units))
    at = [(rows[s], slice(h * DN_DK, (h + 1) * DN_DK)) for s, h in units]
    beta = [gbc[s][:, h:h + 1] for s, h in units]
    gcol = [gcum[s][:, DN_HEADS + h:DN_HEADS + h + 1] for s, h in units]
    if tile_aligned:
        grow = [gcum_t[s][DN_HEADS + h:DN_HEADS + h + 1, :] for s, h in units]
    else:
        grow = [_dot_nt(jnp.where(lane == DN_HEADS + h, 1.0, 0.0), gcum[s], HI) for s, h in units]
    glast = [gcol[i][c_len - 1:c_len, :] for i in n_u]
    decay = [jnp.where(incl, jnp.exp(jnp.where(incl, gcol[i] - grow[i], 0.0)), 0.0) for i in n_u]
    eg = [jnp.exp(gcol[i]) for i in n_u]
    a_mat = [beta[i] * _dot_nt(k_ref[at[i]], k_ref[at[i]]) * jnp.where(strict, decay[i], 0.0) for i in n_u]
    pw = [-a_mat[i] for i in n_u]
    t_inv = [eye_c - a_mat[i] for i in n_u]
    for level in range(n_sq):
        if level < NEUMANN_SPLIT_LEVELS:
            ps = [_split_bf16(pw[i]) for i in n_u]
            pw = [_dot3(ps[i], ps[i]) for i in n_u]
            ps = [_split_bf16(pw[i]) for i in n_u]
            t_inv = [t_inv[i] + _dot3(_split_bf16(t_inv[i]), ps[i]) for i in n_u]
        else:
            pw = [_dot(pw[i].astype(BF16), pw[i].astype(BF16)) for i in n_u]
            t_inv = [t_inv[i] + _dot(t_inv[i].astype(BF16), pw[i].astype(BF16)) for i in n_u]
    wu = [_dot(t_inv[i].astype(BF16),
               jnp.concatenate([beta[i] * eg[i] * k_ref[at[i]], beta[i] * v_ref[at[i]]], axis=1).astype(BF16))
          for i in n_u]
    qk = [_dot_nt(q_ref[at[i]], k_ref[at[i]]) * decay[i] for i in n_u]
    kd = [k_ref[at[i]] * jnp.exp(glast[i] - gcol[i]) for i in n_u]
    kdt = [kd[i].T if tile_aligned else _dot_nt(eye_k, kd[i], HI) for i in n_u]
    for i, (s, h) in enumerate(units):
        s_old = s_scr[h]
        u = wu[i][:, DN_DK:DN_DK + DN_DV] - _dot(wu[i][:, 0:DN_DK], s_old)
        o = eg[i] * _dot(q_ref[at[i]], s_old) + _dot(qk[i], u)
        a_scr[at[i]] = _rms(o, DN_DV) * og_ref[...] * sz_ref[at[i]].astype(F32)
        s_scr[h] = jnp.exp(glast[i]) * s_old + _dot(kdt[i], u)
    y_ref[...] = x_ref[...] + _dot(a_scr[...].astype(BF16), w_ref[...])

    @pl.when(pl.program_id(1) == pl.num_programs(1) - 1)
    def _():
        sout_ref[...] = s_scr[...]


def _dn_rec(q, k, v, gb, s0, sz, o_gain, w_out, x, n_seq, t_len, chunk, chunks_per_step):
    step = chunk * chunks_per_step
    assert t_len % step == 0
    n_steps = t_len // step
    row = lambda b, c: (b * n_steps + c, 0)
    const = lambda b, c: (0, 0)
    wide = pl.BlockSpec((step, D_MODEL), row)
    st = pl.BlockSpec((None, None, DN_HEADS, DN_DK, DN_DV), lambda b, c: (b, 0, 0, 0, 0))
    return pl.pallas_call(
        functools.partial(_dn_rec_kernel, c_len=chunk),
        grid=(n_seq, n_steps),
        in_specs=[wide, wide, wide, pl.BlockSpec((step, LANES), row), st, wide,
                  pl.BlockSpec((1, DN_DV), const), pl.BlockSpec((D_MODEL, D_MODEL), const), wide],
        out_specs=(wide, st),
        out_shape=(jax.ShapeDtypeStruct((n_seq * t_len, D_MODEL), F32),
                   jax.ShapeDtypeStruct((n_seq, 1, DN_HEADS, DN_DK, DN_DV), F32)),
        scratch_shapes=[pltpu.VMEM((DN_HEADS, DN_DK, DN_DV), F32), pltpu.VMEM((step, D_MODEL), F32)],
        compiler_params=pltpu.CompilerParams(dimension_semantics=("parallel", "arbitrary"),
                                             vmem_limit_bytes=VMEM_LIMIT),
        name="dn_rec",
    )(q, k, v, gb, s0, sz, o_gain, w_out, x)


def _pad_lanes(v, offset=0):
    return jnp.zeros((1, LANES), F32).at[0, offset:offset + v.shape[0]].set(v)


def _split_w(w, main, tail):
    small = jnp.pad(w[:, main:tail], ((0, 0), (0, LANES - (tail - main))))
    return w[:, :main].astype(BF16), small.astype(BF16), w[:, tail:].astype(BF16)


def kernel(x_prompt, x_sample, cache_k, cache_v, cache_idx_k, state_dn_S, state_dn_conv, page_table,
           att_norm, att_w_in, att_q_gain, att_k_gain, att_ik_gain, att_w_out,
           dn_norm, dn_w_in, dn_conv_w, dn_A_log, dn_dt_bias, dn_o_gain, dn_w_out):
    bp, seq, d = x_prompt.shape
    bs, t_new, _ = x_sample.shape
    n_pool = cache_k.shape[0]
    np_rows, ns_rows = bp * seq, bs * t_new
    tm = 256

    xp = x_prompt.reshape(np_rows, d)
    xs = x_sample.reshape(ns_rows, d)

    w_att = _split_w(att_w_in[0], A_IKW, A_IKW + IDX_DIM + IDX_HEADS)
    a_norm = att_norm[0][None, :]
    qg, kg = att_q_gain[0][None, :], att_k_gain[0][None, :]
    ikg = _pad_lanes(att_ik_gain[0])
    w_ao = att_w_out[0].astype(BF16)

    q_p, k_p, kb_p, v_p, vt_p, iq_p, ikw_p, ikb_p, iwt_p, sg_p = _att_proj(xp, a_norm, *w_att, qg, kg, ikg, tm)
    y1_p = _att_prompt(q_p, kb_p, vt_p, ikb_p, iq_p, iwt_p, sg_p, w_ao, xp, bp, seq)

    q_s, k_s, _, v_s, _, iq_s, ikw_s, ikb_s, _, sg_s = _att_proj(xs, a_norm, *w_att, qg, kg, ikg, min(tm, ns_rows))
    pad_new = lambda a, per_tok: jnp.pad(a.reshape(bs, t_new * per_tok, a.shape[-1]),
                                         ((0, 0), (0, (PAGE_SIZE - t_new) * per_tok), (0, 0)))
    o_s = _att_sample(page_table, q_s.astype(F32), iq_s.astype(F32), ikw_s, pad_new(ikb_s, 1),
                      pad_new(k_s, N_KV_HEADS), pad_new(v_s, N_KV_HEADS),
                      jnp.transpose(cache_idx_k, (0, 2, 3, 1)).reshape(n_pool, IDX_DIM, PAGE_SIZE),
                      cache_k.reshape(n_pool, PAGE_SIZE * N_KV_HEADS, HEAD_DIM),
                      cache_v.reshape(n_pool, PAGE_SIZE * N_KV_HEADS, HEAD_DIM), t_new)
    y1_s = _out_proj(o_s, sg_s, w_ao, xs, min(tm, ns_rows))

    w_dn = _split_w(dn_w_in[0], N_BA, N_BA + 2 * DN_HEADS)[:2]
    d_norm = dn_norm[0][None, :]
    alog = _pad_lanes(dn_A_log[0], DN_HEADS)
    dtb = _pad_lanes(dn_dt_bias[0], DN_HEADS)
    o_gain = dn_o_gain[0][None, :]
    w_do = dn_w_out[0].astype(BF16)
    hist = CONV_W - 1

    buf_p = jnp.zeros((bp, SUBLANES, CONV_DIM), F32)
    dq_p, dk_p, dv_p, sz_p, gb_p, cb_p = _dn_proj(y1_p, d_norm, *w_dn, dn_conv_w[0], buf_p, alog, dtb,
                                                   n_groups=bp, n_seq=1, tm=tm)
    s0_p = jnp.zeros((bp, 1, DN_HEADS, DN_DK, DN_DV), F32)
    y2_p, s_p = _dn_rec(dq_p, dk_p, dv_p, gb_p, s0_p, sz_p, o_gain, w_do, y1_p, bp, seq, chunk=LANES,
                        chunks_per_step=4)

    to_tm = lambda a: a.reshape(bs, t_new, -1).transpose(1, 0, 2).reshape(ns_rows, -1)
    to_bm = lambda a: a.reshape(t_new, bs, -1).transpose(1, 0, 2).reshape(ns_rows, -1)
    buf_s = state_dn_conv[:, 0].transpose(1, 0, 2).reshape(1, hist * bs, CONV_DIM)
    dq_s, dk_s, dv_s, sz_s, gb_s, cb_s = _dn_proj(to_tm(y1_s), d_norm, *w_dn, dn_conv_w[0], buf_s, alog, dtb,
                                                   n_groups=1, n_seq=bs, tm=ns_rows)
    y2_s, s_s = _dn_rec(to_bm(dq_s), to_bm(dk_s), to_bm(dv_s), to_bm(gb_s), state_dn_S,
                        to_bm(sz_s.astype(F32)), o_gain, w_do, y1_s, bs, t_new, chunk=t_new,
                        chunks_per_step=1)

    return (
        y2_p.reshape(bp, seq, d),
        y2_s.reshape(bs, t_new, d),
        k_p.reshape(bp, seq, 1, N_KV_HEADS, HEAD_DIM),
        v_p.reshape(bp, seq, 1, N_KV_HEADS, HEAD_DIM),
        ikw_p[:, :IDX_DIM].reshape(bp, seq, 1, IDX_DIM),
        k_s.reshape(bs, t_new, 1, N_KV_HEADS, HEAD_DIM),
        v_s.reshape(bs, t_new, 1, N_KV_HEADS, HEAD_DIM),
        ikw_s[:, :IDX_DIM].reshape(bs, t_new, 1, IDX_DIM),
        s_p,
        cb_p[:, SUBLANES - hist:, :].reshape(bp, 1, hist, CONV_DIM),
        s_s,
        cb_s.reshape(hist, bs, CONV_DIM).transpose(1, 0, 2).reshape(bs, 1, hist, CONV_DIM),
    )
```

```python
import functools

import jax
import jax.numpy as jnp
from jax import lax
from jax.experimental import pallas as pl
from jax.experimental.pallas import tpu as pltpu

F32 = jnp.float32
BF16 = jnp.bfloat16
I32 = jnp.int32

EPS = 1e-6
LANES = 128
SUBLANES = 8
VMEM_LIMIT = 56 * 1024 * 1024

D_MODEL = 1024
N_HEADS = 8
HEAD_DIM = 128
N_KV_HEADS = 2
GROUP = N_HEADS // N_KV_HEADS
IDX_HEADS = 8
IDX_DIM = 64
TOPK_MAX = 256
PAGE_SIZE = 128
DN_HEADS = 8
DN_DK = 128
DN_DV = 128
CONV_W = 4
KV_W = N_KV_HEADS * HEAD_DIM
CONV_DIM = 3 * DN_HEADS * DN_DK

A_Q, A_K, A_V, A_IQ, A_IKW = 0, 1024, 1280, 1536, 2048
N_Z, N_BA, N_END = 3072, 4096, 4224

INT_MIN = -2147483648
INT_MAX = 2147483647
NEG_BIG = -1e30
NT_DIMS = (((1,), (1,)), ((), ()))
HI = lax.Precision.HIGHEST
NEUMANN_SPLIT_LEVELS = 4


def _dot(a, b, precision=None):
    return jnp.dot(a, b, preferred_element_type=F32, precision=precision)


def _dot_nt(a, b, precision=None):
    return lax.dot_general(a, b, NT_DIMS, preferred_element_type=F32, precision=precision)


def _dot_tn(a, b):
    return lax.dot_general(a, b, (((0,), (0,)), ((), ())), preferred_element_type=F32)


def _split_bf16(x):
    hi = x.astype(BF16)
    return hi, (x - hi.astype(F32)).astype(BF16)


def _dot3(a, b):
    (ah, al), (bh, bl) = a, b
    return _dot(ah, bh) + (_dot(ah, bl) + _dot(al, bh))


def _rms(x, n):
    return x * lax.rsqrt(jnp.sum(x * x, axis=-1, keepdims=True) * (1.0 / n) + EPS)


def _ordinal_to_float(u):
    key = u ^ jnp.int32(INT_MIN)
    return pltpu.bitcast(jnp.where(key < 0, key ^ jnp.int32(INT_MAX), key), F32)


def _att_proj_kernel(x_ref, g_ref, w_ref, wikw_ref, wg_ref, qg_ref, kg_ref, ikg_ref,
                     q_ref, k_ref, kb_ref, v_ref, vt_ref, iq_ref, ikw_ref, ikb_ref, iwt_ref, sg_ref):
    tm = x_ref.shape[0]
    x = x_ref[...]
    xn = (_rms(x, D_MODEL) * g_ref[...]).astype(BF16)
    p = _dot(xn, w_ref[...])
    for h in range(N_HEADS):
        qh = _rms(p[:, A_Q + h * HEAD_DIM:A_Q + (h + 1) * HEAD_DIM], HEAD_DIM) * qg_ref[...]
        q_ref[:, h * HEAD_DIM:(h + 1) * HEAD_DIM] = (qh * (HEAD_DIM ** -0.5)).astype(BF16)
    v = p[:, A_V:A_V + KV_W]
    for g in range(N_KV_HEADS):
        kh = _rms(p[:, A_K + g * HEAD_DIM:A_K + (g + 1) * HEAD_DIM], HEAD_DIM) * kg_ref[...]
        k_ref[pl.ds(g, tm, stride=N_KV_HEADS), :] = kh
        kb_ref[:, g * HEAD_DIM:(g + 1) * HEAD_DIM] = kh.astype(BF16)
        v_ref[pl.ds(g, tm, stride=N_KV_HEADS), :] = v[:, g * HEAD_DIM:(g + 1) * HEAD_DIM]
    vt_ref[...] = v.T.astype(BF16)
    lane = lax.broadcasted_iota(I32, (1, LANES), 1)
    is_ik = lane < IDX_DIM
    for h in range(IDX_HEADS):
        iqh = p[:, A_IQ + (h // 2) * LANES:A_IQ + (h // 2 + 1) * LANES]
        if h % 2:
            iqh = pltpu.roll(iqh, IDX_DIM, axis=1)
        iq_ref[h] = jnp.where(is_ik, iqh, 0.0).astype(BF16)
    ikw = _dot(xn, wikw_ref[...])
    ik = jnp.where(is_ik, ikw, 0.0)
    ikn = _rms(ik, IDX_DIM) * ikg_ref[...]
    out = jnp.where(is_ik, ikn, ikw * (IDX_HEADS ** -0.5 * IDX_DIM ** -0.5))
    ikw_ref[...] = out
    ikb_ref[...] = ikn.astype(BF16)
    iwt_ref[...] = out.T[IDX_DIM:IDX_DIM + IDX_HEADS, :]
    gate = _dot(xn, wg_ref[...])
    sg_ref[...] = (gate * jax.nn.sigmoid(gate)).astype(BF16)


def _att_proj(x, norm_g, w_main, w_ikw, w_gate, q_gain, k_gain, ik_gain_pad, tm):
    n = x.shape[0]
    assert n % tm == 0 and tm % LANES == 0
    row = lambda i: (i, 0)
    const = lambda i: (0, 0)
    out_shape = (
        jax.ShapeDtypeStruct((n, D_MODEL), BF16),
        jax.ShapeDtypeStruct((N_KV_HEADS * n, HEAD_DIM), F32),
        jax.ShapeDtypeStruct((n, KV_W), BF16),
        jax.ShapeDtypeStruct((N_KV_HEADS * n, HEAD_DIM), F32),
        jax.ShapeDtypeStruct((KV_W, n), BF16),
        jax.ShapeDtypeStruct((IDX_HEADS, n, LANES), BF16),
        jax.ShapeDtypeStruct((n, LANES), F32),
        jax.ShapeDtypeStruct((n, LANES), BF16),
        jax.ShapeDtypeStruct((IDX_HEADS, n), F32),
        jax.ShapeDtypeStruct((n, D_MODEL), BF16),
    )
    out_specs = (
        pl.BlockSpec((tm, D_MODEL), row),
        pl.BlockSpec((N_KV_HEADS * tm, HEAD_DIM), row),
        pl.BlockSpec((tm, KV_W), row),
        pl.BlockSpec((N_KV_HEADS * tm, HEAD_DIM), row),
        pl.BlockSpec((KV_W, tm), lambda i: (0, i)),
        pl.BlockSpec((IDX_HEADS, tm, LANES), lambda i: (0, i, 0)),
        pl.BlockSpec((tm, LANES), row),
        pl.BlockSpec((tm, LANES), row),
        pl.BlockSpec((IDX_HEADS, tm), lambda i: (0, i)),
        pl.BlockSpec((tm, D_MODEL), row),
    )
    return pl.pallas_call(
        _att_proj_kernel,
        grid=(n // tm,),
        in_specs=[
            pl.BlockSpec((tm, D_MODEL), row),
            pl.BlockSpec((1, D_MODEL), const),
            pl.BlockSpec((D_MODEL, A_IKW), const),
            pl.BlockSpec((D_MODEL, LANES), const),
            pl.BlockSpec((D_MODEL, D_MODEL), const),
            pl.BlockSpec((1, HEAD_DIM), const),
            pl.BlockSpec((1, HEAD_DIM), const),
            pl.BlockSpec((1, LANES), const),
        ],
        out_specs=out_specs,
        out_shape=out_shape,
        compiler_params=pltpu.CompilerParams(dimension_semantics=("parallel",), vmem_limit_bytes=VMEM_LIMIT),
        name="att_proj",
    )(x, norm_g, w_main, w_ikw, w_gate, q_gain, k_gain, ik_gain_pad)


def _reduce(x, axis, combine, finish):
    n = x.shape[axis]
    unit = 8 * (SUBLANES if axis == 0 else LANES)
    full = n // unit
    if full < 2:
        return finish(x, axis=axis, keepdims=True)
    cut = (lambda i, j: x[i:j, :]) if axis == 0 else (lambda i, j: x[:, i:j])
    acc = cut(0, unit)
    for i in range(1, full):
        acc = combine(acc, cut(i * unit, (i + 1) * unit))
    out = finish(acc, axis=axis, keepdims=True)
    if n % unit:
        out = combine(out, finish(cut(full * unit, n), axis=axis, keepdims=True))
    return out


def _count(mask, axis):
    return _reduce(jnp.where(mask, 1.0, 0.0), axis, jnp.add, jnp.sum)


def _select_bias(sc_ref, lo_ref, n_keys, key_idx, topk, axis):
    sl = (slice(0, n_keys), slice(None)) if axis == 0 else (slice(None), slice(0, n_keys))
    vec = (1, sc_ref.shape[1]) if axis == 0 else (sc_ref.shape[0], 1)
    kf = float(topk)

    def bit_step(i, prefix):
        cand = prefix | (jnp.int32(1) << (31 - i))
        cnt = _count(sc_ref[sl] >= _ordinal_to_float(cand), axis)
        return jnp.where(cnt >= kf, cand, prefix)

    prefix = lax.fori_loop(0, 32, bit_step, jnp.zeros(vec, I32))
    thr = _ordinal_to_float(prefix)
    sc = sc_ref[sl]
    few = jnp.logical_not(_count(sc >= thr, axis) >= kf)
    thr = jnp.where(few, -jnp.inf, thr)
    need = kf - _count(sc > thr, axis)
    excess = (_count(sc >= thr, axis) > kf) & jnp.logical_not(few)
    lo_ref[...] = jnp.full(vec, INT_MAX, I32)

    @pl.when(jnp.max(jnp.where(excess, 1.0, 0.0)) > 0.0)
    def _():
        nbits = max(1, (n_keys - 1).bit_length())

        def idx_step(i, lo):
            cand = lo + (jnp.int32(1) << (nbits - 1 - i))
            cnt = _count((sc_ref[sl] == thr) & (key_idx < cand), axis)
            return jnp.where(cnt < need, cand, lo)

        lo_ref[...] = lax.fori_loop(0, nbits, idx_step, jnp.zeros(vec, I32))

    lo = jnp.where(few, -1, lo_ref[...])
    sel = (sc > thr) | ((sc == thr) & (key_idx <= lo))
    return jnp.where(sel, 0.0, NEG_BIG)


def _att_prompt_body(n_keys, topk, q_ref, kb_ref, vt_ref, ikb_ref, iq_ref, iwt_ref, o_ref,
                     sc_ref, bias_ref, lo_ref):
    qb = q_ref.shape[0]
    t0 = pl.program_id(0) * qb
    ik = ikb_ref[0:n_keys, :]
    iwt = iwt_ref[...]
    score = jnp.zeros((n_keys, qb), F32)
    for h in range(IDX_HEADS):
        s = _dot_nt(ik, iq_ref[h])
        score = score + jnp.maximum(s, 0.0) * iwt[h:h + 1, :]
    key_idx = lax.broadcasted_iota(I32, (n_keys, qb), 0)
    q_pos = t0 + lax.broadcasted_iota(I32, (n_keys, qb), 1)
    sc_ref[0:n_keys, :] = jnp.where(key_idx <= q_pos, score, -jnp.inf)
    bias_ref[0:n_keys, :] = _select_bias(sc_ref, lo_ref, n_keys, key_idx, topk, axis=0)
    heads = range(N_HEADS)
    hd = [slice(h * HEAD_DIM, (h + 1) * HEAD_DIM) for h in heads]
    gd = [slice((h // GROUP) * HEAD_DIM, (h // GROUP + 1) * HEAD_DIM) for h in heads]
    s = [_dot_nt(kb_ref[0:n_keys, gd[h]], q_ref[:, hd[h]]) + bias_ref[0:n_keys, :] for h in heads]
    m = [_reduce(s[h], 0, jnp.maximum, jnp.max) for h in heads]
    p = [jnp.exp(s[h] - m[h]) for h in heads]
    l = [_reduce(p[h], 0, jnp.add, jnp.sum) for h in heads]
    ot = [_dot(vt_ref[gd[h], 0:n_keys], p[h].astype(BF16)) for h in heads]
    for h in heads:
        o_ref[:, hd[h]] = (ot[h] / l[h]).T


def _att_prompt_kernel(q_ref, kb_ref, vt_ref, ikb_ref, iq_ref, iwt_ref, sg_ref, w_ref, x_ref, y_ref,
                       o_scr, sc_ref, bias_ref, lo_ref, *, cls_len, n_cls, topk):
    qb = q_ref.shape[0]
    cls = (pl.program_id(0) * qb) // cls_len
    for c in range(n_cls):
        pl.when(cls == c)(functools.partial(
            _att_prompt_body, cls_len * (c + 1), topk, q_ref, kb_ref, vt_ref, ikb_ref, iq_ref, iwt_ref,
            o_scr, sc_ref, bias_ref, lo_ref))
    a = (o_scr[...] * sg_ref[...].astype(F32)).astype(BF16)
    y_ref[...] = x_ref[...] + _dot(a, w_ref[...])


def _att_prompt(q, kb, vt, ikb, iq, iwt, sg, w_out, x, batch, seq, qb=LANES):
    nb = seq // qb
    n_cls = min(8, nb)
    assert seq % (n_cls * qb) == 0
    topk = min(TOPK_MAX, seq // 4)
    kern = functools.partial(_att_prompt_kernel, cls_len=seq // n_cls, n_cls=n_cls, topk=topk)
    return pl.pallas_call(
        kern,
        grid=(nb, batch),
        in_specs=[
            pl.BlockSpec((qb, D_MODEL), lambda j, b: (b * nb + j, 0)),
            pl.BlockSpec((seq, KV_W), lambda j, b: (b, 0)),
            pl.BlockSpec((KV_W, seq), lambda j, b: (0, b)),
            pl.BlockSpec((seq, LANES), lambda j, b: (b, 0)),
            pl.BlockSpec((IDX_HEADS, qb, LANES), lambda j, b: (0, b * nb + j, 0)),
            pl.BlockSpec((IDX_HEADS, qb), lambda j, b: (0, b * nb + j)),
            pl.BlockSpec((qb, D_MODEL), lambda j, b: (b * nb + j, 0)),
            pl.BlockSpec((D_MODEL, D_MODEL), lambda j, b: (0, 0)),
            pl.BlockSpec((qb, D_MODEL), lambda j, b: (b * nb + j, 0)),
        ],
        out_specs=pl.BlockSpec((qb, D_MODEL), lambda j, b: (b * nb + j, 0)),
        out_shape=jax.ShapeDtypeStruct((batch * seq, D_MODEL), F32),
        scratch_shapes=[
            pltpu.VMEM((qb, D_MODEL), F32),
            pltpu.VMEM((seq, qb), F32),
            pltpu.VMEM((seq, qb), F32),
            pltpu.VMEM((1, qb), I32),
        ],
        compiler_params=pltpu.CompilerParams(dimension_semantics=("parallel", "arbitrary"),
                                             vmem_limit_bytes=VMEM_LIMIT),
        name="att_prompt",
    )(q, kb, vt, ikb, iq, iwt, sg, w_out, x)


def _att_sample_kernel(pt_ref, q_ref, iq_ref, ikw_ref, ikn_ref, kn_ref, vn_ref,
                       cik_hbm, ck_hbm, cv_hbm, o_ref,
                       ikbuf, kbuf, vbuf, sc_ref, bias_ref, lo_ref, sem_i, sem_kv,
                       *, n_pages, chunk_pages, topk):
    b = pl.program_id(0)
    t_new = q_ref.shape[0]
    past = n_pages * PAGE_SIZE
    n_keys = past + PAGE_SIZE
    n_chunks = n_pages // chunk_pages
    ck = chunk_pages * PAGE_SIZE

    n_seq = pl.num_programs(0)
    islot = b % 2
    ahead = n_chunks - 1

    def idx_copy(seq, slot, p):
        return pltpu.make_async_copy(cik_hbm.at[pt_ref[seq, p]],
                                     ikbuf.at[slot, :, pl.ds(p * PAGE_SIZE, PAGE_SIZE)], sem_i.at[slot])

    def kv_copies(seq, c, p):
        page = pt_ref[seq, c * chunk_pages + p]
        dst = pl.ds(p * PAGE_SIZE * N_KV_HEADS, PAGE_SIZE * N_KV_HEADS)
        return (pltpu.make_async_copy(ck_hbm.at[page], kbuf.at[c, dst], sem_kv.at[0, c]),
                pltpu.make_async_copy(cv_hbm.at[page], vbuf.at[c, dst], sem_kv.at[1, c]))

    def start_idx(seq, slot):
        def go(p, carry):
            idx_copy(seq, slot, p).start()
            return carry
        lax.fori_loop(0, n_pages, go, 0, unroll=8)

    def wait_idx(seq, slot):
        def go(p, carry):
            idx_copy(seq, slot, p).wait()
            return carry
        lax.fori_loop(0, n_pages, go, 0, unroll=8)

    def start_chunk(seq, c):
        def go(p, carry):
            for cp in kv_copies(seq, c, p):
                cp.start()
            return carry
        lax.fori_loop(0, chunk_pages, go, 0, unroll=8)

    def wait_chunk(seq, c):
        def go(p, carry):
            for cp in kv_copies(seq, c, p):
                cp.wait()
            return carry
        lax.fori_loop(0, chunk_pages, go, 0, unroll=8)

    @pl.when(b == 0)
    def _():
        start_idx(b, islot)
        for c in range(ahead):
            start_chunk(b, c)

    @pl.when(b + 1 < n_seq)
    def _():
        start_idx(b + 1, 1 - islot)

    wait_idx(b, islot)

    iq2 = iq_ref[...].reshape(IDX_HEADS * t_new, LANES).astype(BF16)
    ikw = ikw_ref[...]

    def scores(ik_rows):
        s = _dot_nt(iq2, ik_rows)
        acc = jnp.zeros((t_new, ik_rows.shape[0]), F32)
        for h in range(IDX_HEADS):
            acc = acc + jnp.maximum(s[h * t_new:(h + 1) * t_new, :], 0.0) * ikw[:, IDX_DIM + h:IDX_DIM + h + 1]
        return acc

    iq2_past = iq2[:, 0:IDX_DIM]
    for c in range(n_chunks):
        s = _dot(iq2_past, ikbuf[islot, :, c * ck:(c + 1) * ck].astype(BF16))
        acc = jnp.zeros((t_new, ck), F32)
        for h in range(IDX_HEADS):
            acc = acc + jnp.maximum(s[h * t_new:(h + 1) * t_new, :], 0.0) * ikw[:, IDX_DIM + h:IDX_DIM + h + 1]
        sc_ref[:, c * ck:(c + 1) * ck] = acc
    new_idx = lax.broadcasted_iota(I32, (t_new, PAGE_SIZE), 1)
    tok = lax.broadcasted_iota(I32, (t_new, PAGE_SIZE), 0)
    sc_ref[:, past:n_keys] = jnp.where(new_idx <= tok, scores(ikn_ref[...]), -jnp.inf)
    key_idx = lax.broadcasted_iota(I32, (t_new, n_keys), 1)
    bias_ref[...] = _select_bias(sc_ref, lo_ref, n_keys, key_idx, topk, axis=1)

    rows = GROUP * t_new
    qs = [jnp.concatenate([q_ref[:, (g * GROUP + r) * HEAD_DIM:(g * GROUP + r + 1) * HEAD_DIM]
                           for r in range(GROUP)], axis=0).astype(BF16) for g in range(N_KV_HEADS)]
    m = [jnp.full((rows, 1), NEG_BIG, F32) for _ in range(N_KV_HEADS)]
    l = [jnp.zeros((rows, 1), F32) for _ in range(N_KV_HEADS)]
    acc = [jnp.zeros((rows, HEAD_DIM), F32) for _ in range(N_KV_HEADS)]

    def attend(g, k_rows, v_rows, bias):
        s = _dot_nt(qs[g], k_rows) + jnp.concatenate([bias] * GROUP, axis=0)
        m_new = jnp.maximum(m[g], _reduce(s, 1, jnp.maximum, jnp.max))
        alpha = jnp.exp(m[g] - m_new)
        p = jnp.exp(s - m_new)
        l[g] = alpha * l[g] + _reduce(p, 1, jnp.add, jnp.sum)
        acc[g] = alpha * acc[g] + _dot(p.astype(BF16), v_rows)
        m[g] = m_new

    for g in range(N_KV_HEADS):
        new_rows = pl.ds(g, PAGE_SIZE, stride=N_KV_HEADS)
        attend(g, kn_ref[new_rows, :].astype(BF16), vn_ref[new_rows, :].astype(BF16), bias_ref[:, past:n_keys])
    for c in range(n_chunks):
        nxt = c + ahead
        if nxt < n_chunks:
            start_chunk(b, nxt)
        else:
            pl.when(b + 1 < n_seq)(functools.partial(start_chunk, b + 1, nxt - n_chunks))
        wait_chunk(b, c)
        for g in range(N_KV_HEADS):
            head_rows = pl.ds(g, ck, stride=N_KV_HEADS)
            attend(g, kbuf[c, head_rows, :].astype(BF16), vbuf[c, head_rows, :].astype(BF16),
                   bias_ref[:, c * ck:(c + 1) * ck])
    for g in range(N_KV_HEADS):
        og = acc[g] / l[g]
        for r in range(GROUP):
            h = g * GROUP + r
            o_ref[:, h * HEAD_DIM:(h + 1) * HEAD_DIM] = og[r * t_new:(r + 1) * t_new, :]


def _att_sample(page_table, q, iq, ikw, ikn_pad, kn_pad, vn_pad, cache_ik, cache_k, cache_v, t_new):
    nseq, n_pages = page_table.shape
    assert t_new == SUBLANES
    chunk_pages = min(32, n_pages // 2)
    assert n_pages % chunk_pages == 0
    past = n_pages * PAGE_SIZE
    n_keys = past + PAGE_SIZE
    topk = min(TOPK_MAX, (past + t_new) // 4)
    ck = chunk_pages * PAGE_SIZE
    kern = functools.partial(_att_sample_kernel, n_pages=n_pages, chunk_pages=chunk_pages, topk=topk)
    grid_spec = pltpu.PrefetchScalarGridSpec(
        num_scalar_prefetch=1,
        grid=(nseq,),
        in_specs=[
            pl.BlockSpec((t_new, D_MODEL), lambda b, pt: (b, 0)),
            pl.BlockSpec((IDX_HEADS, t_new, LANES), lambda b, pt: (0, b, 0)),
            pl.BlockSpec((t_new, LANES), lambda b, pt: (b, 0)),
            pl.BlockSpec((None, PAGE_SIZE, LANES), lambda b, pt: (b, 0, 0)),
            pl.BlockSpec((None, PAGE_SIZE * N_KV_HEADS, HEAD_DIM), lambda b, pt: (b, 0, 0)),
            pl.BlockSpec((None, PAGE_SIZE * N_KV_HEADS, HEAD_DIM), lambda b, pt: (b, 0, 0)),
            pl.BlockSpec(memory_space=pl.ANY),
            pl.BlockSpec(memory_space=pl.ANY),
            pl.BlockSpec(memory_space=pl.ANY),
        ],
        out_specs=pl.BlockSpec((t_new, D_MODEL), lambda b, pt: (b, 0)),
        scratch_shapes=[
            pltpu.VMEM((2, IDX_DIM, past), F32),
            pltpu.VMEM((n_pages // chunk_pages, ck * N_KV_HEADS, HEAD_DIM), F32),
            pltpu.VMEM((n_pages // chunk_pages, ck * N_KV_HEADS, HEAD_DIM), F32),
            pltpu.VMEM((t_new, n_keys), F32),
            pltpu.VMEM((t_new, n_keys), F32),
            pltpu.VMEM((t_new, 1), I32),
            pltpu.SemaphoreType.DMA((2,)),
            pltpu.SemaphoreType.DMA((2, n_pages // chunk_pages)),
        ],
    )
    return pl.pallas_call(
        kern,
        grid_spec=grid_spec,
        out_shape=jax.ShapeDtypeStruct((nseq * t_new, D_MODEL), F32),
        compiler_params=pltpu.CompilerParams(dimension_semantics=("arbitrary",), vmem_limit_bytes=VMEM_LIMIT),
        name="att_sample",
    )(page_table, q, iq, ikw, ikn_pad, kn_pad, vn_pad, cache_ik, cache_k, cache_v)


def _out_proj_kernel(o_ref, sg_ref, w_ref, x_ref, y_ref):
    a = (o_ref[...] * sg_ref[...].astype(F32)).astype(BF16)
    y_ref[...] = x_ref[...] + _dot(a, w_ref[...])


def _out_proj(o, sg, w_out, x, tm):
    n = o.shape[0]
    row = pl.BlockSpec((tm, D_MODEL), lambda i: (i, 0))
    return pl.pallas_call(
        _out_proj_kernel,
        grid=(n // tm,),
        in_specs=[row, row, pl.BlockSpec((D_MODEL, D_MODEL), lambda i: (0, 0)), row],
        out_specs=row,
        out_shape=jax.ShapeDtypeStruct((n, D_MODEL), F32),
        compiler_params=pltpu.CompilerParams(dimension_semantics=("parallel",), vmem_limit_bytes=VMEM_LIMIT),
        name="out_proj",
    )(o, sg, w_out, x)


def _dn_proj_kernel(x_ref, g_ref, w_ref, cw_ref, buf_ref, alog_ref, dtb_ref,
                    q_ref, k_ref, v_ref, sz_ref, gb_ref, cbuf_ref, xp_ref, *, n_seq, pad):
    tm = x_ref.shape[0]

    @pl.when(pl.program_id(1) == 0)
    def _():
        xp_ref[0:pad, :] = buf_ref[...]

    xn = (_rms(x_ref[...], D_MODEL) * g_ref[...]).astype(BF16)
    p = _dot(xn, w_ref[...])
    xp_ref[pad:pad + tm, :] = p[:, 0:CONV_DIM]
    conv = xp_ref[pad:pad + tm, :] * cw_ref[CONV_W - 1:CONV_W, :]
    for i in range(1, CONV_W):
        conv = conv + xp_ref[pad - i * n_seq:pad - i * n_seq + tm, :] * cw_ref[CONV_W - 1 - i:CONV_W - i, :]
    act = conv * jax.nn.sigmoid(conv)
    for h in range(DN_HEADS):
        sl = slice(h * DN_DK, (h + 1) * DN_DK)
        qh = act[:, sl]
        q_ref[:, sl] = qh * lax.rsqrt(jnp.sum(qh * qh, axis=-1, keepdims=True) + EPS) * (DN_DK ** -0.5)
        kh = act[:, DN_HEADS * DN_DK + h * DN_DK:DN_HEADS * DN_DK + (h + 1) * DN_DK]
        k_ref[:, sl] = kh * lax.rsqrt(jnp.sum(kh * kh, axis=-1, keepdims=True) + EPS)
    v_ref[...] = act[:, 2 * DN_HEADS * DN_DK:CONV_DIM]
    z = p[:, N_Z:N_BA]
    sz_ref[...] = (z * jax.nn.sigmoid(z)).astype(BF16)
    ba = p[:, N_BA:N_END]
    sp_in = ba + dtb_ref[...]
    softplus = jnp.maximum(sp_in, 0.0) + jnp.log1p(jnp.exp(-jnp.abs(sp_in)))
    lane = lax.broadcasted_iota(I32, (1, LANES), 1)
    gb_ref[...] = jnp.where(lane < DN_HEADS, jax.nn.sigmoid(ba), -jnp.exp(alog_ref[...]) * softplus)
    cbuf_ref[...] = xp_ref[tm:tm + pad, :]
    xp_ref[0:pad, :] = xp_ref[tm:tm + pad, :]


def _dn_proj(x, norm_g, w_packed, conv_w, buf, alog_pad, dtb_pad, n_groups, n_seq, tm):
    n = x.shape[0]
    pad = buf.shape[1]
    tiles = n // n_groups // tm
    assert n == n_groups * tiles * tm and pad % SUBLANES == 0 and pad >= (CONV_W - 1) * n_seq
    row = lambda s, i: (s * tiles + i, 0)
    const = lambda s, i: (0, 0)
    big = lambda d: pl.BlockSpec((tm, d), row)
    return pl.pallas_call(
        functools.partial(_dn_proj_kernel, n_seq=n_seq, pad=pad),
        grid=(n_groups, tiles),
        in_specs=[
            big(D_MODEL),
            pl.BlockSpec((1, D_MODEL), const),
            pl.BlockSpec((D_MODEL, N_END), const),
            pl.BlockSpec((CONV_W, CONV_DIM), const),
            pl.BlockSpec((None, pad, CONV_DIM), lambda s, i: (s, 0, 0)),
            pl.BlockSpec((1, LANES), const),
            pl.BlockSpec((1, LANES), const),
        ],
        out_specs=(big(D_MODEL), big(D_MODEL), big(D_MODEL), big(D_MODEL), big(LANES),
                   pl.BlockSpec((None, pad, CONV_DIM), lambda s, i: (s, 0, 0))),
        out_shape=(
            jax.ShapeDtypeStruct((n, D_MODEL), F32),
            jax.ShapeDtypeStruct((n, D_MODEL), F32),
            jax.ShapeDtypeStruct((n, D_MODEL), F32),
            jax.ShapeDtypeStruct((n, D_MODEL), BF16),
            jax.ShapeDtypeStruct((n, LANES), F32),
            jax.ShapeDtypeStruct((n_groups, pad, CONV_DIM), F32),
        ),
        scratch_shapes=[pltpu.VMEM((pad + tm, CONV_DIM), F32)],
        compiler_params=pltpu.CompilerParams(dimension_semantics=("parallel", "arbitrary"),
                                             vmem_limit_bytes=VMEM_LIMIT),
        name="dn_proj",
    )(x, norm_g, w_packed, conv_w, buf, alog_pad, dtb_pad)


def _dn_rec_kernel(q_ref, k_ref, v_ref, gb_ref, s0_ref, sz_ref, og_ref, w_ref, x_ref, y_ref, sout_ref,
                   s_scr, a_scr, *, c_len):
    n_sub = q_ref.shape[0] // c_len

    @pl.when(pl.program_id(1) == 0)
    def _():
        s_scr[...] = s0_ref[...]

    ri = lax.broadcasted_iota(I32, (c_len, c_len), 0)
    ci = lax.broadcasted_iota(I32, (c_len, c_len), 1)
    incl = ri >= ci
    strict = ri > ci
    tril = jnp.where(incl, 1.0, 0.0)
    eye_c = jnp.where(ri == ci, 1.0, 0.0)
    lane = lax.broadcasted_iota(I32, (c_len, LANES), 1)
    n_sq = max(0, (c_len - 1).bit_length() - 1)

    tile_aligned = c_len % LANES == 0

    subs = range(n_sub)
    rows = [slice(s * c_len, (s + 1) * c_len) for s in subs]
    gbc = [gb_ref[rows[s], :] for s in subs]
    gcum = [_dot(tril, gbc[s], HI) for s in subs]
    gcum_t = [gcum[s].T if tile_aligned else None for s in subs]
    units = [(s, h) for s in subs for h in range(DN_HEADS)]
    n_u = range(len(units))
    at = [(rows[s], slice(h * DN_DK, (h + 1) * DN_DK)) for s, h in units]
    beta = [gbc[s][:, h:h + 1] for s, h in units]
    gcol = [gcum[s][:, DN_HEADS + h:DN_HEADS + h + 1] for s, h in units]
    if tile_aligned:
        grow = [gcum_t[s][DN_HEADS + h:DN_HEADS + h + 1, :] for s, h in units]
    else:
        grow = [_dot_nt(jnp.where(lane == DN_HEADS + h, 1.0, 0.0), gcum[s], HI) for s, h in units]
    glast = [gcol[i][c_len - 1:c_len, :] for i in n_u]
    decay = [jnp.where(incl, jnp.exp(jnp.where(incl, gcol[i] - grow[i], 0.0)), 0.0) for i in n_u]
    eg = [jnp.exp(gcol[i]) for i in n_u]
    a_mat = [beta[i] * _dot_nt(k_ref[at[i]], k_ref[at[i]]) * jnp.where(strict, decay[i], 0.0) for i in n_u]
    pw = [-a_mat[i] for i in n_u]
    t_inv = [eye_c - a_mat[i] for i in n_u]
    for level in range(n_sq):
        if level < NEUMANN_SPLIT_LEVELS:
            ps = [_split_bf16(pw[i]) for i in n_u]
            pw = [_dot3(ps[i], ps[i]) for i in n_u]
            ps = [_split_bf16(pw[i]) for i in n_u]
            t_inv = [t_inv[i] + _dot3(_split_bf16(t_inv[i]), ps[i]) for i in n_u]
        else:
            pw = [_dot(pw[i].astype(BF16), pw[i].astype(BF16)) for i in n_u]
            t_inv = [t_inv[i] + _dot(t_inv[i].astype(BF16), pw[i].astype(BF16)) for i in n_u]
    wu = [_dot(t_inv[i].astype(BF16),
               jnp.concatenate([beta[i] * eg[i] * k_ref[at[i]], beta[i] * v_ref[at[i]]], axis=1).astype(BF16))
          for i in n_u]
    qk = [_dot_nt(q_ref[at[i]], k_ref[at[i]]) * decay[i] for i in n_u]
    kd = [k_ref[at[i]] * jnp.exp(glast[i] - gcol[i]) for i in n_u]
    kdt = [kd[i].T if tile_aligned else None for i in n_u]
    for i, (s, h) in enumerate(units):
        s_old = s_scr[h]
        u = wu[i][:, DN_DK:DN_DK + DN_DV] - _dot(wu[i][:, 0:DN_DK], s_old)
        o = eg[i] * _dot(q_ref[at[i]], s_old) + _dot(qk[i], u)
        a_scr[at[i]] = _rms(o, DN_DV) * og_ref[...] * sz_ref[at[i]].astype(F32)
        s_scr[h] = jnp.exp(glast[i]) * s_old + (_dot(kdt[i], u) if tile_aligned else _dot_tn(kd[i], u))
    y_ref[...] = x_ref[...] + _dot(a_scr[...].astype(BF16), w_ref[...])

    @pl.when(pl.program_id(1) == pl.num_programs(1) - 1)
    def _():
        sout_ref[...] = s_scr[...]


def _dn_rec(q, k, v, gb, s0, sz, o_gain, w_out, x, n_seq, t_len, chunk, chunks_per_step):
    step = chunk * chunks_per_step
    assert t_len % step == 0
    n_steps = t_len // step
    row = lambda b, c: (b * n_steps + c, 0)
    const = lambda b, c: (0, 0)
    wide = pl.BlockSpec((step, D_MODEL), row)
    st = pl.BlockSpec((None, None, DN_HEADS, DN_DK, DN_DV), lambda b, c: (b, 0, 0, 0, 0))
    return pl.pallas_call(
        functools.partial(_dn_rec_kernel, c_len=chunk),
        grid=(n_seq, n_steps),
        in_specs=[wide, wide, wide, pl.BlockSpec((step, LANES), row), st, wide,
                  pl.BlockSpec((1, DN_DV), const), pl.BlockSpec((D_MODEL, D_MODEL), const), wide],
        out_specs=(wide, st),
        out_shape=(jax.ShapeDtypeStruct((n_seq * t_len, D_MODEL), F32),
                   jax.ShapeDtypeStruct((n_seq, 1, DN_HEADS, DN_DK, DN_DV), F32)),
        scratch_shapes=[pltpu.VMEM((DN_HEADS, DN_DK, DN_DV), F32), pltpu.VMEM((step, D_MODEL), F32)],
        compiler_params=pltpu.CompilerParams(dimension_semantics=("parallel", "arbitrary"),
                                             vmem_limit_bytes=VMEM_LIMIT),
        name="dn_rec",
    )(q, k, v, gb, s0, sz, o_gain, w_out, x)


def _pad_lanes(v, offset=0):
    return jnp.zeros((1, LANES), F32).at[0, offset:offset + v.shape[0]].set(v)


def _split_w(w, main, tail):
    small = jnp.pad(w[:, main:tail], ((0, 0), (0, LANES - (tail - main))))
    return w[:, :main].astype(BF16), small.astype(BF16), w[:, tail:].astype(BF16)


def kernel(x_prompt, x_sample, cache_k, cache_v, cache_idx_k, state_dn_S, state_dn_conv, page_table,
           att_norm, att_w_in, att_q_gain, att_k_gain, att_ik_gain, att_w_out,
           dn_norm, dn_w_in, dn_conv_w, dn_A_log, dn_dt_bias, dn_o_gain, dn_w_out):
    bp, seq, d = x_prompt.shape
    bs, t_new, _ = x_sample.shape
    n_pool = cache_k.shape[0]
    np_rows, ns_rows = bp * seq, bs * t_new
    tm, tm_dn = 512, 256

    xp = x_prompt.reshape(np_rows, d)
    xs = x_sample.reshape(ns_rows, d)

    w_att = _split_w(att_w_in[0], A_IKW, A_IKW + IDX_DIM + IDX_HEADS)
    a_norm = att_norm[0][None, :]
    qg, kg = att_q_gain[0][None, :], att_k_gain[0][None, :]
    ikg = _pad_lanes(att_ik_gain[0])
    w_ao = att_w_out[0].astype(BF16)

    q_p, k_p, kb_p, v_p, vt_p, iq_p, ikw_p, ikb_p, iwt_p, sg_p = _att_proj(xp, a_norm, *w_att, qg, kg, ikg, tm)
    y1_p = _att_prompt(q_p, kb_p, vt_p, ikb_p, iq_p, iwt_p, sg_p, w_ao, xp, bp, seq)

    q_s, k_s, _, v_s, _, iq_s, ikw_s, ikb_s, _, sg_s = _att_proj(xs, a_norm, *w_att, qg, kg, ikg, min(tm, ns_rows))
    pad_new = lambda a, per_tok: jnp.pad(a.reshape(bs, t_new * per_tok, a.shape[-1]),
                                         ((0, 0), (0, (PAGE_SIZE - t_new) * per_tok), (0, 0)))
    o_s = _att_sample(page_table, q_s.astype(F32), iq_s.astype(F32), ikw_s, pad_new(ikb_s, 1),
                      pad_new(k_s, N_KV_HEADS), pad_new(v_s, N_KV_HEADS),
                      jnp.transpose(cache_idx_k, (0, 2, 3, 1)).reshape(n_pool, IDX_DIM, PAGE_SIZE),
                      cache_k.reshape(n_pool, PAGE_SIZE * N_KV_HEADS, HEAD_DIM),
                      cache_v.reshape(n_pool, PAGE_SIZE * N_KV_HEADS, HEAD_DIM), t_new)
    y1_s = _out_proj(o_s, sg_s, w_ao, xs, min(tm, ns_rows))

    w_dn = jnp.pad(dn_w_in[0], ((0, 0), (0, N_END - dn_w_in.shape[-1]))).astype(BF16)
    d_norm = dn_norm[0][None, :]
    alog = _pad_lanes(dn_A_log[0], DN_HEADS)
    dtb = _pad_lanes(dn_dt_bias[0], DN_HEADS)
    o_gain = dn_o_gain[0][None, :]
    w_do = dn_w_out[0].astype(BF16)
    hist = CONV_W - 1

    buf_p = jnp.zeros((bp, SUBLANES, CONV_DIM), F32)
    dq_p, dk_p, dv_p, sz_p, gb_p, cb_p = _dn_proj(y1_p, d_norm, w_dn, dn_conv_w[0], buf_p, alog, dtb,
                                                   n_groups=bp, n_seq=1, tm=tm_dn)
    s0_p = jnp.zeros((bp, 1, DN_HEADS, DN_DK, DN_DV), F32)
    y2_p, s_p = _dn_rec(dq_p, dk_p, dv_p, gb_p, s0_p, sz_p, o_gain, w_do, y1_p, bp, seq, chunk=LANES,
                        chunks_per_step=4)

    to_tm = lambda a: a.reshape(bs, t_new, -1).transpose(1, 0, 2).reshape(ns_rows, -1)
    to_bm = lambda a: a.reshape(t_new, bs, -1).transpose(1, 0, 2).reshape(ns_rows, -1)
    buf_s = state_dn_conv[:, 0].transpose(1, 0, 2).reshape(1, hist * bs, CONV_DIM)
    dq_s, dk_s, dv_s, sz_s, gb_s, cb_s = _dn_proj(to_tm(y1_s), d_norm, w_dn, dn_conv_w[0], buf_s, alog, dtb,
                                                   n_groups=1, n_seq=bs, tm=ns_rows)
    y2_s, s_s = _dn_rec(to_bm(dq_s), to_bm(dk_s), to_bm(dv_s), to_bm(gb_s), state_dn_S,
                        to_bm(sz_s.astype(F32)), o_gain, w_do, y1_s, bs, t_new, chunk=t_new,
                        chunks_per_step=1)

    return (
        y2_p.reshape(bp, seq, d),
        y2_s.reshape(bs, t_new, d),
        k_p.reshape(bp, seq, 1, N_KV_HEADS, HEAD_DIM),
        v_p.reshape(bp, seq, 1, N_KV_HEADS, HEAD_DIM),
        ikw_p[:, :IDX_DIM].reshape(bp, seq, 1, IDX_DIM),
        k_s.reshape(bs, t_new, 1, N_KV_HEADS, HEAD_DIM),
        v_s.reshape(bs, t_new, 1, N_KV_HEADS, HEAD_DIM),
        ikw_s[:, :IDX_DIM].reshape(bs, t_new, 1, IDX_DIM),
        s_p,
        cb_p[:, SUBLANES - hist:, :].reshape(bp, 1, hist, CONV_DIM),
        s_s,
        cb_s.reshape(hist, bs, CONV_DIM).transpose(1, 0, 2).reshape(bs, 1, hist, CONV_DIM),
    )
```

```python
import functools

import jax
import jax.numpy as jnp
from jax import lax
from jax.experimental import pallas as pl
from jax.experimental.pallas import tpu as pltpu

F32 = jnp.float32
BF16 = jnp.bfloat16
I32 = jnp.int32

EPS = 1e-6
LANES = 128
SUBLANES = 8
VMEM_LIMIT = 56 * 1024 * 1024

D_MODEL = 1024
N_HEADS = 8
HEAD_DIM = 128
N_KV_HEADS = 2
GROUP = N_HEADS // N_KV_HEADS
IDX_HEADS = 8
IDX_DIM = 64
TOPK_MAX = 256
PAGE_SIZE = 128
DN_HEADS = 8
DN_DK = 128
DN_DV = 128
CONV_W = 4
KV_W = N_KV_HEADS * HEAD_DIM
CONV_DIM = 3 * DN_HEADS * DN_DK

A_Q, A_K, A_V, A_IQ, A_IKW = 0, 1024, 1280, 1536, 2048
N_Z, N_BA, N_END = 3072, 4096, 4224

INT_MIN = -2147483648
INT_MAX = 2147483647
NEG_BIG = -1e30
NT_DIMS = (((1,), (1,)), ((), ()))
HI = lax.Precision.HIGHEST
NEUMANN_SPLIT_LEVELS = 4


def _dot(a, b, precision=None):
    return jnp.dot(a, b, preferred_element_type=F32, precision=precision)


def _dot_nt(a, b, precision=None):
    return lax.dot_general(a, b, NT_DIMS, preferred_element_type=F32, precision=precision)


def _dot_tn(a, b):
    return lax.dot_general(a, b, (((0,), (0,)), ((), ())), preferred_element_type=F32)


def _split_bf16(x):
    hi = x.astype(BF16)
    return hi, (x - hi.astype(F32)).astype(BF16)


def _dot3(a, b):
    (ah, al), (bh, bl) = a, b
    return _dot(ah, bh) + (_dot(ah, bl) + _dot(al, bh))


def _rms(x, n):
    return x * lax.rsqrt(jnp.sum(x * x, axis=-1, keepdims=True) * (1.0 / n) + EPS)


def _ordinal_to_float(u):
    key = u ^ jnp.int32(INT_MIN)
    return pltpu.bitcast(jnp.where(key < 0, key ^ jnp.int32(INT_MAX), key), F32)


def _att_proj_kernel(x_ref, g_ref, w_ref, wikw_ref, wg_ref, qg_ref, kg_ref, ikg_ref,
                     q_ref, k_ref, kb_ref, v_ref, vt_ref, iq_ref, ikw_ref, ikb_ref, iwt_ref, sg_ref):
    tm = x_ref.shape[0]
    x = x_ref[...]
    xn = (_rms(x, D_MODEL) * g_ref[...]).astype(BF16)
    p = _dot(xn, w_ref[...])
    for h in range(N_HEADS):
        qh = _rms(p[:, A_Q + h * HEAD_DIM:A_Q + (h + 1) * HEAD_DIM], HEAD_DIM) * qg_ref[...]
        q_ref[:, h * HEAD_DIM:(h + 1) * HEAD_DIM] = (qh * (HEAD_DIM ** -0.5)).astype(BF16)
    v = p[:, A_V:A_V + KV_W]
    for g in range(N_KV_HEADS):
        kh = _rms(p[:, A_K + g * HEAD_DIM:A_K + (g + 1) * HEAD_DIM], HEAD_DIM) * kg_ref[...]
        k_ref[pl.ds(g, tm, stride=N_KV_HEADS), :] = kh
        kb_ref[:, g * HEAD_DIM:(g + 1) * HEAD_DIM] = kh.astype(BF16)
        v_ref[pl.ds(g, tm, stride=N_KV_HEADS), :] = v[:, g * HEAD_DIM:(g + 1) * HEAD_DIM]
    vt_ref[...] = v.T.astype(BF16)
    lane = lax.broadcasted_iota(I32, (1, LANES), 1)
    is_ik = lane < IDX_DIM
    for h in range(IDX_HEADS):
        iqh = p[:, A_IQ + (h // 2) * LANES:A_IQ + (h // 2 + 1) * LANES]
        if h % 2:
            iqh = pltpu.roll(iqh, IDX_DIM, axis=1)
        iq_ref[h] = jnp.where(is_ik, iqh, 0.0).astype(BF16)
    ikw = _dot(xn, wikw_ref[...])
    ik = jnp.where(is_ik, ikw, 0.0)
    ikn = _rms(ik, IDX_DIM) * ikg_ref[...]
    out = jnp.where(is_ik, ikn, ikw * (IDX_HEADS ** -0.5 * IDX_DIM ** -0.5))
    ikw_ref[...] = out
    ikb_ref[...] = ikn.astype(BF16)
    iwt_ref[...] = out.T[IDX_DIM:IDX_DIM + IDX_HEADS, :]
    gate = _dot(xn, wg_ref[...])
    sg_ref[...] = (gate * jax.nn.sigmoid(gate)).astype(BF16)


def _att_proj(x, norm_g, w_main, w_ikw, w_gate, q_gain, k_gain, ik_gain_pad, tm):
    n = x.shape[0]
    assert n % tm == 0 and tm % LANES == 0
    row = lambda i: (i, 0)
    const = lambda i: (0, 0)
    out_shape = (
        jax.ShapeDtypeStruct((n, D_MODEL), BF16),
        jax.ShapeDtypeStruct((N_KV_HEADS * n, HEAD_DIM), F32),
        jax.ShapeDtypeStruct((n, KV_W), BF16),
        jax.ShapeDtypeStruct((N_KV_HEADS * n, HEAD_DIM), F32),
        jax.ShapeDtypeStruct((KV_W, n), BF16),
        jax.ShapeDtypeStruct((IDX_HEADS, n, LANES), BF16),
        jax.ShapeDtypeStruct((n, LANES), F32),
        jax.ShapeDtypeStruct((n, LANES), BF16),
        jax.ShapeDtypeStruct((IDX_HEADS, n), F32),
        jax.ShapeDtypeStruct((n, D_MODEL), BF16),
    )
    out_specs = (
        pl.BlockSpec((tm, D_MODEL), row),
        pl.BlockSpec((N_KV_HEADS * tm, HEAD_DIM), row),
        pl.BlockSpec((tm, KV_W), row),
        pl.BlockSpec((N_KV_HEADS * tm, HEAD_DIM), row),
        pl.BlockSpec((KV_W, tm), lambda i: (0, i)),
        pl.BlockSpec((IDX_HEADS, tm, LANES), lambda i: (0, i, 0)),
        pl.BlockSpec((tm, LANES), row),
        pl.BlockSpec((tm, LANES), row),
        pl.BlockSpec((IDX_HEADS, tm), lambda i: (0, i)),
        pl.BlockSpec((tm, D_MODEL), row),
    )
    return pl.pallas_call(
        _att_proj_kernel,
        grid=(n // tm,),
        in_specs=[
            pl.BlockSpec((tm, D_MODEL), row),
            pl.BlockSpec((1, D_MODEL), const),
            pl.BlockSpec((D_MODEL, A_IKW), const),
            pl.BlockSpec((D_MODEL, LANES), const),
            pl.BlockSpec((D_MODEL, D_MODEL), const),
            pl.BlockSpec((1, HEAD_DIM), const),
            pl.BlockSpec((1, HEAD_DIM), const),
            pl.BlockSpec((1, LANES), const),
        ],
        out_specs=out_specs,
        out_shape=out_shape,
        compiler_params=pltpu.CompilerParams(dimension_semantics=("parallel",), vmem_limit_bytes=VMEM_LIMIT),
        name="att_proj",
    )(x, norm_g, w_main, w_ikw, w_gate, q_gain, k_gain, ik_gain_pad)


def _reduce(x, axis, combine, finish):
    n = x.shape[axis]
    unit = 8 * (SUBLANES if axis == 0 else LANES)
    full = n // unit
    if full < 2:
        return finish(x, axis=axis, keepdims=True)
    cut = (lambda i, j: x[i:j, :]) if axis == 0 else (lambda i, j: x[:, i:j])
    acc = cut(0, unit)
    for i in range(1, full):
        acc = combine(acc, cut(i * unit, (i + 1) * unit))
    out = finish(acc, axis=axis, keepdims=True)
    if n % unit:
        out = combine(out, finish(cut(full * unit, n), axis=axis, keepdims=True))
    return out


def _count(mask, axis):
    return _reduce(jnp.where(mask, 1.0, 0.0), axis, jnp.add, jnp.sum)


def _select_bias(sc_ref, lo_ref, n_keys, key_idx, topk, axis):
    sl = (slice(0, n_keys), slice(None)) if axis == 0 else (slice(None), slice(0, n_keys))
    vec = (1, sc_ref.shape[1]) if axis == 0 else (sc_ref.shape[0], 1)
    kf = float(topk)

    def bit_step(i, prefix):
        cand = prefix | (jnp.int32(1) << (31 - i))
        cnt = _count(sc_ref[sl] >= _ordinal_to_float(cand), axis)
        return jnp.where(cnt >= kf, cand, prefix)

    prefix = lax.fori_loop(0, 32, bit_step, jnp.zeros(vec, I32))
    thr = _ordinal_to_float(prefix)
    sc = sc_ref[sl]
    few = jnp.logical_not(_count(sc >= thr, axis) >= kf)
    thr = jnp.where(few, -jnp.inf, thr)
    need = kf - _count(sc > thr, axis)
    excess = (_count(sc >= thr, axis) > kf) & jnp.logical_not(few)
    lo_ref[...] = jnp.full(vec, INT_MAX, I32)

    @pl.when(jnp.max(jnp.where(excess, 1.0, 0.0)) > 0.0)
    def _():
        nbits = max(1, (n_keys - 1).bit_length())

        def idx_step(i, lo):
            cand = lo + (jnp.int32(1) << (nbits - 1 - i))
            cnt = _count((sc_ref[sl] == thr) & (key_idx < cand), axis)
            return jnp.where(cnt < need, cand, lo)

        lo_ref[...] = lax.fori_loop(0, nbits, idx_step, jnp.zeros(vec, I32))

    lo = jnp.where(few, -1, lo_ref[...])
    sel = (sc > thr) | ((sc == thr) & (key_idx <= lo))
    return jnp.where(sel, 0.0, NEG_BIG)


def _att_prompt_body(n_keys, topk, q_ref, kb_ref, vt_ref, ikb_ref, iq_ref, iwt_ref, o_ref,
                     sc_ref, bias_ref, lo_ref):
    qb = q_ref.shape[0]
    t0 = pl.program_id(0) * qb
    ik = ikb_ref[0:n_keys, :]
    iwt = iwt_ref[...]
    score = jnp.zeros((n_keys, qb), F32)
    for h in range(IDX_HEADS):
        s = _dot_nt(ik, iq_ref[h])
        score = score + jnp.maximum(s, 0.0) * iwt[h:h + 1, :]
    key_idx = lax.broadcasted_iota(I32, (n_keys, qb), 0)
    q_pos = t0 + lax.broadcasted_iota(I32, (n_keys, qb), 1)
    sc_ref[0:n_keys, :] = jnp.where(key_idx <= q_pos, score, -jnp.inf)
    bias_ref[0:n_keys, :] = _select_bias(sc_ref, lo_ref, n_keys, key_idx, topk, axis=0)
    heads = range(N_HEADS)
    hd = [slice(h * HEAD_DIM, (h + 1) * HEAD_DIM) for h in heads]
    gd = [slice((h // GROUP) * HEAD_DIM, (h // GROUP + 1) * HEAD_DIM) for h in heads]
    s = [_dot_nt(kb_ref[0:n_keys, gd[h]], q_ref[:, hd[h]]) + bias_ref[0:n_keys, :] for h in heads]
    m = [_reduce(s[h], 0, jnp.maximum, jnp.max) for h in heads]
    p = [jnp.exp(s[h] - m[h]) for h in heads]
    l = [_reduce(p[h], 0, jnp.add, jnp.sum) for h in heads]
    ot = [_dot(vt_ref[gd[h], 0:n_keys], p[h].astype(BF16)) for h in heads]
    for h in heads:
        o_ref[:, hd[h]] = (ot[h] / l[h]).T


def _att_prompt_kernel(q_ref, kb_ref, vt_ref, ikb_ref, iq_ref, iwt_ref, sg_ref, w_ref, x_ref, y_ref,
                       o_scr, sc_ref, bias_ref, lo_ref, *, cls_len, n_cls, topk):
    qb = q_ref.shape[0]
    cls = (pl.program_id(0) * qb) // cls_len
    for c in range(n_cls):
        pl.when(cls == c)(functools.partial(
            _att_prompt_body, cls_len * (c + 1), topk, q_ref, kb_ref, vt_ref, ikb_ref, iq_ref, iwt_ref,
            o_scr, sc_ref, bias_ref, lo_ref))
    a = (o_scr[...] * sg_ref[...].astype(F32)).astype(BF16)
    y_ref[...] = x_ref[...] + _dot(a, w_ref[...])


def _att_prompt(q, kb, vt, ikb, iq, iwt, sg, w_out, x, batch, seq, qb=LANES):
    nb = seq // qb
    n_cls = min(16, nb)
    assert seq % (n_cls * qb) == 0
    topk = min(TOPK_MAX, seq // 4)
    kern = functools.partial(_att_prompt_kernel, cls_len=seq // n_cls, n_cls=n_cls, topk=topk)
    return pl.pallas_call(
        kern,
        grid=(nb, batch),
        in_specs=[
            pl.BlockSpec((qb, D_MODEL), lambda j, b: (b * nb + j, 0)),
            pl.BlockSpec((seq, KV_W), lambda j, b: (b, 0)),
            pl.BlockSpec((KV_W, seq), lambda j, b: (0, b)),
            pl.BlockSpec((seq, LANES), lambda j, b: (b, 0)),
            pl.BlockSpec((IDX_HEADS, qb, LANES), lambda j, b: (0, b * nb + j, 0)),
            pl.BlockSpec((IDX_HEADS, qb), lambda j, b: (0, b * nb + j)),
            pl.BlockSpec((qb, D_MODEL), lambda j, b: (b * nb + j, 0)),
            pl.BlockSpec((D_MODEL, D_MODEL), lambda j, b: (0, 0)),
            pl.BlockSpec((qb, D_MODEL), lambda j, b: (b * nb + j, 0)),
        ],
        out_specs=pl.BlockSpec((qb, D_MODEL), lambda j, b: (b * nb + j, 0)),
        out_shape=jax.ShapeDtypeStruct((batch * seq, D_MODEL), F32),
        scratch_shapes=[
            pltpu.VMEM((qb, D_MODEL), F32),
            pltpu.VMEM((seq, qb), F32),
            pltpu.VMEM((seq, qb), F32),
            pltpu.VMEM((1, qb), I32),
        ],
        compiler_params=pltpu.CompilerParams(dimension_semantics=("parallel", "arbitrary"),
                                             vmem_limit_bytes=VMEM_LIMIT),
        name="att_prompt",
    )(q, kb, vt, ikb, iq, iwt, sg, w_out, x)


def _att_sample_kernel(pt_ref, q_ref, iq_ref, ikw_ref, ikn_ref, kn_ref, vn_ref,
                       cik_hbm, ck_hbm, cv_hbm, o_ref,
                       ikbuf, kbuf, vbuf, sc_ref, bias_ref, lo_ref, sem_i, sem_kv,
                       *, n_pages, chunk_pages, topk):
    b = pl.program_id(0)
    t_new = q_ref.shape[0]
    past = n_pages * PAGE_SIZE
    n_keys = past + PAGE_SIZE
    n_chunks = n_pages // chunk_pages
    ck = chunk_pages * PAGE_SIZE

    n_seq = pl.num_programs(0)
    islot = b % 2
    ahead = n_chunks - 1

    def idx_copy(seq, slot, p):
        return pltpu.make_async_copy(cik_hbm.at[pt_ref[seq, p]],
                                     ikbuf.at[slot, :, pl.ds(p * PAGE_SIZE, PAGE_SIZE)], sem_i.at[slot])

    def kv_copies(seq, c, p):
        page = pt_ref[seq, c * chunk_pages + p]
        dst = pl.ds(p * PAGE_SIZE * N_KV_HEADS, PAGE_SIZE * N_KV_HEADS)
        return (pltpu.make_async_copy(ck_hbm.at[page], kbuf.at[c, dst], sem_kv.at[0, c]),
                pltpu.make_async_copy(cv_hbm.at[page], vbuf.at[c, dst], sem_kv.at[1, c]))

    def start_idx(seq, slot):
        def go(p, carry):
            idx_copy(seq, slot, p).start()
            return carry
        lax.fori_loop(0, n_pages, go, 0, unroll=8)

    def wait_idx(seq, slot):
        def go(p, carry):
            idx_copy(seq, slot, p).wait()
            return carry
        lax.fori_loop(0, n_pages, go, 0, unroll=8)

    def start_chunk(seq, c):
        def go(p, carry):
            for cp in kv_copies(seq, c, p):
                cp.start()
            return carry
        lax.fori_loop(0, chunk_pages, go, 0, unroll=8)

    def wait_chunk(seq, c):
        def go(p, carry):
            for cp in kv_copies(seq, c, p):
                cp.wait()
            return carry
        lax.fori_loop(0, chunk_pages, go, 0, unroll=8)

    @pl.when(b == 0)
    def _():
        start_idx(b, islot)
        for c in range(ahead):
            start_chunk(b, c)

    @pl.when(b + 1 < n_seq)
    def _():
        start_idx(b + 1, 1 - islot)

    wait_idx(b, islot)

    iq2 = iq_ref[...].reshape(IDX_HEADS * t_new, LANES).astype(BF16)
    ikw = ikw_ref[...]

    def scores(ik_rows):
        s = _dot_nt(iq2, ik_rows)
        acc = jnp.zeros((t_new, ik_rows.shape[0]), F32)
        for h in range(IDX_HEADS):
            acc = acc + jnp.maximum(s[h * t_new:(h + 1) * t_new, :], 0.0) * ikw[:, IDX_DIM + h:IDX_DIM + h + 1]
        return acc

    iq2_past = iq2[:, 0:IDX_DIM]
    for c in range(n_chunks):
        s = _dot(iq2_past, ikbuf[islot, :, c * ck:(c + 1) * ck].astype(BF16))
        acc = jnp.zeros((t_new, ck), F32)
        for h in range(IDX_HEADS):
            acc = acc + jnp.maximum(s[h * t_new:(h + 1) * t_new, :], 0.0) * ikw[:, IDX_DIM + h:IDX_DIM + h + 1]
        sc_ref[:, c * ck:(c + 1) * ck] = acc
    new_idx = lax.broadcasted_iota(I32, (t_new, PAGE_SIZE), 1)
    tok = lax.broadcasted_iota(I32, (t_new, PAGE_SIZE), 0)
    sc_ref[:, past:n_keys] = jnp.where(new_idx <= tok, scores(ikn_ref[...]), -jnp.inf)
    key_idx = lax.broadcasted_iota(I32, (t_new, n_keys), 1)
    bias_ref[...] = _select_bias(sc_ref, lo_ref, n_keys, key_idx, topk, axis=1)

    rows = GROUP * t_new
    qs = [jnp.concatenate([q_ref[:, (g * GROUP + r) * HEAD_DIM:(g * GROUP + r + 1) * HEAD_DIM]
                           for r in range(GROUP)], axis=0).astype(BF16) for g in range(N_KV_HEADS)]
    m = [jnp.full((rows, 1), NEG_BIG, F32) for _ in range(N_KV_HEADS)]
    l = [jnp.zeros((rows, 1), F32) for _ in range(N_KV_HEADS)]
    acc = [jnp.zeros((rows, HEAD_DIM), F32) for _ in range(N_KV_HEADS)]

    def attend(g, k_rows, v_rows, bias):
        s = _dot_nt(qs[g], k_rows) + jnp.concatenate([bias] * GROUP, axis=0)
        m_new = jnp.maximum(m[g], _reduce(s, 1, jnp.maximum, jnp.max))
        alpha = jnp.exp(m[g] - m_new)
        p = jnp.exp(s - m_new)
        l[g] = alpha * l[g] + _reduce(p, 1, jnp.add, jnp.sum)
        acc[g] = alpha * acc[g] + _dot(p.astype(BF16), v_rows)
        m[g] = m_new

    for g in range(N_KV_HEADS):
        new_rows = pl.ds(g, PAGE_SIZE, stride=N_KV_HEADS)
        attend(g, kn_ref[new_rows, :].astype(BF16), vn_ref[new_rows, :].astype(BF16), bias_ref[:, past:n_keys])
    for c in range(n_chunks):
        nxt = c + ahead
        if nxt < n_chunks:
            start_chunk(b, nxt)
        else:
            pl.when(b + 1 < n_seq)(functools.partial(start_chunk, b + 1, nxt - n_chunks))
        wait_chunk(b, c)
        for g in range(N_KV_HEADS):
            head_rows = pl.ds(g, ck, stride=N_KV_HEADS)
            attend(g, kbuf[c, head_rows, :].astype(BF16), vbuf[c, head_rows, :].astype(BF16),
                   bias_ref[:, c * ck:(c + 1) * ck])
    for g in range(N_KV_HEADS):
        og = acc[g] / l[g]
        for r in range(GROUP):
            h = g * GROUP + r
            o_ref[:, h * HEAD_DIM:(h + 1) * HEAD_DIM] = og[r * t_new:(r + 1) * t_new, :]


def _att_sample(page_table, q, iq, ikw, ikn_pad, kn_pad, vn_pad, cache_ik, cache_k, cache_v, t_new):
    nseq, n_pages = page_table.shape
    assert t_new == SUBLANES
    chunk_pages = min(32, n_pages // 2)
    assert n_pages % chunk_pages == 0
    past = n_pages * PAGE_SIZE
    n_keys = past + PAGE_SIZE
    topk = min(TOPK_MAX, (past + t_new) // 4)
    ck = chunk_pages * PAGE_SIZE
    kern = functools.partial(_att_sample_kernel, n_pages=n_pages, chunk_pages=chunk_pages, topk=topk)
    grid_spec = pltpu.PrefetchScalarGridSpec(
        num_scalar_prefetch=1,
        grid=(nseq,),
        in_specs=[
            pl.BlockSpec((t_new, D_MODEL), lambda b, pt: (b, 0)),
            pl.BlockSpec((IDX_HEADS, t_new, LANES), lambda b, pt: (0, b, 0)),
            pl.BlockSpec((t_new, LANES), lambda b, pt: (b, 0)),
            pl.BlockSpec((None, PAGE_SIZE, LANES), lambda b, pt: (b, 0, 0)),
            pl.BlockSpec((None, PAGE_SIZE * N_KV_HEADS, HEAD_DIM), lambda b, pt: (b, 0, 0)),
            pl.BlockSpec((None, PAGE_SIZE * N_KV_HEADS, HEAD_DIM), lambda b, pt: (b, 0, 0)),
            pl.BlockSpec(memory_space=pl.ANY),
            pl.BlockSpec(memory_space=pl.ANY),
            pl.BlockSpec(memory_space=pl.ANY),
        ],
        out_specs=pl.BlockSpec((t_new, D_MODEL), lambda b, pt: (b, 0)),
        scratch_shapes=[
            pltpu.VMEM((2, IDX_DIM, past), F32),
            pltpu.VMEM((n_pages // chunk_pages, ck * N_KV_HEADS, HEAD_DIM), F32),
            pltpu.VMEM((n_pages // chunk_pages, ck * N_KV_HEADS, HEAD_DIM), F32),
            pltpu.VMEM((t_new, n_keys), F32),
            pltpu.VMEM((t_new, n_keys), F32),
            pltpu.VMEM((t_new, 1), I32),
            pltpu.SemaphoreType.DMA((2,)),
            pltpu.SemaphoreType.DMA((2, n_pages // chunk_pages)),
        ],
    )
    return pl.pallas_call(
        kern,
        grid_spec=grid_spec,
        out_shape=jax.ShapeDtypeStruct((nseq * t_new, D_MODEL), F32),
        compiler_params=pltpu.CompilerParams(dimension_semantics=("arbitrary",), vmem_limit_bytes=VMEM_LIMIT),
        name="att_sample",
    )(page_table, q, iq, ikw, ikn_pad, kn_pad, vn_pad, cache_ik, cache_k, cache_v)


def _out_proj_kernel(o_ref, sg_ref, w_ref, x_ref, y_ref):
    a = (o_ref[...] * sg_ref[...].astype(F32)).astype(BF16)
    y_ref[...] = x_ref[...] + _dot(a, w_ref[...])


def _out_proj(o, sg, w_out, x, tm):
    n = o.shape[0]
    row = pl.BlockSpec((tm, D_MODEL), lambda i: (i, 0))
    return pl.pallas_call(
        _out_proj_kernel,
        grid=(n // tm,),
        in_specs=[row, row, pl.BlockSpec((D_MODEL, D_MODEL), lambda i: (0, 0)), row],
        out_specs=row,
        out_shape=jax.ShapeDtypeStruct((n, D_MODEL), F32),
        compiler_params=pltpu.CompilerParams(dimension_semantics=("parallel",), vmem_limit_bytes=VMEM_LIMIT),
        name="out_proj",
    )(o, sg, w_out, x)


def _dn_proj_kernel(x_ref, g_ref, w_ref, cw_ref, buf_ref, alog_ref, dtb_ref,
                    q_ref, k_ref, v_ref, sz_ref, gb_ref, cbuf_ref, xp_ref, *, n_seq, pad):
    tm = x_ref.shape[0]

    @pl.when(pl.program_id(1) == 0)
    def _():
        xp_ref[0:pad, :] = buf_ref[...]

    xn = (_rms(x_ref[...], D_MODEL) * g_ref[...]).astype(BF16)
    p = _dot(xn, w_ref[...])
    xp_ref[pad:pad + tm, :] = p[:, 0:CONV_DIM]
    conv = xp_ref[pad:pad + tm, :] * cw_ref[CONV_W - 1:CONV_W, :]
    for i in range(1, CONV_W):
        conv = conv + xp_ref[pad - i * n_seq:pad - i * n_seq + tm, :] * cw_ref[CONV_W - 1 - i:CONV_W - i, :]
    act = conv * jax.nn.sigmoid(conv)
    for h in range(DN_HEADS):
        sl = slice(h * DN_DK, (h + 1) * DN_DK)
        qh = act[:, sl]
        q_ref[:, sl] = qh * lax.rsqrt(jnp.sum(qh * qh, axis=-1, keepdims=True) + EPS) * (DN_DK ** -0.5)
        kh = act[:, DN_HEADS * DN_DK + h * DN_DK:DN_HEADS * DN_DK + (h + 1) * DN_DK]
        k_ref[:, sl] = kh * lax.rsqrt(jnp.sum(kh * kh, axis=-1, keepdims=True) + EPS)
    v_ref[...] = act[:, 2 * DN_HEADS * DN_DK:CONV_DIM]
    z = p[:, N_Z:N_BA]
    sz_ref[...] = (z * jax.nn.sigmoid(z)).astype(BF16)
    ba = p[:, N_BA:N_END]
    sp_in = ba + dtb_ref[...]
    softplus = jnp.maximum(sp_in, 0.0) + jnp.log1p(jnp.exp(-jnp.abs(sp_in)))
    lane = lax.broadcasted_iota(I32, (1, LANES), 1)
    gb_ref[...] = jnp.where(lane < DN_HEADS, jax.nn.sigmoid(ba), -jnp.exp(alog_ref[...]) * softplus)
    cbuf_ref[...] = xp_ref[tm:tm + pad, :]
    xp_ref[0:pad, :] = xp_ref[tm:tm + pad, :]


def _dn_proj(x, norm_g, w_packed, conv_w, buf, alog_pad, dtb_pad, n_groups, n_seq, tm):
    n = x.shape[0]
    pad = buf.shape[1]
    tiles = n // n_groups // tm
    assert n == n_groups * tiles * tm and pad % SUBLANES == 0 and pad >= (CONV_W - 1) * n_seq
    row = lambda s, i: (s * tiles + i, 0)
    const = lambda s, i: (0, 0)
    big = lambda d: pl.BlockSpec((tm, d), row)
    return pl.pallas_call(
        functools.partial(_dn_proj_kernel, n_seq=n_seq, pad=pad),
        grid=(n_groups, tiles),
        in_specs=[
            big(D_MODEL),
            pl.BlockSpec((1, D_MODEL), const),
            pl.BlockSpec((D_MODEL, N_END), const),
            pl.BlockSpec((CONV_W, CONV_DIM), const),
            pl.BlockSpec((None, pad, CONV_DIM), lambda s, i: (s, 0, 0)),
            pl.BlockSpec((1, LANES), const),
            pl.BlockSpec((1, LANES), const),
        ],
        out_specs=(big(D_MODEL), big(D_MODEL), big(D_MODEL), big(D_MODEL), big(LANES),
                   pl.BlockSpec((None, pad, CONV_DIM), lambda s, i: (s, 0, 0))),
        out_shape=(
            jax.ShapeDtypeStruct((n, D_MODEL), F32),
            jax.ShapeDtypeStruct((n, D_MODEL), F32),
            jax.ShapeDtypeStruct((n, D_MODEL), F32),
            jax.ShapeDtypeStruct((n, D_MODEL), BF16),
            jax.ShapeDtypeStruct((n, LANES), F32),
            jax.ShapeDtypeStruct((n_groups, pad, CONV_DIM), F32),
        ),
        scratch_shapes=[pltpu.VMEM((pad + tm, CONV_DIM), F32)],
        compiler_params=pltpu.CompilerParams(dimension_semantics=("parallel", "arbitrary"),
                                             vmem_limit_bytes=VMEM_LIMIT),
        name="dn_proj",
    )(x, norm_g, w_packed, conv_w, buf, alog_pad, dtb_pad)


def _dn_rec_kernel(q_ref, k_ref, v_ref, gb_ref, s0_ref, sz_ref, og_ref, w_ref, x_ref, y_ref, sout_ref,
                   s_scr, a_scr, *, c_len):
    n_sub = q_ref.shape[0] // c_len

    @pl.when(pl.program_id(1) == 0)
    def _():
        s_scr[...] = s0_ref[...]

    ri = lax.broadcasted_iota(I32, (c_len, c_len), 0)
    ci = lax.broadcasted_iota(I32, (c_len, c_len), 1)
    incl = ri >= ci
    strict = ri > ci
    tril = jnp.where(incl, 1.0, 0.0)
    eye_c = jnp.where(ri == ci, 1.0, 0.0)
    lane = lax.broadcasted_iota(I32, (c_len, LANES), 1)
    n_sq = max(0, (c_len - 1).bit_length() - 1)

    tile_aligned = c_len % LANES == 0

    subs = range(n_sub)
    rows = [slice(s * c_len, (s + 1) * c_len) for s in subs]
    gbc = [gb_ref[rows[s], :] for s in subs]
    gcum = [_dot(tril, gbc[s], HI) for s in subs]
    gcum_t = [gcum[s].T if tile_aligned else None for s in subs]
    units = [(s, h) for s in subs for h in range(DN_HEADS)]
    n_u = range(len(units))
    at = [(rows[s], slice(h * DN_DK, (h + 1) * DN_DK)) for s, h in units]
    beta = [gbc[s][:, h:h + 1] for s, h in units]
    gcol = [gcum[s][:, DN_HEADS + h:DN_HEADS + h + 1] for s, h in units]
    if tile_aligned:
        grow = [gcum_t[s][DN_HEADS + h:DN_HEADS + h + 1, :] for s, h in units]
    else:
        grow = [_dot_nt(jnp.where(lane == DN_HEADS + h, 1.0, 0.0), gcum[s], HI) for s, h in units]
    glast = [gcol[i][c_len - 1:c_len, :] for i in n_u]
    decay = [jnp.where(incl, jnp.exp(jnp.where(incl, gcol[i] - grow[i], 0.0)), 0.0) for i in n_u]
    eg = [jnp.exp(gcol[i]) for i in n_u]
    a_mat = [beta[i] * _dot_nt(k_ref[at[i]], k_ref[at[i]]) * jnp.where(strict, decay[i], 0.0) for i in n_u]
    pw = [-a_mat[i] for i in n_u]
    t_inv = [eye_c - a_mat[i] for i in n_u]
    for level in range(n_sq):
        if level < NEUMANN_SPLIT_LEVELS:
            ps = [_split_bf16(pw[i]) for i in n_u]
            pw = [_dot3(ps[i], ps[i]) for i in n_u]
            ps = [_split_bf16(pw[i]) for i in n_u]
            t_inv = [t_inv[i] + _dot3(_split_bf16(t_inv[i]), ps[i]) for i in n_u]
        else:
            pw = [_dot(pw[i].astype(BF16), pw[i].astype(BF16)) for i in n_u]
            t_inv = [t_inv[i] + _dot(t_inv[i].astype(BF16), pw[i].astype(BF16)) for i in n_u]
    wu = [_dot(t_inv[i].astype(BF16),
               jnp.concatenate([beta[i] * eg[i] * k_ref[at[i]], beta[i] * v_ref[at[i]]], axis=1).astype(BF16))
          for i in n_u]
    qk = [_dot_nt(q_ref[at[i]], k_ref[at[i]]) * decay[i] for i in n_u]
    kd = [k_ref[at[i]] * jnp.exp(glast[i] - gcol[i]) for i in n_u]
    kdt = [kd[i].T if tile_aligned else None for i in n_u]
    for i, (s, h) in enumerate(units):
        s_old = s_scr[h]
        u = wu[i][:, DN_DK:DN_DK + DN_DV] - _dot(wu[i][:, 0:DN_DK], s_old)
        o = eg[i] * _dot(q_ref[at[i]], s_old) + _dot(qk[i], u)
        a_scr[at[i]] = _rms(o, DN_DV) * og_ref[...] * sz_ref[at[i]].astype(F32)
        s_scr[h] = jnp.exp(glast[i]) * s_old + (_dot(kdt[i], u) if tile_aligned else _dot_tn(kd[i], u))
    y_ref[...] = x_ref[...] + _dot(a_scr[...].astype(BF16), w_ref[...])

    @pl.when(pl.program_id(1) == pl.num_programs(1) - 1)
    def _():
        sout_ref[...] = s_scr[...]


def _dn_rec(q, k, v, gb, s0, sz, o_gain, w_out, x, n_seq, t_len, chunk, chunks_per_step):
    step = chunk * chunks_per_step
    assert t_len % step == 0
    n_steps = t_len // step
    row = lambda b, c: (b * n_steps + c, 0)
    const = lambda b, c: (0, 0)
    wide = pl.BlockSpec((step, D_MODEL), row)
    st = pl.BlockSpec((None, None, DN_HEADS, DN_DK, DN_DV), lambda b, c: (b, 0, 0, 0, 0))
    return pl.pallas_call(
        functools.partial(_dn_rec_kernel, c_len=chunk),
        grid=(n_seq, n_steps),
        in_specs=[wide, wide, wide, pl.BlockSpec((step, LANES), row), st, wide,
                  pl.BlockSpec((1, DN_DV), const), pl.BlockSpec((D_MODEL, D_MODEL), const), wide],
        out_specs=(wide, st),
        out_shape=(jax.ShapeDtypeStruct((n_seq * t_len, D_MODEL), F32),
                   jax.ShapeDtypeStruct((n_seq, 1, DN_HEADS, DN_DK, DN_DV), F32)),
        scratch_shapes=[pltpu.VMEM((DN_HEADS, DN_DK, DN_DV), F32), pltpu.VMEM((step, D_MODEL), F32)],
        compiler_params=pltpu.CompilerParams(dimension_semantics=("parallel", "arbitrary"),
                                             vmem_limit_bytes=VMEM_LIMIT),
        name="dn_rec",
    )(q, k, v, gb, s0, sz, o_gain, w_out, x)


def _pad_lanes(v, offset=0):
    return jnp.zeros((1, LANES), F32).at[0, offset:offset + v.shape[0]].set(v)


def _split_w(w, main, tail):
    small = jnp.pad(w[:, main:tail], ((0, 0), (0, LANES - (tail - main))))
    return w[:, :main].astype(BF16), small.astype(BF16), w[:, tail:].astype(BF16)


def kernel(x_prompt, x_sample, cache_k, cache_v, cache_idx_k, state_dn_S, state_dn_conv, page_table,
           att_norm, att_w_in, att_q_gain, att_k_gain, att_ik_gain, att_w_out,
           dn_norm, dn_w_in, dn_conv_w, dn_A_log, dn_dt_bias, dn_o_gain, dn_w_out):
    bp, seq, d = x_prompt.shape
    bs, t_new, _ = x_sample.shape
    n_pool = cache_k.shape[0]
    np_rows, ns_rows = bp * seq, bs * t_new
    tm, tm_dn = 512, 256

    xp = x_prompt.reshape(np_rows, d)
    xs = x_sample.reshape(ns_rows, d)

    w_att = _split_w(att_w_in[0], A_IKW, A_IKW + IDX_DIM + IDX_HEADS)
    a_norm = att_norm[0][None, :]
    qg, kg = att_q_gain[0][None, :], att_k_gain[0][None, :]
    ikg = _pad_lanes(att_ik_gain[0])
    w_ao = att_w_out[0].astype(BF16)

    q_p, k_p, kb_p, v_p, vt_p, iq_p, ikw_p, ikb_p, iwt_p, sg_p = _att_proj(xp, a_norm, *w_att, qg, kg, ikg, tm)
    y1_p = _att_prompt(q_p, kb_p, vt_p, ikb_p, iq_p, iwt_p, sg_p, w_ao, xp, bp, seq)

    q_s, k_s, _, v_s, _, iq_s, ikw_s, ikb_s, _, sg_s = _att_proj(xs, a_norm, *w_att, qg, kg, ikg, min(tm, ns_rows))
    pad_new = lambda a, per_tok: jnp.pad(a.reshape(bs, t_new * per_tok, a.shape[-1]),
                                         ((0, 0), (0, (PAGE_SIZE - t_new) * per_tok), (0, 0)))
    o_s = _att_sample(page_table, q_s.astype(F32), iq_s.astype(F32), ikw_s, pad_new(ikb_s, 1),
                      pad_new(k_s, N_KV_HEADS), pad_new(v_s, N_KV_HEADS),
                      jnp.transpose(cache_idx_k, (0, 2, 3, 1)).reshape(n_pool, IDX_DIM, PAGE_SIZE),
                      cache_k.reshape(n_pool, PAGE_SIZE * N_KV_HEADS, HEAD_DIM),
                      cache_v.reshape(n_pool, PAGE_SIZE * N_KV_HEADS, HEAD_DIM), t_new)
    y1_s = _out_proj(o_s, sg_s, w_ao, xs, min(tm, ns_rows))

    w_dn = jnp.pad(dn_w_in[0], ((0, 0), (0, N_END - dn_w_in.shape[-1]))).astype(BF16)
    d_norm = dn_norm[0][None, :]
    alog = _pad_lanes(dn_A_log[0], DN_HEADS)
    dtb = _pad_lanes(dn_dt_bias[0], DN_HEADS)
    o_gain = dn_o_gain[0][None, :]
    w_do = dn_w_out[0].astype(BF16)
    hist = CONV_W - 1

    buf_p = jnp.zeros((bp, SUBLANES, CONV_DIM), F32)
    dq_p, dk_p, dv_p, sz_p, gb_p, cb_p = _dn_proj(y1_p, d_norm, w_dn, dn_conv_w[0], buf_p, alog, dtb,
                                                   n_groups=bp, n_seq=1, tm=tm_dn)
    s0_p = jnp.zeros((bp, 1, DN_HEADS, DN_DK, DN_DV), F32)
    y2_p, s_p = _dn_rec(dq_p, dk_p, dv_p, gb_p, s0_p, sz_p, o_gain, w_do, y1_p, bp, seq, chunk=LANES,
                        chunks_per_step=4)

    to_tm = lambda a: a.reshape(bs, t_new, -1).transpose(1, 0, 2).reshape(ns_rows, -1)
    to_bm = lambda a: a.reshape(t_new, bs, -1).transpose(1, 0, 2).reshape(ns_rows, -1)
    buf_s = state_dn_conv[:, 0].transpose(1, 0, 2).reshape(1, hist * bs, CONV_DIM)
    dq_s, dk_s, dv_s, sz_s, gb_s, cb_s = _dn_proj(to_tm(y1_s), d_norm, w_dn, dn_conv_w[0], buf_s, alog, dtb,
                                                   n_groups=1, n_seq=bs, tm=ns_rows)
    y2_s, s_s = _dn_rec(to_bm(dq_s), to_bm(dk_s), to_bm(dv_s), to_bm(gb_s), state_dn_S,
                        to_bm(sz_s.astype(F32)), o_gain, w_do, y1_s, bs, t_new, chunk=t_new,
                        chunks_per_step=1)

    return (
        y2_p.reshape(bp, seq, d),
        y2_s.reshape(bs, t_new, d),
        k_p.reshape(bp, seq, 1, N_KV_HEADS, HEAD_DIM),
        v_p.reshape(bp, seq, 1, N_KV_HEADS, HEAD_DIM),
        ikw_p[:, :IDX_DIM].reshape(bp, seq, 1, IDX_DIM),
        k_s.reshape(bs, t_new, 1, N_KV_HEADS, HEAD_DIM),
        v_s.reshape(bs, t_new, 1, N_KV_HEADS, HEAD_DIM),
        ikw_s[:, :IDX_DIM].reshape(bs, t_new, 1, IDX_DIM),
        s_p,
        cb_p[:, SUBLANES - hist:, :].reshape(bp, 1, hist, CONV_DIM),
        s_s,
        cb_s.reshape(hist, bs, CONV_DIM).transpose(1, 0, 2).reshape(bs, 1, hist, CONV_DIM),
    )
```

```python
import functools

import jax
import jax.numpy as jnp
from jax import lax
from jax.experimental import pallas as pl
from jax.experimental.pallas import tpu as pltpu

F32 = jnp.float32
BF16 = jnp.bfloat16
I32 = jnp.int32

EPS = 1e-6
LANES = 128
SUBLANES = 8
VMEM_LIMIT = 56 * 1024 * 1024

D_MODEL = 1024
N_HEADS = 8
HEAD_DIM = 128
N_KV_HEADS = 2
GROUP = N_HEADS // N_KV_HEADS
IDX_HEADS = 8
IDX_DIM = 64
TOPK_MAX = 256
PAGE_SIZE = 128
DN_HEADS = 8
DN_DK = 128
DN_DV = 128
CONV_W = 4
KV_W = N_KV_HEADS * HEAD_DIM
CONV_DIM = 3 * DN_HEADS * DN_DK

A_Q, A_K, A_V, A_IQ, A_IKW = 0, 1024, 1280, 1536, 2048
N_Z, N_BA, N_END = 3072, 4096, 4224

INT_MIN = -2147483648
INT_MAX = 2147483647
NEG_BIG = -1e30
NT_DIMS = (((1,), (1,)), ((), ()))
HI = lax.Precision.HIGHEST
NEUMANN_SPLIT_LEVELS = 4


def _dot(a, b, precision=None):
    return jnp.dot(a, b, preferred_element_type=F32, precision=precision)


def _dot_nt(a, b, precision=None):
    return lax.dot_general(a, b, NT_DIMS, preferred_element_type=F32, precision=precision)


def _dot_tn(a, b):
    return lax.dot_general(a, b, (((0,), (0,)), ((), ())), preferred_element_type=F32)


def _split_bf16(x):
    hi = x.astype(BF16)
    return hi, (x - hi.astype(F32)).astype(BF16)


def _dot3(a, b):
    (ah, al), (bh, bl) = a, b
    return _dot(ah, bh) + (_dot(ah, bl) + _dot(al, bh))


def _rms(x, n):
    return x * lax.rsqrt(jnp.sum(x * x, axis=-1, keepdims=True) * (1.0 / n) + EPS)


def _ordinal_to_float(u):
    key = u ^ jnp.int32(INT_MIN)
    return pltpu.bitcast(jnp.where(key < 0, key ^ jnp.int32(INT_MAX), key), F32)


def _att_proj_kernel(x_ref, g_ref, w_ref, wikw_ref, wg_ref, qg_ref, kg_ref, ikg_ref,
                     q_ref, k_ref, kb_ref, v_ref, vt_ref, iq_ref, ikw_ref, ikb_ref, iwt_ref, sg_ref):
    tm = x_ref.shape[0]
    x = x_ref[...]
    xn = (_rms(x, D_MODEL) * g_ref[...]).astype(BF16)
    p = _dot(xn, w_ref[...])
    q_gain = qg_ref[...] * (HEAD_DIM ** -0.5)
    for h in range(N_HEADS):
        qh = _rms(p[:, A_Q + h * HEAD_DIM:A_Q + (h + 1) * HEAD_DIM], HEAD_DIM) * q_gain
        q_ref[:, h * HEAD_DIM:(h + 1) * HEAD_DIM] = qh.astype(BF16)
    v = p[:, A_V:A_V + KV_W]
    for g in range(N_KV_HEADS):
        kh = _rms(p[:, A_K + g * HEAD_DIM:A_K + (g + 1) * HEAD_DIM], HEAD_DIM) * kg_ref[...]
        k_ref[pl.ds(g, tm, stride=N_KV_HEADS), :] = kh
        kb_ref[:, g * HEAD_DIM:(g + 1) * HEAD_DIM] = kh.astype(BF16)
        v_ref[pl.ds(g, tm, stride=N_KV_HEADS), :] = v[:, g * HEAD_DIM:(g + 1) * HEAD_DIM]
    vt_ref[...] = v.T.astype(BF16)
    lane = lax.broadcasted_iota(I32, (1, LANES), 1)
    is_ik = lane < IDX_DIM
    for h in range(IDX_HEADS):
        iqh = p[:, A_IQ + (h // 2) * LANES:A_IQ + (h // 2 + 1) * LANES]
        if h % 2:
            iqh = pltpu.roll(iqh, IDX_DIM, axis=1)
        iq_ref[h] = jnp.where(is_ik, iqh, 0.0).astype(BF16)
    ikw = _dot(xn, wikw_ref[...])
    ik = jnp.where(is_ik, ikw, 0.0)
    ikn = _rms(ik, IDX_DIM) * ikg_ref[...]
    out = jnp.where(is_ik, ikn, ikw * (IDX_HEADS ** -0.5 * IDX_DIM ** -0.5))
    ikw_ref[...] = out
    ikb_ref[...] = ikn.astype(BF16)
    iwt_ref[...] = out.T[IDX_DIM:IDX_DIM + IDX_HEADS, :]
    gate = _dot(xn, wg_ref[...])
    sg_ref[...] = (gate * jax.nn.sigmoid(gate)).astype(BF16)


def _att_proj(x, norm_g, w_main, w_ikw, w_gate, q_gain, k_gain, ik_gain_pad, tm):
    n = x.shape[0]
    assert n % tm == 0 and tm % LANES == 0
    row = lambda i: (i, 0)
    const = lambda i: (0, 0)
    out_shape = (
        jax.ShapeDtypeStruct((n, D_MODEL), BF16),
        jax.ShapeDtypeStruct((N_KV_HEADS * n, HEAD_DIM), F32),
        jax.ShapeDtypeStruct((n, KV_W), BF16),
        jax.ShapeDtypeStruct((N_KV_HEADS * n, HEAD_DIM), F32),
        jax.ShapeDtypeStruct((KV_W, n), BF16),
        jax.ShapeDtypeStruct((IDX_HEADS, n, LANES), BF16),
        jax.ShapeDtypeStruct((n, LANES), F32),
        jax.ShapeDtypeStruct((n, LANES), BF16),
        jax.ShapeDtypeStruct((IDX_HEADS, n), F32),
        jax.ShapeDtypeStruct((n, D_MODEL), BF16),
    )
    out_specs = (
        pl.BlockSpec((tm, D_MODEL), row),
        pl.BlockSpec((N_KV_HEADS * tm, HEAD_DIM), row),
        pl.BlockSpec((tm, KV_W), row),
        pl.BlockSpec((N_KV_HEADS * tm, HEAD_DIM), row),
        pl.BlockSpec((KV_W, tm), lambda i: (0, i)),
        pl.BlockSpec((IDX_HEADS, tm, LANES), lambda i: (0, i, 0)),
        pl.BlockSpec((tm, LANES), row),
        pl.BlockSpec((tm, LANES), row),
        pl.BlockSpec((IDX_HEADS, tm), lambda i: (0, i)),
        pl.BlockSpec((tm, D_MODEL), row),
    )
    return pl.pallas_call(
        _att_proj_kernel,
        grid=(n // tm,),
        in_specs=[
            pl.BlockSpec((tm, D_MODEL), row),
            pl.BlockSpec((1, D_MODEL), const),
            pl.BlockSpec((D_MODEL, A_IKW), const),
            pl.BlockSpec((D_MODEL, LANES), const),
            pl.BlockSpec((D_MODEL, D_MODEL), const),
            pl.BlockSpec((1, HEAD_DIM), const),
            pl.BlockSpec((1, HEAD_DIM), const),
            pl.BlockSpec((1, LANES), const),
        ],
        out_specs=out_specs,
        out_shape=out_shape,
        compiler_params=pltpu.CompilerParams(dimension_semantics=("parallel",), vmem_limit_bytes=VMEM_LIMIT),
        name="att_proj",
    )(x, norm_g, w_main, w_ikw, w_gate, q_gain, k_gain, ik_gain_pad)


def _reduce(x, axis, combine, finish):
    n = x.shape[axis]
    unit = 8 * (SUBLANES if axis == 0 else LANES)
    full = n // unit
    if full < 2:
        return finish(x, axis=axis, keepdims=True)
    cut = (lambda i, j: x[i:j, :]) if axis == 0 else (lambda i, j: x[:, i:j])
    acc = cut(0, unit)
    for i in range(1, full):
        acc = combine(acc, cut(i * unit, (i + 1) * unit))
    out = finish(acc, axis=axis, keepdims=True)
    if n % unit:
        out = combine(out, finish(cut(full * unit, n), axis=axis, keepdims=True))
    return out


def _count(mask, axis):
    return _reduce(jnp.where(mask, 1.0, 0.0), axis, jnp.add, jnp.sum)


def _select_bias(sc_ref, lo_ref, n_keys, key_idx, topk, axis):
    sl = (slice(0, n_keys), slice(None)) if axis == 0 else (slice(None), slice(0, n_keys))
    vec = (1, sc_ref.shape[1]) if axis == 0 else (sc_ref.shape[0], 1)
    kf = float(topk)

    def bit_step(i, prefix):
        cand = prefix | (jnp.int32(1) << (31 - i))
        cnt = _count(sc_ref[sl] >= _ordinal_to_float(cand), axis)
        return jnp.where(cnt >= kf, cand, prefix)

    prefix = lax.fori_loop(0, 32, bit_step, jnp.zeros(vec, I32))
    thr = _ordinal_to_float(prefix)
    sc = sc_ref[sl]
    few = jnp.logical_not(_count(sc >= thr, axis) >= kf)
    thr = jnp.where(few, -jnp.inf, thr)
    need = kf - _count(sc > thr, axis)
    excess = (_count(sc >= thr, axis) > kf) & jnp.logical_not(few)
    lo_ref[...] = jnp.full(vec, INT_MAX, I32)

    @pl.when(jnp.max(jnp.where(excess, 1.0, 0.0)) > 0.0)
    def _():
        nbits = max(1, (n_keys - 1).bit_length())

        def idx_step(i, lo):
            cand = lo + (jnp.int32(1) << (nbits - 1 - i))
            cnt = _count((sc_ref[sl] == thr) & (key_idx < cand), axis)
            return jnp.where(cnt < need, cand, lo)

        lo_ref[...] = lax.fori_loop(0, nbits, idx_step, jnp.zeros(vec, I32))

    lo = jnp.where(few, -1, lo_ref[...])
    sel = (sc > thr) | ((sc == thr) & (key_idx <= lo))
    return jnp.where(sel, 0.0, NEG_BIG)


def _att_prompt_body(n_keys, topk, q_ref, kb_ref, vt_ref, ikb_ref, iq_ref, iwt_ref, o_ref,
                     sc_ref, bias_ref, lo_ref):
    qb = q_ref.shape[0]
    t0 = pl.program_id(0) * qb
    ik = ikb_ref[0:n_keys, :]
    iwt = iwt_ref[...]
    score = jnp.zeros((n_keys, qb), F32)
    for h in range(IDX_HEADS):
        s = _dot_nt(ik, iq_ref[h])
        score = score + jnp.maximum(s, 0.0) * iwt[h:h + 1, :]
    key_idx = lax.broadcasted_iota(I32, (n_keys, qb), 0)
    q_pos = t0 + lax.broadcasted_iota(I32, (n_keys, qb), 1)
    sc_ref[0:n_keys, :] = jnp.where(key_idx <= q_pos, score, -jnp.inf)
    bias_ref[0:n_keys, :] = _select_bias(sc_ref, lo_ref, n_keys, key_idx, topk, axis=0)
    heads = range(N_HEADS)
    hd = [slice(h * HEAD_DIM, (h + 1) * HEAD_DIM) for h in heads]
    gd = [slice((h // GROUP) * HEAD_DIM, (h // GROUP + 1) * HEAD_DIM) for h in heads]
    s = [_dot_nt(kb_ref[0:n_keys, gd[h]], q_ref[:, hd[h]]) + bias_ref[0:n_keys, :] for h in heads]
    m = [_reduce(s[h], 0, jnp.maximum, jnp.max) for h in heads]
    p = [jnp.exp(s[h] - m[h]) for h in heads]
    l = [_reduce(p[h], 0, jnp.add, jnp.sum) for h in heads]
    ot = [_dot(vt_ref[gd[h], 0:n_keys], p[h].astype(BF16)) for h in heads]
    for h in heads:
        o_ref[:, hd[h]] = (ot[h] / l[h]).T


def _att_prompt_kernel(q_ref, kb_ref, vt_ref, ikb_ref, iq_ref, iwt_ref, sg_ref, w_ref, x_ref, y_ref,
                       o_scr, sc_ref, bias_ref, lo_ref, *, cls_len, n_cls, topk):
    qb = q_ref.shape[0]
    cls = (pl.program_id(0) * qb) // cls_len
    for c in range(n_cls):
        pl.when(cls == c)(functools.partial(
            _att_prompt_body, cls_len * (c + 1), topk, q_ref, kb_ref, vt_ref, ikb_ref, iq_ref, iwt_ref,
            o_scr, sc_ref, bias_ref, lo_ref))
    a = (o_scr[...] * sg_ref[...].astype(F32)).astype(BF16)
    y_ref[...] = x_ref[...] + _dot(a, w_ref[...])


def _att_prompt(q, kb, vt, ikb, iq, iwt, sg, w_out, x, batch, seq, qb=LANES):
    nb = seq // qb
    n_cls = min(8, nb)
    assert seq % (n_cls * qb) == 0
    topk = min(TOPK_MAX, seq // 4)
    kern = functools.partial(_att_prompt_kernel, cls_len=seq // n_cls, n_cls=n_cls, topk=topk)
    return pl.pallas_call(
        kern,
        grid=(nb, batch),
        in_specs=[
            pl.BlockSpec((qb, D_MODEL), lambda j, b: (b * nb + j, 0)),
            pl.BlockSpec((seq, KV_W), lambda j, b: (b, 0)),
            pl.BlockSpec((KV_W, seq), lambda j, b: (0, b)),
            pl.BlockSpec((seq, LANES), lambda j, b: (b, 0)),
            pl.BlockSpec((IDX_HEADS, qb, LANES), lambda j, b: (0, b * nb + j, 0)),
            pl.BlockSpec((IDX_HEADS, qb), lambda j, b: (0, b * nb + j)),
            pl.BlockSpec((qb, D_MODEL), lambda j, b: (b * nb + j, 0)),
            pl.BlockSpec((D_MODEL, D_MODEL), lambda j, b: (0, 0)),
            pl.BlockSpec((qb, D_MODEL), lambda j, b: (b * nb + j, 0)),
        ],
        out_specs=pl.BlockSpec((qb, D_MODEL), lambda j, b: (b * nb + j, 0)),
        out_shape=jax.ShapeDtypeStruct((batch * seq, D_MODEL), F32),
        scratch_shapes=[
            pltpu.VMEM((qb, D_MODEL), F32),
            pltpu.VMEM((seq, qb), F32),
            pltpu.VMEM((seq, qb), F32),
            pltpu.VMEM((1, qb), I32),
        ],
        compiler_params=pltpu.CompilerParams(dimension_semantics=("parallel", "arbitrary"),
                                             vmem_limit_bytes=VMEM_LIMIT),
        name="att_prompt",
    )(q, kb, vt, ikb, iq, iwt, sg, w_out, x)


def _att_sample_kernel(pt_ref, q_ref, iq_ref, ikw_ref, ikn_ref, kn_ref, vn_ref,
                       cik_hbm, ck_hbm, cv_hbm, o_ref,
                       ikbuf, kbuf, vbuf, sc_ref, bias_ref, lo_ref, sem_i, sem_kv,
                       *, n_pages, chunk_pages, topk):
    b = pl.program_id(0)
    t_new = q_ref.shape[0]
    past = n_pages * PAGE_SIZE
    n_keys = past + PAGE_SIZE
    n_chunks = n_pages // chunk_pages
    ck = chunk_pages * PAGE_SIZE

    n_seq = pl.num_programs(0)
    islot = b % 2
    ahead = n_chunks - 1

    def idx_copy(seq, slot, p):
        return pltpu.make_async_copy(cik_hbm.at[pt_ref[seq, p]],
                                     ikbuf.at[slot, :, pl.ds(p * PAGE_SIZE, PAGE_SIZE)], sem_i.at[slot])

    def kv_copies(seq, c, p):
        page = pt_ref[seq, c * chunk_pages + p]
        dst = pl.ds(p * PAGE_SIZE * N_KV_HEADS, PAGE_SIZE * N_KV_HEADS)
        return (pltpu.make_async_copy(ck_hbm.at[page], kbuf.at[c, dst], sem_kv.at[0, c]),
                pltpu.make_async_copy(cv_hbm.at[page], vbuf.at[c, dst], sem_kv.at[1, c]))

    def start_idx(seq, slot):
        def go(p, carry):
            idx_copy(seq, slot, p).start()
            return carry
        lax.fori_loop(0, n_pages, go, 0, unroll=8)

    def wait_idx(seq, slot):
        def go(p, carry):
            idx_copy(seq, slot, p).wait()
            return carry
        lax.fori_loop(0, n_pages, go, 0, unroll=8)

    def start_chunk(seq, c):
        def go(p, carry):
            for cp in kv_copies(seq, c, p):
                cp.start()
            return carry
        lax.fori_loop(0, chunk_pages, go, 0, unroll=8)

    def wait_chunk(seq, c):
        def go(p, carry):
            for cp in kv_copies(seq, c, p):
                cp.wait()
            return carry
        lax.fori_loop(0, chunk_pages, go, 0, unroll=8)

    @pl.when(b == 0)
    def _():
        start_idx(b, islot)
        for c in range(ahead):
            start_chunk(b, c)

    @pl.when(b + 1 < n_seq)
    def _():
        start_idx(b + 1, 1 - islot)

    wait_idx(b, islot)

    iq2 = iq_ref[...].reshape(IDX_HEADS * t_new, LANES).astype(BF16)
    ikw = ikw_ref[...]

    def scores(ik_rows):
        s = _dot_nt(iq2, ik_rows)
        acc = jnp.zeros((t_new, ik_rows.shape[0]), F32)
        for h in range(IDX_HEADS):
            acc = acc + jnp.maximum(s[h * t_new:(h + 1) * t_new, :], 0.0) * ikw[:, IDX_DIM + h:IDX_DIM + h + 1]
        return acc

    iq2_past = iq2[:, 0:IDX_DIM]
    for c in range(n_chunks):
        s = _dot(iq2_past, ikbuf[islot, :, c * ck:(c + 1) * ck].astype(BF16))
        acc = jnp.zeros((t_new, ck), F32)
        for h in range(IDX_HEADS):
            acc = acc + jnp.maximum(s[h * t_new:(h + 1) * t_new, :], 0.0) * ikw[:, IDX_DIM + h:IDX_DIM + h + 1]
        sc_ref[:, c * ck:(c + 1) * ck] = acc
    new_idx = lax.broadcasted_iota(I32, (t_new, PAGE_SIZE), 1)
    tok = lax.broadcasted_iota(I32, (t_new, PAGE_SIZE), 0)
    sc_ref[:, past:n_keys] = jnp.where(new_idx <= tok, scores(ikn_ref[...]), -jnp.inf)
    key_idx = lax.broadcasted_iota(I32, (t_new, n_keys), 1)
    bias_ref[...] = _select_bias(sc_ref, lo_ref, n_keys, key_idx, topk, axis=1)

    rows = GROUP * t_new
    qs = [jnp.concatenate([q_ref[:, (g * GROUP + r) * HEAD_DIM:(g * GROUP + r + 1) * HEAD_DIM]
                           for r in range(GROUP)], axis=0).astype(BF16) for g in range(N_KV_HEADS)]
    m = [jnp.full((rows, 1), NEG_BIG, F32) for _ in range(N_KV_HEADS)]
    l = [jnp.zeros((rows, 1), F32) for _ in range(N_KV_HEADS)]
    acc = [jnp.zeros((rows, HEAD_DIM), F32) for _ in range(N_KV_HEADS)]

    def attend(g, k_rows, v_rows, bias):
        s = _dot_nt(qs[g], k_rows) + jnp.concatenate([bias] * GROUP, axis=0)
        m_new = jnp.maximum(m[g], _reduce(s, 1, jnp.maximum, jnp.max))
        alpha = jnp.exp(m[g] - m_new)
        p = jnp.exp(s - m_new)
        l[g] = alpha * l[g] + _reduce(p, 1, jnp.add, jnp.sum)
        acc[g] = alpha * acc[g] + _dot(p.astype(BF16), v_rows)
        m[g] = m_new

    for g in range(N_KV_HEADS):
        new_rows = pl.ds(g, PAGE_SIZE, stride=N_KV_HEADS)
        attend(g, kn_ref[new_rows, :].astype(BF16), vn_ref[new_rows, :].astype(BF16), bias_ref[:, past:n_keys])
    for c in range(n_chunks):
        nxt = c + ahead
        if nxt < n_chunks:
            start_chunk(b, nxt)
        else:
            pl.when(b + 1 < n_seq)(functools.partial(start_chunk, b + 1, nxt - n_chunks))
        wait_chunk(b, c)
        for g in range(N_KV_HEADS):
            head_rows = pl.ds(g, ck, stride=N_KV_HEADS)
            attend(g, kbuf[c, head_rows, :].astype(BF16), vbuf[c, head_rows, :].astype(BF16),
                   bias_ref[:, c * ck:(c + 1) * ck])
    for g in range(N_KV_HEADS):
        og = acc[g] / l[g]
        for r in range(GROUP):
            h = g * GROUP + r
            o_ref[:, h * HEAD_DIM:(h + 1) * HEAD_DIM] = og[r * t_new:(r + 1) * t_new, :]


def _att_sample(page_table, q, iq, ikw, ikn_pad, kn_pad, vn_pad, cache_ik, cache_k, cache_v, t_new):
    nseq, n_pages = page_table.shape
    assert t_new == SUBLANES
    chunk_pages = min(32, n_pages // 2)
    assert n_pages % chunk_pages == 0
    past = n_pages * PAGE_SIZE
    n_keys = past + PAGE_SIZE
    topk = min(TOPK_MAX, (past + t_new) // 4)
    ck = chunk_pages * PAGE_SIZE
    kern = functools.partial(_att_sample_kernel, n_pages=n_pages, chunk_pages=chunk_pages, topk=topk)
    grid_spec = pltpu.PrefetchScalarGridSpec(
        num_scalar_prefetch=1,
        grid=(nseq,),
        in_specs=[
            pl.BlockSpec((t_new, D_MODEL), lambda b, pt: (b, 0)),
            pl.BlockSpec((IDX_HEADS, t_new, LANES), lambda b, pt: (0, b, 0)),
            pl.BlockSpec((t_new, LANES), lambda b, pt: (b, 0)),
            pl.BlockSpec((None, PAGE_SIZE, LANES), lambda b, pt: (b, 0, 0)),
            pl.BlockSpec((None, PAGE_SIZE * N_KV_HEADS, HEAD_DIM), lambda b, pt: (b, 0, 0)),
            pl.BlockSpec((None, PAGE_SIZE * N_KV_HEADS, HEAD_DIM), lambda b, pt: (b, 0, 0)),
            pl.BlockSpec(memory_space=pl.ANY),
            pl.BlockSpec(memory_space=pl.ANY),
            pl.BlockSpec(memory_space=pl.ANY),
        ],
        out_specs=pl.BlockSpec((t_new, D_MODEL), lambda b, pt: (b, 0)),
        scratch_shapes=[
            pltpu.VMEM((2, IDX_DIM, past), F32),
            pltpu.VMEM((n_pages // chunk_pages, ck * N_KV_HEADS, HEAD_DIM), F32),
            pltpu.VMEM((n_pages // chunk_pages, ck * N_KV_HEADS, HEAD_DIM), F32),
            pltpu.VMEM((t_new, n_keys), F32),
            pltpu.VMEM((t_new, n_keys), F32),
            pltpu.VMEM((t_new, 1), I32),
            pltpu.SemaphoreType.DMA((2,)),
            pltpu.SemaphoreType.DMA((2, n_pages // chunk_pages)),
        ],
    )
    return pl.pallas_call(
        kern,
        grid_spec=grid_spec,
        out_shape=jax.ShapeDtypeStruct((nseq * t_new, D_MODEL), F32),
        compiler_params=pltpu.CompilerParams(dimension_semantics=("arbitrary",), vmem_limit_bytes=VMEM_LIMIT),
        name="att_sample",
    )(page_table, q, iq, ikw, ikn_pad, kn_pad, vn_pad, cache_ik, cache_k, cache_v)


def _out_proj_kernel(o_ref, sg_ref, w_ref, x_ref, y_ref):
    a = (o_ref[...] * sg_ref[...].astype(F32)).astype(BF16)
    y_ref[...] = x_ref[...] + _dot(a, w_ref[...])


def _out_proj(o, sg, w_out, x, tm):
    n = o.shape[0]
    row = pl.BlockSpec((tm, D_MODEL), lambda i: (i, 0))
    return pl.pallas_call(
        _out_proj_kernel,
        grid=(n // tm,),
        in_specs=[row, row, pl.BlockSpec((D_MODEL, D_MODEL), lambda i: (0, 0)), row],
        out_specs=row,
        out_shape=jax.ShapeDtypeStruct((n, D_MODEL), F32),
        compiler_params=pltpu.CompilerParams(dimension_semantics=("parallel",), vmem_limit_bytes=VMEM_LIMIT),
        name="out_proj",
    )(o, sg, w_out, x)


def _dn_proj_kernel(x_ref, g_ref, w_ref, cw_ref, buf_ref, alog_ref, dtb_ref,
                    q_ref, k_ref, v_ref, sz_ref, gb_ref, cbuf_ref, xp_ref, *, n_seq, pad):
    tm = x_ref.shape[0]

    @pl.when(pl.program_id(1) == 0)
    def _():
        xp_ref[0:pad, :] = buf_ref[...]

    xn = (_rms(x_ref[...], D_MODEL) * g_ref[...]).astype(BF16)
    p = _dot(xn, w_ref[...])
    xp_ref[pad:pad + tm, :] = p[:, 0:CONV_DIM]
    conv = xp_ref[pad:pad + tm, :] * cw_ref[CONV_W - 1:CONV_W, :]
    for i in range(1, CONV_W):
        conv = conv + xp_ref[pad - i * n_seq:pad - i * n_seq + tm, :] * cw_ref[CONV_W - 1 - i:CONV_W - i, :]
    act = conv * jax.nn.sigmoid(conv)
    for h in range(DN_HEADS):
        sl = slice(h * DN_DK, (h + 1) * DN_DK)
        qh = act[:, sl]
        q_ref[:, sl] = qh * (lax.rsqrt(jnp.sum(qh * qh, axis=-1, keepdims=True) + EPS) * (DN_DK ** -0.5))
        kh = act[:, DN_HEADS * DN_DK + h * DN_DK:DN_HEADS * DN_DK + (h + 1) * DN_DK]
        k_ref[:, sl] = kh * lax.rsqrt(jnp.sum(kh * kh, axis=-1, keepdims=True) + EPS)
    v_ref[...] = act[:, 2 * DN_HEADS * DN_DK:CONV_DIM]
    z = p[:, N_Z:N_BA]
    sz_ref[...] = (z * jax.nn.sigmoid(z)).astype(BF16)
    ba = p[:, N_BA:N_END]
    sp_in = ba + dtb_ref[...]
    softplus = jnp.maximum(sp_in, 0.0) + jnp.log1p(jnp.exp(-jnp.abs(sp_in)))
    lane = lax.broadcasted_iota(I32, (1, LANES), 1)
    gb_ref[...] = jnp.where(lane < DN_HEADS, jax.nn.sigmoid(ba), -jnp.exp(alog_ref[...]) * softplus)
    cbuf_ref[...] = xp_ref[tm:tm + pad, :]
    xp_ref[0:pad, :] = xp_ref[tm:tm + pad, :]


def _dn_proj(x, norm_g, w_packed, conv_w, buf, alog_pad, dtb_pad, n_groups, n_seq, tm):
    n = x.shape[0]
    pad = buf.shape[1]
    tiles = n // n_groups // tm
    assert n == n_groups * tiles * tm and pad % SUBLANES == 0 and pad >= (CONV_W - 1) * n_seq
    row = lambda s, i: (s * tiles + i, 0)
    const = lambda s, i: (0, 0)
    big = lambda d: pl.BlockSpec((tm, d), row)
    return pl.pallas_call(
        functools.partial(_dn_proj_kernel, n_seq=n_seq, pad=pad),
        grid=(n_groups, tiles),
        in_specs=[
            big(D_MODEL),
            pl.BlockSpec((1, D_MODEL), const),
            pl.BlockSpec((D_MODEL, N_END), const),
            pl.BlockSpec((CONV_W, CONV_DIM), const),
            pl.BlockSpec((None, pad, CONV_DIM), lambda s, i: (s, 0, 0)),
            pl.BlockSpec((1, LANES), const),
            pl.BlockSpec((1, LANES), const),
        ],
        out_specs=(big(D_MODEL), big(D_MODEL), big(D_MODEL), big(D_MODEL), big(LANES),
                   pl.BlockSpec((None, pad, CONV_DIM), lambda s, i: (s, 0, 0))),
        out_shape=(
            jax.ShapeDtypeStruct((n, D_MODEL), F32),
            jax.ShapeDtypeStruct((n, D_MODEL), F32),
            jax.ShapeDtypeStruct((n, D_MODEL), F32),
            jax.ShapeDtypeStruct((n, D_MODEL), BF16),
            jax.ShapeDtypeStruct((n, LANES), F32),
            jax.ShapeDtypeStruct((n_groups, pad, CONV_DIM), F32),
        ),
        scratch_shapes=[pltpu.VMEM((pad + tm, CONV_DIM), F32)],
        compiler_params=pltpu.CompilerParams(dimension_semantics=("parallel", "arbitrary"),
                                             vmem_limit_bytes=VMEM_LIMIT),
        name="dn_proj",
    )(x, norm_g, w_packed, conv_w, buf, alog_pad, dtb_pad)


def _dn_rec_kernel(q_ref, k_ref, v_ref, gb_ref, s0_ref, sz_ref, og_ref, w_ref, x_ref, y_ref, sout_ref,
                   s_scr, a_scr, *, c_len):
    n_sub = q_ref.shape[0] // c_len

    @pl.when(pl.program_id(1) == 0)
    def _():
        s_scr[...] = s0_ref[...]

    ri = lax.broadcasted_iota(I32, (c_len, c_len), 0)
    ci = lax.broadcasted_iota(I32, (c_len, c_len), 1)
    incl = ri >= ci
    strict = ri > ci
    tril = jnp.where(incl, 1.0, 0.0)
    eye_c = jnp.where(ri == ci, 1.0, 0.0)
    lane = lax.broadcasted_iota(I32, (c_len, LANES), 1)
    n_sq = max(0, (c_len - 1).bit_length() - 1)

    tile_aligned = c_len % LANES == 0

    subs = range(n_sub)
    rows = [slice(s * c_len, (s + 1) * c_len) for s in subs]
    gbc = [gb_ref[rows[s], :] for s in subs]
    gcum = [_dot(tril, gbc[s], HI) for s in subs]
    gcum_t = [gcum[s].T if tile_aligned else None for s in subs]
    units = [(s, h) for s in subs for h in range(DN_HEADS)]
    n_u = range(len(units))
    at = [(rows[s], slice(h * DN_DK, (h + 1) * DN_DK)) for s, h in units]
    beta = [gbc[s][:, h:h + 1] for s, h in units]
    gcol = [gcum[s][:, DN_HEADS + h:DN_HEADS + h + 1] for s, h in units]
    if tile_aligned:
        grow = [gcum_t[s][DN_HEADS + h:DN_HEADS + h + 1, :] for s, h in units]
    else:
        grow = [_dot_nt(jnp.where(lane == DN_HEADS + h, 1.0, 0.0), gcum[s], HI) for s, h in units]
    glast = [gcol[i][c_len - 1:c_len, :] for i in n_u]
    decay = [jnp.where(incl, jnp.exp(jnp.where(incl, gcol[i] - grow[i], 0.0)), 0.0) for i in n_u]
    eg = [jnp.exp(gcol[i]) for i in n_u]
    a_mat = [beta[i] * _dot_nt(k_ref[at[i]], k_ref[at[i]]) * jnp.where(strict, decay[i], 0.0) for i in n_u]
    pw = [-a_mat[i] for i in n_u]
    t_inv = [eye_c - a_mat[i] for i in n_u]
    for level in range(n_sq):
        if level < NEUMANN_SPLIT_LEVELS:
            ps = [_split_bf16(pw[i]) for i in n_u]
            pw = [_dot3(ps[i], ps[i]) for i in n_u]
            ps = [_split_bf16(pw[i]) for i in n_u]
            t_inv = [t_inv[i] + _dot3(_split_bf16(t_inv[i]), ps[i]) for i in n_u]
        else:
            pw = [_dot(pw[i].astype(BF16), pw[i].astype(BF16)) for i in n_u]
            t_inv = [t_inv[i] + _dot(t_inv[i].astype(BF16), pw[i].astype(BF16)) for i in n_u]
    wu = [_dot(t_inv[i].astype(BF16),
               jnp.concatenate([beta[i] * eg[i] * k_ref[at[i]], beta[i] * v_ref[at[i]]], axis=1).astype(BF16))
          for i in n_u]
    qk = [_dot_nt(q_ref[at[i]], k_ref[at[i]]) * decay[i] for i in n_u]
    kd = [k_ref[at[i]] * jnp.exp(glast[i] - gcol[i]) for i in n_u]
    kdt = [kd[i].T if tile_aligned else None for i in n_u]
    for i, (s, h) in enumerate(units):
        s_old = s_scr[h]
        u = wu[i][:, DN_DK:DN_DK + DN_DV] - _dot(wu[i][:, 0:DN_DK], s_old)
        o = eg[i] * _dot(q_ref[at[i]], s_old) + _dot(qk[i], u)
        a_scr[at[i]] = _rms(o, DN_DV) * og_ref[...] * sz_ref[at[i]].astype(F32)
        s_scr[h] = jnp.exp(glast[i]) * s_old + (_dot(kdt[i], u) if tile_aligned else _dot_tn(kd[i], u))
    y_ref[...] = x_ref[...] + _dot(a_scr[...].astype(BF16), w_ref[...])

    @pl.when(pl.program_id(1) == pl.num_programs(1) - 1)
    def _():
        sout_ref[...] = s_scr[...]


def _dn_rec(q, k, v, gb, s0, sz, o_gain, w_out, x, n_seq, t_len, chunk, chunks_per_step):
    step = chunk * chunks_per_step
    assert t_len % step == 0
    n_steps = t_len // step
    row = lambda b, c: (b * n_steps + c, 0)
    const = lambda b, c: (0, 0)
    wide = pl.BlockSpec((step, D_MODEL), row)
    st = pl.BlockSpec((None, None, DN_HEADS, DN_DK, DN_DV), lambda b, c: (b, 0, 0, 0, 0))
    return pl.pallas_call(
        functools.partial(_dn_rec_kernel, c_len=chunk),
        grid=(n_seq, n_steps),
        in_specs=[wide, wide, wide, pl.BlockSpec((step, LANES), row), st, wide,
                  pl.BlockSpec((1, DN_DV), const), pl.BlockSpec((D_MODEL, D_MODEL), const), wide],
        out_specs=(wide, st),
        out_shape=(jax.ShapeDtypeStruct((n_seq * t_len, D_MODEL), F32),
                   jax.ShapeDtypeStruct((n_seq, 1, DN_HEADS, DN_DK, DN_DV), F32)),
        scratch_shapes=[pltpu.VMEM((DN_HEADS, DN_DK, DN_DV), F32), pltpu.VMEM((step, D_MODEL), F32)],
        compiler_params=pltpu.CompilerParams(dimension_semantics=("parallel", "arbitrary"),
                                             vmem_limit_bytes=VMEM_LIMIT),
        name="dn_rec",
    )(q, k, v, gb, s0, sz, o_gain, w_out, x)


def _pad_lanes(v, offset=0):
    return jnp.zeros((1, LANES), F32).at[0, offset:offset + v.shape[0]].set(v)


def _split_w(w, main, tail):
    small = jnp.pad(w[:, main:tail], ((0, 0), (0, LANES - (tail - main))))
    return w[:, :main].astype(BF16), small.astype(BF16), w[:, tail:].astype(BF16)


def kernel(x_prompt, x_sample, cache_k, cache_v, cache_idx_k, state_dn_S, state_dn_conv, page_table,
           att_norm, att_w_in, att_q_gain, att_k_gain, att_ik_gain, att_w_out,
           dn_norm, dn_w_in, dn_conv_w, dn_A_log, dn_dt_bias, dn_o_gain, dn_w_out):
    bp, seq, d = x_prompt.shape
    bs, t_new, _ = x_sample.shape
    n_pool = cache_k.shape[0]
    np_rows, ns_rows = bp * seq, bs * t_new
    tm, tm_dn = 512, 256

    xp = x_prompt.reshape(np_rows, d)
    xs = x_sample.reshape(ns_rows, d)

    w_att = _split_w(att_w_in[0], A_IKW, A_IKW + IDX_DIM + IDX_HEADS)
    a_norm = att_norm[0][None, :]
    qg, kg = att_q_gain[0][None, :], att_k_gain[0][None, :]
    ikg = _pad_lanes(att_ik_gain[0])
    w_ao = att_w_out[0].astype(BF16)

    q_p, k_p, kb_p, v_p, vt_p, iq_p, ikw_p, ikb_p, iwt_p, sg_p = _att_proj(xp, a_norm, *w_att, qg, kg, ikg, tm)
    y1_p = _att_prompt(q_p, kb_p, vt_p, ikb_p, iq_p, iwt_p, sg_p, w_ao, xp, bp, seq)

    q_s, k_s, _, v_s, _, iq_s, ikw_s, ikb_s, _, sg_s = _att_proj(xs, a_norm, *w_att, qg, kg, ikg, min(tm, ns_rows))
    pad_new = lambda a, per_tok: jnp.pad(a.reshape(bs, t_new * per_tok, a.shape[-1]),
                                         ((0, 0), (0, (PAGE_SIZE - t_new) * per_tok), (0, 0)))
    o_s = _att_sample(page_table, q_s.astype(F32), iq_s.astype(F32), ikw_s, pad_new(ikb_s, 1),
                      pad_new(k_s, N_KV_HEADS), pad_new(v_s, N_KV_HEADS),
                      jnp.transpose(cache_idx_k, (0, 2, 3, 1)).reshape(n_pool, IDX_DIM, PAGE_SIZE),
                      cache_k.reshape(n_pool, PAGE_SIZE * N_KV_HEADS, HEAD_DIM),
                      cache_v.reshape(n_pool, PAGE_SIZE * N_KV_HEADS, HEAD_DIM), t_new)
    y1_s = _out_proj(o_s, sg_s, w_ao, xs, min(tm, ns_rows))

    w_dn = jnp.pad(dn_w_in[0], ((0, 0), (0, N_END - dn_w_in.shape[-1]))).astype(BF16)
    d_norm = dn_norm[0][None, :]
    alog = _pad_lanes(dn_A_log[0], DN_HEADS)
    dtb = _pad_lanes(dn_dt_bias[0], DN_HEADS)
    o_gain = dn_o_gain[0][None, :]
    w_do = dn_w_out[0].astype(BF16)
    hist = CONV_W - 1

    buf_p = jnp.zeros((bp, SUBLANES, CONV_DIM), F32)
    dq_p, dk_p, dv_p, sz_p, gb_p, cb_p = _dn_proj(y1_p, d_norm, w_dn, dn_conv_w[0], buf_p, alog, dtb,
                                                   n_groups=bp, n_seq=1, tm=tm_dn)
    s0_p = jnp.zeros((bp, 1, DN_HEADS, DN_DK, DN_DV), F32)
    y2_p, s_p = _dn_rec(dq_p, dk_p, dv_p, gb_p, s0_p, sz_p, o_gain, w_do, y1_p, bp, seq, chunk=LANES,
                        chunks_per_step=4)

    to_tm = lambda a: a.reshape(bs, t_new, -1).transpose(1, 0, 2).reshape(ns_rows, -1)
    to_bm = lambda a: a.reshape(t_new, bs, -1).transpose(1, 0, 2).reshape(ns_rows, -1)
    buf_s = state_dn_conv[:, 0].transpose(1, 0, 2).reshape(1, hist * bs, CONV_DIM)
    dq_s, dk_s, dv_s, sz_s, gb_s, cb_s = _dn_proj(to_tm(y1_s), d_norm, w_dn, dn_conv_w[0], buf_s, alog, dtb,
                                                   n_groups=1, n_seq=bs, tm=ns_rows)
    y2_s, s_s = _dn_rec(to_bm(dq_s), to_bm(dk_s), to_bm(dv_s), to_bm(gb_s), state_dn_S,
                        to_bm(sz_s.astype(F32)), o_gain, w_do, y1_s, bs, t_new, chunk=t_new,
                        chunks_per_step=1)

    return (
        y2_p.reshape(bp, seq, d),
        y2_s.reshape(bs, t_new, d),
        k_p.reshape(bp, seq, 1, N_KV_HEADS, HEAD_DIM),
        v_p.reshape(bp, seq, 1, N_KV_HEADS, HEAD_DIM),
        ikw_p[:, :IDX_DIM].reshape(bp, seq, 1, IDX_DIM),
        k_s.reshape(bs, t_new, 1, N_KV_HEADS, HEAD_DIM),
        v_s.reshape(bs, t_new, 1, N_KV_HEADS, HEAD_DIM),
        ikw_s[:, :IDX_DIM].reshape(bs, t_new, 1, IDX_DIM),
        s_p,
        cb_p[:, SUBLANES - hist:, :].reshape(bp, 1, hist, CONV_DIM),
        s_s,
        cb_s.reshape(hist, bs, CONV_DIM).transpose(1, 0, 2).reshape(bs, 1, hist, CONV_DIM),
    )
```

```python
import functools

import jax
import jax.numpy as jnp
from jax import lax
from jax.experimental import pallas as pl
from jax.experimental.pallas import tpu as pltpu

F32 = jnp.float32
BF16 = jnp.bfloat16
I32 = jnp.int32

EPS = 1e-6
LANES = 128
SUBLANES = 8
VMEM_LIMIT = 56 * 1024 * 1024

D_MODEL = 1024
N_HEADS = 8
HEAD_DIM = 128
N_KV_HEADS = 2
GROUP = N_HEADS // N_KV_HEADS
IDX_HEADS = 8
IDX_DIM = 64
TOPK_MAX = 256
PAGE_SIZE = 128
DN_HEADS = 8
DN_DK = 128
DN_DV = 128
CONV_W = 4
KV_W = N_KV_HEADS * HEAD_DIM
CONV_DIM = 3 * DN_HEADS * DN_DK

A_Q, A_K, A_V, A_IQ, A_IKW = 0, 1024, 1280, 1536, 2048
N_Z, N_BA, N_END = 3072, 4096, 4224

INT_MIN = -2147483648
INT_MAX = 2147483647
NEG_BIG = -1e30
NT_DIMS = (((1,), (1,)), ((), ()))
HI = lax.Precision.HIGHEST
NEUMANN_SPLIT_LEVELS = 4
W_CAST_ROWS = 64


def _dot(a, b, precision=None):
    return jnp.dot(a, b, preferred_element_type=F32, precision=precision)


def _dot_nt(a, b, precision=None):
    return lax.dot_general(a, b, NT_DIMS, preferred_element_type=F32, precision=precision)


def _dot_tn(a, b):
    return lax.dot_general(a, b, (((0,), (0,)), ((), ())), preferred_element_type=F32)


def _split_bf16(x):
    hi = x.astype(BF16)
    return hi, (x - hi.astype(F32)).astype(BF16)


def _dot3(a, b):
    (ah, al), (bh, bl) = a, b
    return _dot(ah, bh) + (_dot(ah, bl) + _dot(al, bh))


def _rms(x, n):
    return x * lax.rsqrt(jnp.sum(x * x, axis=-1, keepdims=True) * (1.0 / n) + EPS)


def _ordinal_to_float(u):
    key = u ^ jnp.int32(INT_MIN)
    return pltpu.bitcast(jnp.where(key < 0, key ^ jnp.int32(INT_MAX), key), F32)


def _att_proj_kernel(x_ref, g_ref, w_ref, wikw_ref, wg_ref, qg_ref, kg_ref, ikg_ref,
                     q_ref, k_ref, kb_ref, v_ref, vt_ref, iq_ref, ikw_ref, ikb_ref, iwt_ref, sg_ref):
    tm = x_ref.shape[0]
    x = x_ref[...]
    xn = (_rms(x, D_MODEL) * g_ref[...]).astype(BF16)
    p = _dot(xn, w_ref[...])
    q_gain = qg_ref[...] * (HEAD_DIM ** -0.5)
    for h in range(N_HEADS):
        qh = _rms(p[:, A_Q + h * HEAD_DIM:A_Q + (h + 1) * HEAD_DIM], HEAD_DIM) * q_gain
        q_ref[:, h * HEAD_DIM:(h + 1) * HEAD_DIM] = qh.astype(BF16)
    v = p[:, A_V:A_V + KV_W]
    for g in range(N_KV_HEADS):
        kh = _rms(p[:, A_K + g * HEAD_DIM:A_K + (g + 1) * HEAD_DIM], HEAD_DIM) * kg_ref[...]
        k_ref[pl.ds(g, tm, stride=N_KV_HEADS), :] = kh
        kb_ref[:, g * HEAD_DIM:(g + 1) * HEAD_DIM] = kh.astype(BF16)
        v_ref[pl.ds(g, tm, stride=N_KV_HEADS), :] = v[:, g * HEAD_DIM:(g + 1) * HEAD_DIM]
    vt_ref[...] = v.T.astype(BF16)
    lane = lax.broadcasted_iota(I32, (1, LANES), 1)
    is_ik = lane < IDX_DIM
    for h in range(IDX_HEADS):
        iqh = p[:, A_IQ + (h // 2) * LANES:A_IQ + (h // 2 + 1) * LANES]
        if h % 2:
            iqh = pltpu.roll(iqh, IDX_DIM, axis=1)
        iq_ref[h] = jnp.where(is_ik, iqh, 0.0).astype(BF16)
    ikw = _dot(xn, wikw_ref[...])
    ik = jnp.where(is_ik, ikw, 0.0)
    ikn = _rms(ik, IDX_DIM) * ikg_ref[...]
    out = jnp.where(is_ik, ikn, ikw * (IDX_HEADS ** -0.5 * IDX_DIM ** -0.5))
    ikw_ref[...] = out
    ikb_ref[...] = ikn.astype(BF16)
    iwt_ref[...] = out.T[IDX_DIM:IDX_DIM + IDX_HEADS, :]
    gate = _dot(xn, wg_ref[...])
    sg_ref[...] = (gate * jax.nn.sigmoid(gate)).astype(BF16)


def _att_proj(x, norm_g, w_main, w_ikw, w_gate, q_gain, k_gain, ik_gain_pad, tm):
    n = x.shape[0]
    assert n % tm == 0 and tm % LANES == 0
    row = lambda i: (i, 0)
    const = lambda i: (0, 0)
    out_shape = (
        jax.ShapeDtypeStruct((n, D_MODEL), BF16),
        jax.ShapeDtypeStruct((N_KV_HEADS * n, HEAD_DIM), F32),
        jax.ShapeDtypeStruct((n, KV_W), BF16),
        jax.ShapeDtypeStruct((N_KV_HEADS * n, HEAD_DIM), F32),
        jax.ShapeDtypeStruct((KV_W, n), BF16),
        jax.ShapeDtypeStruct((IDX_HEADS, n, LANES), BF16),
        jax.ShapeDtypeStruct((n, LANES), F32),
        jax.ShapeDtypeStruct((n, LANES), BF16),
        jax.ShapeDtypeStruct((IDX_HEADS, n), F32),
        jax.ShapeDtypeStruct((n, D_MODEL), BF16),
    )
    out_specs = (
        pl.BlockSpec((tm, D_MODEL), row),
        pl.BlockSpec((N_KV_HEADS * tm, HEAD_DIM), row),
        pl.BlockSpec((tm, KV_W), row),
        pl.BlockSpec((N_KV_HEADS * tm, HEAD_DIM), row),
        pl.BlockSpec((KV_W, tm), lambda i: (0, i)),
        pl.BlockSpec((IDX_HEADS, tm, LANES), lambda i: (0, i, 0)),
        pl.BlockSpec((tm, LANES), row),
        pl.BlockSpec((tm, LANES), row),
        pl.BlockSpec((IDX_HEADS, tm), lambda i: (0, i)),
        pl.BlockSpec((tm, D_MODEL), row),
    )
    return pl.pallas_call(
        _att_proj_kernel,
        grid=(n // tm,),
        in_specs=[
            pl.BlockSpec((tm, D_MODEL), row),
            pl.BlockSpec((1, D_MODEL), const),
            pl.BlockSpec((D_MODEL, A_IKW), const),
            pl.BlockSpec((D_MODEL, LANES), const),
            pl.BlockSpec((D_MODEL, D_MODEL), const),
            pl.BlockSpec((1, HEAD_DIM), const),
            pl.BlockSpec((1, HEAD_DIM), const),
            pl.BlockSpec((1, LANES), const),
        ],
        out_specs=out_specs,
        out_shape=out_shape,
        compiler_params=pltpu.CompilerParams(dimension_semantics=("parallel",), vmem_limit_bytes=VMEM_LIMIT),
        name="att_proj",
    )(x, norm_g, w_main, w_ikw, w_gate, q_gain, k_gain, ik_gain_pad)


def _reduce(x, axis, combine, finish):
    n = x.shape[axis]
    unit = 8 * (SUBLANES if axis == 0 else LANES)
    full = n // unit
    if full < 2:
        return finish(x, axis=axis, keepdims=True)
    cut = (lambda i, j: x[i:j, :]) if axis == 0 else (lambda i, j: x[:, i:j])
    acc = cut(0, unit)
    for i in range(1, full):
        acc = combine(acc, cut(i * unit, (i + 1) * unit))
    out = finish(acc, axis=axis, keepdims=True)
    if n % unit:
        out = combine(out, finish(cut(full * unit, n), axis=axis, keepdims=True))
    return out


def _count(mask, axis):
    return _reduce(jnp.where(mask, 1.0, 0.0), axis, jnp.add, jnp.sum)


def _select_bias(sc_ref, lo_ref, n_keys, key_idx, topk, axis):
    sl = (slice(0, n_keys), slice(None)) if axis == 0 else (slice(None), slice(0, n_keys))
    vec = (1, sc_ref.shape[1]) if axis == 0 else (sc_ref.shape[0], 1)
    kf = float(topk)

    def bit_step(i, prefix):
        cand = prefix | (jnp.int32(1) << (31 - i))
        cnt = _count(sc_ref[sl] >= _ordinal_to_float(cand), axis)
        return jnp.where(cnt >= kf, cand, prefix)

    prefix = lax.fori_loop(0, 32, bit_step, jnp.zeros(vec, I32))
    thr = _ordinal_to_float(prefix)
    sc = sc_ref[sl]
    few = jnp.logical_not(_count(sc >= thr, axis) >= kf)
    thr = jnp.where(few, -jnp.inf, thr)
    need = kf - _count(sc > thr, axis)
    excess = (_count(sc >= thr, axis) > kf) & jnp.logical_not(few)
    lo_ref[...] = jnp.full(vec, INT_MAX, I32)

    @pl.when(jnp.max(jnp.where(excess, 1.0, 0.0)) > 0.0)
    def _():
        nbits = max(1, (n_keys - 1).bit_length())

        def idx_step(i, lo):
            cand = lo + (jnp.int32(1) << (nbits - 1 - i))
            cnt = _count((sc_ref[sl] == thr) & (key_idx < cand), axis)
            return jnp.where(cnt < need, cand, lo)

        lo_ref[...] = lax.fori_loop(0, nbits, idx_step, jnp.zeros(vec, I32))

    lo = jnp.where(few, -1, lo_ref[...])
    sel = (sc > thr) | ((sc == thr) & (key_idx <= lo))
    return jnp.where(sel, 0.0, NEG_BIG)


def _att_prompt_body(n_keys, topk, q_ref, kb_ref, vt_ref, ikb_ref, iq_ref, iwt_ref, o_ref,
                     sc_ref, bias_ref, lo_ref):
    qb = q_ref.shape[0]
    t0 = pl.program_id(0) * qb
    ik = ikb_ref[0:n_keys, :]
    iwt = iwt_ref[...]
    score = jnp.zeros((n_keys, qb), F32)
    for h in range(IDX_HEADS):
        s = _dot_nt(ik, iq_ref[h])
        score = score + jnp.maximum(s, 0.0) * iwt[h:h + 1, :]
    key_idx = lax.broadcasted_iota(I32, (n_keys, qb), 0)
    q_pos = t0 + lax.broadcasted_iota(I32, (n_keys, qb), 1)
    sc_ref[0:n_keys, :] = jnp.where(key_idx <= q_pos, score, -jnp.inf)
    bias_ref[0:n_keys, :] = _select_bias(sc_ref, lo_ref, n_keys, key_idx, topk, axis=0)
    heads = range(N_HEADS)
    hd = [slice(h * HEAD_DIM, (h + 1) * HEAD_DIM) for h in heads]
    gd = [slice((h // GROUP) * HEAD_DIM, (h // GROUP + 1) * HEAD_DIM) for h in heads]
    s = [_dot_nt(kb_ref[0:n_keys, gd[h]], q_ref[:, hd[h]]) + bias_ref[0:n_keys, :] for h in heads]
    m = [_reduce(s[h], 0, jnp.maximum, jnp.max) for h in heads]
    p = [jnp.exp(s[h] - m[h]) for h in heads]
    l = [_reduce(p[h], 0, jnp.add, jnp.sum) for h in heads]
    ot = [_dot(vt_ref[gd[h], 0:n_keys], p[h].astype(BF16)) for h in heads]
    for h in heads:
        o_ref[:, hd[h]] = (ot[h] / l[h]).T


def _att_prompt_kernel(q_ref, kb_ref, vt_ref, ikb_ref, iq_ref, iwt_ref, sg_ref, w_ref, x_ref, y_ref,
                       o_scr, sc_ref, bias_ref, lo_ref, *, cls_len, n_cls, topk):
    qb = q_ref.shape[0]
    cls = (pl.program_id(0) * qb) // cls_len
    for c in range(n_cls):
        pl.when(cls == c)(functools.partial(
            _att_prompt_body, cls_len * (c + 1), topk, q_ref, kb_ref, vt_ref, ikb_ref, iq_ref, iwt_ref,
            o_scr, sc_ref, bias_ref, lo_ref))
    a = (o_scr[...] * sg_ref[...].astype(F32)).astype(BF16)
    y_ref[...] = x_ref[...] + _dot(a, w_ref[...])


def _att_prompt(q, kb, vt, ikb, iq, iwt, sg, w_out, x, batch, seq, qb=LANES):
    nb = seq // qb
    n_cls = min(8, nb)
    assert seq % (n_cls * qb) == 0
    topk = min(TOPK_MAX, seq // 4)
    kern = functools.partial(_att_prompt_kernel, cls_len=seq // n_cls, n_cls=n_cls, topk=topk)
    return pl.pallas_call(
        kern,
        grid=(nb, batch),
        in_specs=[
            pl.BlockSpec((qb, D_MODEL), lambda j, b: (b * nb + j, 0)),
            pl.BlockSpec((seq, KV_W), lambda j, b: (b, 0)),
            pl.BlockSpec((KV_W, seq), lambda j, b: (0, b)),
            pl.BlockSpec((seq, LANES), lambda j, b: (b, 0)),
            pl.BlockSpec((IDX_HEADS, qb, LANES), lambda j, b: (0, b * nb + j, 0)),
            pl.BlockSpec((IDX_HEADS, qb), lambda j, b: (0, b * nb + j)),
            pl.BlockSpec((qb, D_MODEL), lambda j, b: (b * nb + j, 0)),
            pl.BlockSpec((D_MODEL, D_MODEL), lambda j, b: (0, 0)),
            pl.BlockSpec((qb, D_MODEL), lambda j, b: (b * nb + j, 0)),
        ],
        out_specs=pl.BlockSpec((qb, D_MODEL), lambda j, b: (b * nb + j, 0)),
        out_shape=jax.ShapeDtypeStruct((batch * seq, D_MODEL), F32),
        scratch_shapes=[
            pltpu.VMEM((qb, D_MODEL), F32),
            pltpu.VMEM((seq, qb), F32),
            pltpu.VMEM((seq, qb), F32),
            pltpu.VMEM((1, qb), I32),
        ],
        compiler_params=pltpu.CompilerParams(dimension_semantics=("parallel", "arbitrary"),
                                             vmem_limit_bytes=VMEM_LIMIT),
        name="att_prompt",
    )(q, kb, vt, ikb, iq, iwt, sg, w_out, x)


def _att_sample_kernel(pt_ref, q_ref, iq_ref, ikw_ref, ikn_ref, kn_ref, vn_ref,
                       cik_hbm, ck_hbm, cv_hbm, o_ref,
                       ikbuf, kbuf, vbuf, sc_ref, bias_ref, lo_ref, sem_i, sem_kv,
                       *, n_pages, chunk_pages, topk):
    b = pl.program_id(0)
    t_new = q_ref.shape[0]
    past = n_pages * PAGE_SIZE
    n_keys = past + PAGE_SIZE
    n_chunks = n_pages // chunk_pages
    ck = chunk_pages * PAGE_SIZE

    n_seq = pl.num_programs(0)
    islot = b % 2
    ahead = n_chunks - 1

    def idx_copy(seq, slot, p):
        return pltpu.make_async_copy(cik_hbm.at[pt_ref[seq, p]],
                                     ikbuf.at[slot, :, pl.ds(p * PAGE_SIZE, PAGE_SIZE)], sem_i.at[slot])

    def kv_copies(seq, c, p):
        page = pt_ref[seq, c * chunk_pages + p]
        dst = pl.ds(p * PAGE_SIZE * N_KV_HEADS, PAGE_SIZE * N_KV_HEADS)
        return (pltpu.make_async_copy(ck_hbm.at[page], kbuf.at[c, dst], sem_kv.at[0, c]),
                pltpu.make_async_copy(cv_hbm.at[page], vbuf.at[c, dst], sem_kv.at[1, c]))

    def start_idx(seq, slot):
        def go(p, carry):
            idx_copy(seq, slot, p).start()
            return carry
        lax.fori_loop(0, n_pages, go, 0, unroll=8)

    def wait_idx(seq, slot):
        def go(p, carry):
            idx_copy(seq, slot, p).wait()
            return carry
        lax.fori_loop(0, n_pages, go, 0, unroll=8)

    def start_chunk(seq, c):
        def go(p, carry):
            for cp in kv_copies(seq, c, p):
                cp.start()
            return carry
        lax.fori_loop(0, chunk_pages, go, 0, unroll=8)

    def wait_chunk(seq, c):
        def go(p, carry):
            for cp in kv_copies(seq, c, p):
                cp.wait()
            return carry
        lax.fori_loop(0, chunk_pages, go, 0, unroll=8)

    @pl.when(b == 0)
    def _():
        start_idx(b, islot)
        for c in range(ahead):
            start_chunk(b, c)

    @pl.when(b + 1 < n_seq)
    def _():
        start_idx(b + 1, 1 - islot)

    wait_idx(b, islot)

    iq2 = iq_ref[...].reshape(IDX_HEADS * t_new, LANES).astype(BF16)
    ikw = ikw_ref[...]

    def scores(ik_rows):
        s = _dot_nt(iq2, ik_rows)
        acc = jnp.zeros((t_new, ik_rows.shape[0]), F32)
        for h in range(IDX_HEADS):
            acc = acc + jnp.maximum(s[h * t_new:(h + 1) * t_new, :], 0.0) * ikw[:, IDX_DIM + h:IDX_DIM + h + 1]
        return acc

    iq2_past = iq2[:, 0:IDX_DIM]
    for c in range(n_chunks):
        s = _dot(iq2_past, ikbuf[islot, :, c * ck:(c + 1) * ck].astype(BF16))
        acc = jnp.zeros((t_new, ck), F32)
        for h in range(IDX_HEADS):
            acc = acc + jnp.maximum(s[h * t_new:(h + 1) * t_new, :], 0.0) * ikw[:, IDX_DIM + h:IDX_DIM + h + 1]
        sc_ref[:, c * ck:(c + 1) * ck] = acc
    new_idx = lax.broadcasted_iota(I32, (t_new, PAGE_SIZE), 1)
    tok = lax.broadcasted_iota(I32, (t_new, PAGE_SIZE), 0)
    sc_ref[:, past:n_keys] = jnp.where(new_idx <= tok, scores(ikn_ref[...]), -jnp.inf)
    key_idx = lax.broadcasted_iota(I32, (t_new, n_keys), 1)
    bias_ref[...] = _select_bias(sc_ref, lo_ref, n_keys, key_idx, topk, axis=1)

    rows = GROUP * t_new
    qs = [jnp.concatenate([q_ref[:, (g * GROUP + r) * HEAD_DIM:(g * GROUP + r + 1) * HEAD_DIM]
                           for r in range(GROUP)], axis=0).astype(BF16) for g in range(N_KV_HEADS)]
    m = [jnp.full((rows, 1), NEG_BIG, F32) for _ in range(N_KV_HEADS)]
    l = [jnp.zeros((rows, 1), F32) for _ in range(N_KV_HEADS)]
    acc = [jnp.zeros((rows, HEAD_DIM), F32) for _ in range(N_KV_HEADS)]

    def attend(g, k_rows, v_rows, bias):
        s = _dot_nt(qs[g], k_rows) + jnp.concatenate([bias] * GROUP, axis=0)
        m_new = jnp.maximum(m[g], _reduce(s, 1, jnp.maximum, jnp.max))
        alpha = jnp.exp(m[g] - m_new)
        p = jnp.exp(s - m_new)
        l[g] = alpha * l[g] + _reduce(p, 1, jnp.add, jnp.sum)
        acc[g] = alpha * acc[g] + _dot(p.astype(BF16), v_rows)
        m[g] = m_new

    for g in range(N_KV_HEADS):
        new_rows = pl.ds(g, PAGE_SIZE, stride=N_KV_HEADS)
        attend(g, kn_ref[new_rows, :].astype(BF16), vn_ref[new_rows, :].astype(BF16), bias_ref[:, past:n_keys])
    for c in range(n_chunks):
        nxt = c + ahead
        if nxt < n_chunks:
            start_chunk(b, nxt)
        else:
            pl.when(b + 1 < n_seq)(functools.partial(start_chunk, b + 1, nxt - n_chunks))
        wait_chunk(b, c)
        for g in range(N_KV_HEADS):
            head_rows = pl.ds(g, ck, stride=N_KV_HEADS)
            attend(g, kbuf[c, head_rows, :].astype(BF16), vbuf[c, head_rows, :].astype(BF16),
                   bias_ref[:, c * ck:(c + 1) * ck])
    for g in range(N_KV_HEADS):
        og = acc[g] / l[g]
        for r in range(GROUP):
            h = g * GROUP + r
            o_ref[:, h * HEAD_DIM:(h + 1) * HEAD_DIM] = og[r * t_new:(r + 1) * t_new, :]


def _att_sample(page_table, q, iq, ikw, ikn_pad, kn_pad, vn_pad, cache_ik, cache_k, cache_v, t_new):
    nseq, n_pages = page_table.shape
    assert t_new == SUBLANES
    chunk_pages = min(32, n_pages // 2)
    assert n_pages % chunk_pages == 0
    past = n_pages * PAGE_SIZE
    n_keys = past + PAGE_SIZE
    topk = min(TOPK_MAX, (past + t_new) // 4)
    ck = chunk_pages * PAGE_SIZE
    kern = functools.partial(_att_sample_kernel, n_pages=n_pages, chunk_pages=chunk_pages, topk=topk)
    grid_spec = pltpu.PrefetchScalarGridSpec(
        num_scalar_prefetch=1,
        grid=(nseq,),
        in_specs=[
            pl.BlockSpec((t_new, D_MODEL), lambda b, pt: (b, 0)),
            pl.BlockSpec((IDX_HEADS, t_new, LANES), lambda b, pt: (0, b, 0)),
            pl.BlockSpec((t_new, LANES), lambda b, pt: (b, 0)),
            pl.BlockSpec((None, PAGE_SIZE, LANES), lambda b, pt: (b, 0, 0)),
            pl.BlockSpec((None, PAGE_SIZE * N_KV_HEADS, HEAD_DIM), lambda b, pt: (b, 0, 0)),
            pl.BlockSpec((None, PAGE_SIZE * N_KV_HEADS, HEAD_DIM), lambda b, pt: (b, 0, 0)),
            pl.BlockSpec(memory_space=pl.ANY),
            pl.BlockSpec(memory_space=pl.ANY),
            pl.BlockSpec(memory_space=pl.ANY),
        ],
        out_specs=pl.BlockSpec((t_new, D_MODEL), lambda b, pt: (b, 0)),
        scratch_shapes=[
            pltpu.VMEM((2, IDX_DIM, past), F32),
            pltpu.VMEM((n_pages // chunk_pages, ck * N_KV_HEADS, HEAD_DIM), F32),
            pltpu.VMEM((n_pages // chunk_pages, ck * N_KV_HEADS, HEAD_DIM), F32),
            pltpu.VMEM((t_new, n_keys), F32),
            pltpu.VMEM((t_new, n_keys), F32),
            pltpu.VMEM((t_new, 1), I32),
            pltpu.SemaphoreType.DMA((2,)),
            pltpu.SemaphoreType.DMA((2, n_pages // chunk_pages)),
        ],
    )
    return pl.pallas_call(
        kern,
        grid_spec=grid_spec,
        out_shape=jax.ShapeDtypeStruct((nseq * t_new, D_MODEL), F32),
        compiler_params=pltpu.CompilerParams(dimension_semantics=("arbitrary",), vmem_limit_bytes=VMEM_LIMIT),
        name="att_sample",
    )(page_table, q, iq, ikw, ikn_pad, kn_pad, vn_pad, cache_ik, cache_k, cache_v)


def _out_proj_kernel(o_ref, sg_ref, w_ref, x_ref, y_ref):
    a = (o_ref[...] * sg_ref[...].astype(F32)).astype(BF16)
    y_ref[...] = x_ref[...] + _dot(a, w_ref[...])


def _out_proj(o, sg, w_out, x, tm):
    n = o.shape[0]
    row = pl.BlockSpec((tm, D_MODEL), lambda i: (i, 0))
    return pl.pallas_call(
        _out_proj_kernel,
        grid=(n // tm,),
        in_specs=[row, row, pl.BlockSpec((D_MODEL, D_MODEL), lambda i: (0, 0)), row],
        out_specs=row,
        out_shape=jax.ShapeDtypeStruct((n, D_MODEL), F32),
        compiler_params=pltpu.CompilerParams(dimension_semantics=("parallel",), vmem_limit_bytes=VMEM_LIMIT),
        name="out_proj",
    )(o, sg, w_out, x)


def _dn_proj_kernel(x_ref, g_ref, w_ref, cw_ref, buf_ref, alog_ref, dtb_ref,
                    q_ref, k_ref, v_ref, sz_ref, gb_ref, cbuf_ref, xp_ref, wq_ref, *, n_seq, pad):
    tm = x_ref.shape[0]
    n_in = w_ref.shape[1]

    @pl.when((pl.program_id(0) == 0) & (pl.program_id(1) == 0))
    def _():
        def cast_rows(r, carry):
            rows = pl.ds(pl.multiple_of(r * W_CAST_ROWS, W_CAST_ROWS), W_CAST_ROWS)
            wq_ref[rows, 0:N_BA] = w_ref[rows, 0:N_BA].astype(BF16)
            tail = jnp.concatenate([w_ref[rows, N_BA:n_in], jnp.zeros((W_CAST_ROWS, N_END - n_in), F32)], axis=1)
            wq_ref[rows, N_BA:N_END] = tail.astype(BF16)
            return carry
        lax.fori_loop(0, D_MODEL // W_CAST_ROWS, cast_rows, 0)

    @pl.when(pl.program_id(1) == 0)
    def _():
        xp_ref[0:pad, :] = buf_ref[...]

    xn = (_rms(x_ref[...], D_MODEL) * g_ref[...]).astype(BF16)
    p = _dot(xn, wq_ref[...])
    xp_ref[pad:pad + tm, :] = p[:, 0:CONV_DIM]
    conv = xp_ref[pad:pad + tm, :] * cw_ref[CONV_W - 1:CONV_W, :]
    for i in range(1, CONV_W):
        conv = conv + xp_ref[pad - i * n_seq:pad - i * n_seq + tm, :] * cw_ref[CONV_W - 1 - i:CONV_W - i, :]
    act = conv * jax.nn.sigmoid(conv)
    for h in range(DN_HEADS):
        sl = slice(h * DN_DK, (h + 1) * DN_DK)
        qh = act[:, sl]
        q_ref[:, sl] = qh * (lax.rsqrt(jnp.sum(qh * qh, axis=-1, keepdims=True) + EPS) * (DN_DK ** -0.5))
        kh = act[:, DN_HEADS * DN_DK + h * DN_DK:DN_HEADS * DN_DK + (h + 1) * DN_DK]
        k_ref[:, sl] = kh * lax.rsqrt(jnp.sum(kh * kh, axis=-1, keepdims=True) + EPS)
    v_ref[...] = act[:, 2 * DN_HEADS * DN_DK:CONV_DIM]
    z = p[:, N_Z:N_BA]
    sz_ref[...] = (z * jax.nn.sigmoid(z)).astype(BF16)
    ba = p[:, N_BA:N_END]
    sp_in = ba + dtb_ref[...]
    softplus = jnp.maximum(sp_in, 0.0) + jnp.log1p(jnp.exp(-jnp.abs(sp_in)))
    lane = lax.broadcasted_iota(I32, (1, LANES), 1)
    gb_ref[...] = jnp.where(lane < DN_HEADS, jax.nn.sigmoid(ba), -jnp.exp(alog_ref[...]) * softplus)
    cbuf_ref[...] = xp_ref[tm:tm + pad, :]
    xp_ref[0:pad, :] = xp_ref[tm:tm + pad, :]


def _dn_proj(x, norm_g, w_f32, conv_w, buf, alog_pad, dtb_pad, n_groups, n_seq, tm):
    n = x.shape[0]
    pad = buf.shape[1]
    tiles = n // n_groups // tm
    assert n == n_groups * tiles * tm and pad % SUBLANES == 0 and pad >= (CONV_W - 1) * n_seq
    row = lambda s, i: (s * tiles + i, 0)
    const = lambda s, i: (0, 0)
    big = lambda d: pl.BlockSpec((tm, d), row)
    return pl.pallas_call(
        functools.partial(_dn_proj_kernel, n_seq=n_seq, pad=pad),
        grid=(n_groups, tiles),
        in_specs=[
            big(D_MODEL),
            pl.BlockSpec((1, D_MODEL), const),
            pl.BlockSpec((None,) + w_f32.shape[1:], lambda s, i: (0, 0, 0), pipeline_mode=pl.Buffered(1)),
            pl.BlockSpec((CONV_W, CONV_DIM), const),
            pl.BlockSpec((None, pad, CONV_DIM), lambda s, i: (s, 0, 0)),
            pl.BlockSpec((1, LANES), const),
            pl.BlockSpec((1, LANES), const),
        ],
        out_specs=(big(D_MODEL), big(D_MODEL), big(D_MODEL), big(D_MODEL), big(LANES),
                   pl.BlockSpec((None, pad, CONV_DIM), lambda s, i: (s, 0, 0))),
        out_shape=(
            jax.ShapeDtypeStruct((n, D_MODEL), F32),
            jax.ShapeDtypeStruct((n, D_MODEL), F32),
            jax.ShapeDtypeStruct((n, D_MODEL), F32),
            jax.ShapeDtypeStruct((n, D_MODEL), BF16),
            jax.ShapeDtypeStruct((n, LANES), F32),
            jax.ShapeDtypeStruct((n_groups, pad, CONV_DIM), F32),
        ),
        scratch_shapes=[pltpu.VMEM((pad + tm, CONV_DIM), F32), pltpu.VMEM((D_MODEL, N_END), BF16)],
        compiler_params=pltpu.CompilerParams(dimension_semantics=("arbitrary", "arbitrary"),
                                             vmem_limit_bytes=VMEM_LIMIT),
        name="dn_proj",
    )(x, norm_g, w_f32, conv_w, buf, alog_pad, dtb_pad)


def _dn_rec_kernel(q_ref, k_ref, v_ref, gb_ref, s0_ref, sz_ref, og_ref, w_ref, x_ref, y_ref, sout_ref,
                   s_scr, a_scr, *, c_len):
    n_sub = q_ref.shape[0] // c_len

    @pl.when(pl.program_id(1) == 0)
    def _():
        s_scr[...] = s0_ref[...]

    ri = lax.broadcasted_iota(I32, (c_len, c_len), 0)
    ci = lax.broadcasted_iota(I32, (c_len, c_len), 1)
    incl = ri >= ci
    strict = ri > ci
    tril = jnp.where(incl, 1.0, 0.0)
    eye_c = jnp.where(ri == ci, 1.0, 0.0)
    lane = lax.broadcasted_iota(I32, (c_len, LANES), 1)
    n_sq = max(0, (c_len - 1).bit_length() - 1)

    tile_aligned = c_len % LANES == 0

    subs = range(n_sub)
    rows = [slice(s * c_len, (s + 1) * c_len) for s in subs]
    gbc = [gb_ref[rows[s], :] for s in subs]
    gcum = [_dot(tril, gbc[s], HI) for s in subs]
    gcum_t = [gcum[s].T if tile_aligned else None for s in subs]
    units = [(s, h) for s in subs for h in range(DN_HEADS)]
    n_u = range(len(units))
    at = [(rows[s], slice(h * DN_DK, (h + 1) * DN_DK)) for s, h in units]
    beta = [gbc[s][:, h:h + 1] for s, h in units]
    gcol = [gcum[s][:, DN_HEADS + h:DN_HEADS + h + 1] for s, h in units]
    if tile_aligned:
        grow = [gcum_t[s][DN_HEADS + h:DN_HEADS + h + 1, :] for s, h in units]
    else:
        grow = [_dot_nt(jnp.where(lane == DN_HEADS + h, 1.0, 0.0), gcum[s], HI) for s, h in units]
    glast = [gcol[i][c_len - 1:c_len, :] for i in n_u]
    decay = [jnp.where(incl, jnp.exp(jnp.where(incl, gcol[i] - grow[i], 0.0)), 0.0) for i in n_u]
    eg = [jnp.exp(gcol[i]) for i in n_u]
    a_mat = [beta[i] * _dot_nt(k_ref[at[i]], k_ref[at[i]]) * jnp.where(strict, decay[i], 0.0) for i in n_u]
    pw = [-a_mat[i] for i in n_u]
    t_inv = [eye_c - a_mat[i] for i in n_u]
    for level in range(n_sq):
        if level < NEUMANN_SPLIT_LEVELS:
            ps = [_split_bf16(pw[i]) for i in n_u]
            pw = [_dot3(ps[i], ps[i]) for i in n_u]
            ps = [_split_bf16(pw[i]) for i in n_u]
            t_inv = [t_inv[i] + _dot3(_split_bf16(t_inv[i]), ps[i]) for i in n_u]
        else:
            pw = [_dot(pw[i].astype(BF16), pw[i].astype(BF16)) for i in n_u]
            t_inv = [t_inv[i] + _dot(t_inv[i].astype(BF16), pw[i].astype(BF16)) for i in n_u]
    wu = [_dot(t_inv[i].astype(BF16),
               jnp.concatenate([beta[i] * eg[i] * k_ref[at[i]], beta[i] * v_ref[at[i]]], axis=1).astype(BF16))
          for i in n_u]
    qk = [_dot_nt(q_ref[at[i]], k_ref[at[i]]) * decay[i] for i in n_u]
    kd = [k_ref[at[i]] * jnp.exp(glast[i] - gcol[i]) for i in n_u]
    kdt = [kd[i].T if tile_aligned else None for i in n_u]
    for i, (s, h) in enumerate(units):
        s_old = s_scr[h]
        u = wu[i][:, DN_DK:DN_DK + DN_DV] - _dot(wu[i][:, 0:DN_DK], s_old)
        o = eg[i] * _dot(q_ref[at[i]], s_old) + _dot(qk[i], u)
        a_scr[at[i]] = _rms(o, DN_DV) * og_ref[...] * sz_ref[at[i]].astype(F32)
        s_scr[h] = jnp.exp(glast[i]) * s_old + (_dot(kdt[i], u) if tile_aligned else _dot_tn(kd[i], u))
    y_ref[...] = x_ref[...] + _dot(a_scr[...].astype(BF16), w_ref[...])

    @pl.when(pl.program_id(1) == pl.num_programs(1) - 1)
    def _():
        sout_ref[...] = s_scr[...]


def _dn_rec(q, k, v, gb, s0, sz, o_gain, w_out, x, n_seq, t_len, chunk, chunks_per_step):
    step = chunk * chunks_per_step
    assert t_len % step == 0
    n_steps = t_len // step
    row = lambda b, c: (b * n_steps + c, 0)
    const = lambda b, c: (0, 0)
    wide = pl.BlockSpec((step, D_MODEL), row)
    st = pl.BlockSpec((None, None, DN_HEADS, DN_DK, DN_DV), lambda b, c: (b, 0, 0, 0, 0))
    return pl.pallas_call(
        functools.partial(_dn_rec_kernel, c_len=chunk),
        grid=(n_seq, n_steps),
        in_specs=[wide, wide, wide, pl.BlockSpec((step, LANES), row), st, wide,
                  pl.BlockSpec((1, DN_DV), const), pl.BlockSpec((D_MODEL, D_MODEL), const), wide],
        out_specs=(wide, st),
        out_shape=(jax.ShapeDtypeStruct((n_seq * t_len, D_MODEL), F32),
                   jax.ShapeDtypeStruct((n_seq, 1, DN_HEADS, DN_DK, DN_DV), F32)),
        scratch_shapes=[pltpu.VMEM((DN_HEADS, DN_DK, DN_DV), F32), pltpu.VMEM((step, D_MODEL), F32)],
        compiler_params=pltpu.CompilerParams(dimension_semantics=("parallel", "arbitrary"),
                                             vmem_limit_bytes=VMEM_LIMIT),
        name="dn_rec",
    )(q, k, v, gb, s0, sz, o_gain, w_out, x)


def _pad_lanes(v, offset=0):
    return jnp.zeros((1, LANES), F32).at[0, offset:offset + v.shape[0]].set(v)


def _split_w(w, main, tail):
    small = jnp.pad(w[:, main:tail], ((0, 0), (0, LANES - (tail - main))))
    return w[:, :main].astype(BF16), small.astype(BF16), w[:, tail:].astype(BF16)


def kernel(x_prompt, x_sample, cache_k, cache_v, cache_idx_k, state_dn_S, state_dn_conv, page_table,
           att_norm, att_w_in, att_q_gain, att_k_gain, att_ik_gain, att_w_out,
           dn_norm, dn_w_in, dn_conv_w, dn_A_log, dn_dt_bias, dn_o_gain, dn_w_out):
    bp, seq, d = x_prompt.shape
    bs, t_new, _ = x_sample.shape
    n_pool = cache_k.shape[0]
    np_rows, ns_rows = bp * seq, bs * t_new
    tm, tm_dn = 512, 256

    xp = x_prompt.reshape(np_rows, d)
    xs = x_sample.reshape(ns_rows, d)

    w_att = _split_w(att_w_in[0], A_IKW, A_IKW + IDX_DIM + IDX_HEADS)
    a_norm = att_norm[0][None, :]
    qg, kg = att_q_gain[0][None, :], att_k_gain[0][None, :]
    ikg = _pad_lanes(att_ik_gain[0])
    w_ao = att_w_out[0].astype(BF16)

    q_p, k_p, kb_p, v_p, vt_p, iq_p, ikw_p, ikb_p, iwt_p, sg_p = _att_proj(xp, a_norm, *w_att, qg, kg, ikg, tm)
    y1_p = _att_prompt(q_p, kb_p, vt_p, ikb_p, iq_p, iwt_p, sg_p, w_ao, xp, bp, seq)

    q_s, k_s, _, v_s, _, iq_s, ikw_s, ikb_s, _, sg_s = _att_proj(xs, a_norm, *w_att, qg, kg, ikg, min(tm, ns_rows))
    pad_new = lambda a, per_tok: jnp.pad(a.reshape(bs, t_new * per_tok, a.shape[-1]),
                                         ((0, 0), (0, (PAGE_SIZE - t_new) * per_tok), (0, 0)))
    o_s = _att_sample(page_table, q_s.astype(F32), iq_s.astype(F32), ikw_s, pad_new(ikb_s, 1),
                      pad_new(k_s, N_KV_HEADS), pad_new(v_s, N_KV_HEADS),
                      jnp.transpose(cache_idx_k, (0, 2, 3, 1)).reshape(n_pool, IDX_DIM, PAGE_SIZE),
                      cache_k.reshape(n_pool, PAGE_SIZE * N_KV_HEADS, HEAD_DIM),
                      cache_v.reshape(n_pool, PAGE_SIZE * N_KV_HEADS, HEAD_DIM), t_new)
    y1_s = _out_proj(o_s, sg_s, w_ao, xs, min(tm, ns_rows))

    w_dn = dn_w_in
    d_norm = dn_norm[0][None, :]
    alog = _pad_lanes(dn_A_log[0], DN_HEADS)
    dtb = _pad_lanes(dn_dt_bias[0], DN_HEADS)
    o_gain = dn_o_gain[0][None, :]
    w_do = dn_w_out[0].astype(BF16)
    hist = CONV_W - 1

    buf_p = jnp.zeros((bp, SUBLANES, CONV_DIM), F32)
    dq_p, dk_p, dv_p, sz_p, gb_p, cb_p = _dn_proj(y1_p, d_norm, w_dn, dn_conv_w[0], buf_p, alog, dtb,
                                                   n_groups=bp, n_seq=1, tm=tm_dn)
    s0_p = jnp.zeros((bp, 1, DN_HEADS, DN_DK, DN_DV), F32)
    y2_p, s_p = _dn_rec(dq_p, dk_p, dv_p, gb_p, s0_p, sz_p, o_gain, w_do, y1_p, bp, seq, chunk=LANES,
                        chunks_per_step=4)

    to_tm = lambda a: a.reshape(bs, t_new, -1).transpose(1, 0, 2).reshape(ns_rows, -1)
    to_bm = lambda a: a.reshape(t_new, bs, -1).transpose(1, 0, 2).reshape(ns_rows, -1)
    buf_s = state_dn_conv[:, 0].transpose(1, 0, 2).reshape(1, hist * bs, CONV_DIM)
    dq_s, dk_s, dv_s, sz_s, gb_s, cb_s = _dn_proj(to_tm(y1_s), d_norm, w_dn, dn_conv_w[0], buf_s, alog, dtb,
                                                   n_groups=1, n_seq=bs, tm=ns_rows)
    y2_s, s_s = _dn_rec(to_bm(dq_s), to_bm(dk_s), to_bm(dv_s), to_bm(gb_s), state_dn_S,
                        to_bm(sz_s.astype(F32)), o_gain, w_do, y1_s, bs, t_new, chunk=t_new,
                        chunks_per_step=1)

    return (
        y2_p.reshape(bp, seq, d),
        y2_s.reshape(bs, t_new, d),
        k_p.reshape(bp, seq, 1, N_KV_HEADS, HEAD_DIM),
        v_p.reshape(bp, seq, 1, N_KV_HEADS, HEAD_DIM),
        ikw_p[:, :IDX_DIM].reshape(bp, seq, 1, IDX_DIM),
        k_s.reshape(bs, t_new, 1, N_KV_HEADS, HEAD_DIM),
        v_s.reshape(bs, t_new, 1, N_KV_HEADS, HEAD_DIM),
        ikw_s[:, :IDX_DIM].reshape(bs, t_new, 1, IDX_DIM),
        s_p,
        cb_p[:, SUBLANES - hist:, :].reshape(bp, 1, hist, CONV_DIM),
        s_s,
        cb_s.reshape(hist, bs, CONV_DIM).transpose(1, 0, 2).reshape(bs, 1, hist, CONV_DIM),
    )
```

```python
import functools

import jax
import jax.numpy as jnp
from jax import lax
from jax.experimental import pallas as pl
from jax.experimental.pallas import tpu as pltpu

F32 = jnp.float32
BF16 = jnp.bfloat16
I32 = jnp.int32

EPS = 1e-6
LANES = 128
SUBLANES = 8
VMEM_LIMIT = 56 * 1024 * 1024

D_MODEL = 1024
N_HEADS = 8
HEAD_DIM = 128
N_KV_HEADS = 2
GROUP = N_HEADS // N_KV_HEADS
IDX_HEADS = 8
IDX_DIM = 64
TOPK_MAX = 256
PAGE_SIZE = 128
DN_HEADS = 8
DN_DK = 128
DN_DV = 128
CONV_W = 4
KV_W = N_KV_HEADS * HEAD_DIM
CONV_DIM = 3 * DN_HEADS * DN_DK

A_Q, A_K, A_V, A_IQ, A_IKW = 0, 1024, 1280, 1536, 2048
N_Z, N_BA, N_END = 3072, 4096, 4224

INT_MIN = -2147483648
INT_MAX = 2147483647
NEG_BIG = -1e30
LOG2_E = 1.4426950408889634
NT_DIMS = (((1,), (1,)), ((), ()))
HI = lax.Precision.HIGHEST
NEUMANN_SPLIT_LEVELS = 4


def _dot(a, b, precision=None):
    return jnp.dot(a, b, preferred_element_type=F32, precision=precision)


def _dot_nt(a, b, precision=None):
    return lax.dot_general(a, b, NT_DIMS, preferred_element_type=F32, precision=precision)


def _dot_tn(a, b):
    return lax.dot_general(a, b, (((0,), (0,)), ((), ())), preferred_element_type=F32)


def _split_bf16(x):
    hi = x.astype(BF16)
    return hi, (x - hi.astype(F32)).astype(BF16)


def _dot3(a, b):
    (ah, al), (bh, bl) = a, b
    return _dot(ah, bh) + (_dot(ah, bl) + _dot(al, bh))


def _rms(x, n):
    return x * lax.rsqrt(jnp.sum(x * x, axis=-1, keepdims=True) * (1.0 / n) + EPS)


def _ordinal_to_float(u):
    key = u ^ jnp.int32(INT_MIN)
    return pltpu.bitcast(jnp.where(key < 0, key ^ jnp.int32(INT_MAX), key), F32)


def _att_proj_kernel(x_ref, g_ref, w_ref, wikw_ref, wg_ref, qg_ref, kg_ref, ikg_ref,
                     q_ref, k_ref, kb_ref, v_ref, vt_ref, iq_ref, ikw_ref, ikb_ref, iwt_ref, sg_ref):
    tm = x_ref.shape[0]
    x = x_ref[...]
    xn = (_rms(x, D_MODEL) * g_ref[...]).astype(BF16)
    p = _dot(xn, w_ref[...])
    q_gain = qg_ref[...] * (HEAD_DIM ** -0.5 * LOG2_E)
    for h in range(N_HEADS):
        qh = _rms(p[:, A_Q + h * HEAD_DIM:A_Q + (h + 1) * HEAD_DIM], HEAD_DIM) * q_gain
        q_ref[:, h * HEAD_DIM:(h + 1) * HEAD_DIM] = qh.astype(BF16)
    v = p[:, A_V:A_V + KV_W]
    for g in range(N_KV_HEADS):
        kh = _rms(p[:, A_K + g * HEAD_DIM:A_K + (g + 1) * HEAD_DIM], HEAD_DIM) * kg_ref[...]
        k_ref[pl.ds(g, tm, stride=N_KV_HEADS), :] = kh
        kb_ref[:, g * HEAD_DIM:(g + 1) * HEAD_DIM] = kh.astype(BF16)
        v_ref[pl.ds(g, tm, stride=N_KV_HEADS), :] = v[:, g * HEAD_DIM:(g + 1) * HEAD_DIM]
    vt_ref[...] = v.T.astype(BF16)
    lane = lax.broadcasted_iota(I32, (1, LANES), 1)
    is_ik = lane < IDX_DIM
    for h in range(IDX_HEADS):
        iqh = p[:, A_IQ + (h // 2) * LANES:A_IQ + (h // 2 + 1) * LANES]
        if h % 2:
            iqh = pltpu.roll(iqh, IDX_DIM, axis=1)
        iq_ref[h] = jnp.where(is_ik, iqh, 0.0).astype(BF16)
    ikw = _dot(xn, wikw_ref[...])
    ik = jnp.where(is_ik, ikw, 0.0)
    ikn = _rms(ik, IDX_DIM) * ikg_ref[...]
    out = jnp.where(is_ik, ikn, ikw * (IDX_HEADS ** -0.5 * IDX_DIM ** -0.5))
    ikw_ref[...] = out
    ikb_ref[...] = ikn.astype(BF16)
    iwt_ref[...] = out.T[IDX_DIM:IDX_DIM + IDX_HEADS, :]
    gate = _dot(xn, wg_ref[...])
    sg_ref[...] = (gate * jax.nn.sigmoid(gate)).astype(BF16)


def _att_proj(x, norm_g, w_main, w_ikw, w_gate, q_gain, k_gain, ik_gain_pad, tm):
    n = x.shape[0]
    assert n % tm == 0 and tm % LANES == 0
    row = lambda i: (i, 0)
    const = lambda i: (0, 0)
    out_shape = (
        jax.ShapeDtypeStruct((n, D_MODEL), BF16),
        jax.ShapeDtypeStruct((N_KV_HEADS * n, HEAD_DIM), F32),
        jax.ShapeDtypeStruct((n, KV_W), BF16),
        jax.ShapeDtypeStruct((N_KV_HEADS * n, HEAD_DIM), F32),
        jax.ShapeDtypeStruct((KV_W, n), BF16),
        jax.ShapeDtypeStruct((IDX_HEADS, n, LANES), BF16),
        jax.ShapeDtypeStruct((n, LANES), F32),
        jax.ShapeDtypeStruct((n, LANES), BF16),
        jax.ShapeDtypeStruct((IDX_HEADS, n), F32),
        jax.ShapeDtypeStruct((n, D_MODEL), BF16),
    )
    out_specs = (
        pl.BlockSpec((tm, D_MODEL), row),
        pl.BlockSpec((N_KV_HEADS * tm, HEAD_DIM), row),
        pl.BlockSpec((tm, KV_W), row),
        pl.BlockSpec((N_KV_HEADS * tm, HEAD_DIM), row),
        pl.BlockSpec((KV_W, tm), lambda i: (0, i)),
        pl.BlockSpec((IDX_HEADS, tm, LANES), lambda i: (0, i, 0)),
        pl.BlockSpec((tm, LANES), row),
        pl.BlockSpec((tm, LANES), row),
        pl.BlockSpec((IDX_HEADS, tm), lambda i: (0, i)),
        pl.BlockSpec((tm, D_MODEL), row),
    )
    return pl.pallas_call(
        _att_proj_kernel,
        grid=(n // tm,),
        in_specs=[
            pl.BlockSpec((tm, D_MODEL), row),
            pl.BlockSpec((1, D_MODEL), const),
            pl.BlockSpec((D_MODEL, A_IKW), const),
            pl.BlockSpec((D_MODEL, LANES), const),
            pl.BlockSpec((D_MODEL, D_MODEL), const),
            pl.BlockSpec((1, HEAD_DIM), const),
            pl.BlockSpec((1, HEAD_DIM), const),
            pl.BlockSpec((1, LANES), const),
        ],
        out_specs=out_specs,
        out_shape=out_shape,
        compiler_params=pltpu.CompilerParams(dimension_semantics=("parallel",), vmem_limit_bytes=VMEM_LIMIT),
        name="att_proj",
    )(x, norm_g, w_main, w_ikw, w_gate, q_gain, k_gain, ik_gain_pad)


def _reduce(x, axis, combine, finish):
    n = x.shape[axis]
    unit = 8 * (SUBLANES if axis == 0 else LANES)
    full = n // unit
    if full < 2:
        return finish(x, axis=axis, keepdims=True)
    cut = (lambda i, j: x[i:j, :]) if axis == 0 else (lambda i, j: x[:, i:j])
    acc = cut(0, unit)
    for i in range(1, full):
        acc = combine(acc, cut(i * unit, (i + 1) * unit))
    out = finish(acc, axis=axis, keepdims=True)
    if n % unit:
        out = combine(out, finish(cut(full * unit, n), axis=axis, keepdims=True))
    return out


def _count(mask, axis):
    return _reduce(jnp.where(mask, 1.0, 0.0), axis, jnp.add, jnp.sum)


def _select_bias(sc_ref, lo_ref, n_keys, key_idx, topk, axis):
    sl = (slice(0, n_keys), slice(None)) if axis == 0 else (slice(None), slice(0, n_keys))
    vec = (1, sc_ref.shape[1]) if axis == 0 else (sc_ref.shape[0], 1)
    kf = float(topk)

    def bit_step(i, prefix):
        cand = prefix | (jnp.int32(1) << (31 - i))
        cnt = _count(sc_ref[sl] >= _ordinal_to_float(cand), axis)
        return jnp.where(cnt >= kf, cand, prefix)

    prefix = lax.fori_loop(0, 32, bit_step, jnp.zeros(vec, I32))
    thr = _ordinal_to_float(prefix)
    sc = sc_ref[sl]
    few = jnp.logical_not(_count(sc >= thr, axis) >= kf)
    thr = jnp.where(few, -jnp.inf, thr)
    need = kf - _count(sc > thr, axis)
    excess = (_count(sc >= thr, axis) > kf) & jnp.logical_not(few)
    lo_ref[...] = jnp.full(vec, INT_MAX, I32)

    @pl.when(jnp.max(jnp.where(excess, 1.0, 0.0)) > 0.0)
    def _():
        nbits = max(1, (n_keys - 1).bit_length())

        def idx_step(i, lo):
            cand = lo + (jnp.int32(1) << (nbits - 1 - i))
            cnt = _count((sc_ref[sl] == thr) & (key_idx < cand), axis)
            return jnp.where(cnt < need, cand, lo)

        lo_ref[...] = lax.fori_loop(0, nbits, idx_step, jnp.zeros(vec, I32))

    lo = jnp.where(few, -1, lo_ref[...])
    sel = (sc > thr) | ((sc == thr) & (key_idx <= lo))
    return jnp.where(sel, 0.0, NEG_BIG)


def _att_prompt_body(n_keys, topk, q_ref, kb_ref, vt_ref, ikb_ref, iq_ref, iwt_ref, o_ref,
                     sc_ref, bias_ref, lo_ref):
    qb = q_ref.shape[0]
    t0 = pl.program_id(0) * qb
    ik = ikb_ref[0:n_keys, :]
    iwt = iwt_ref[...]
    score = jnp.zeros((n_keys, qb), F32)
    for h in range(IDX_HEADS):
        s = _dot_nt(ik, iq_ref[h])
        score = score + jnp.maximum(s, 0.0) * iwt[h:h + 1, :]
    key_idx = lax.broadcasted_iota(I32, (n_keys, qb), 0)
    q_pos = t0 + lax.broadcasted_iota(I32, (n_keys, qb), 1)
    sc_ref[0:n_keys, :] = jnp.where(key_idx <= q_pos, score, -jnp.inf)
    bias_ref[0:n_keys, :] = _select_bias(sc_ref, lo_ref, n_keys, key_idx, topk, axis=0)
    heads = range(N_HEADS)
    hd = [slice(h * HEAD_DIM, (h + 1) * HEAD_DIM) for h in heads]
    gd = [slice((h // GROUP) * HEAD_DIM, (h // GROUP + 1) * HEAD_DIM) for h in heads]
    s = [_dot_nt(kb_ref[0:n_keys, gd[h]], q_ref[:, hd[h]]) + bias_ref[0:n_keys, :] for h in heads]
    m = [_reduce(s[h], 0, jnp.maximum, jnp.max) for h in heads]
    p = [jnp.exp2(s[h] - m[h]) for h in heads]
    l = [_reduce(p[h], 0, jnp.add, jnp.sum) for h in heads]
    ot = [_dot(vt_ref[gd[h], 0:n_keys], p[h].astype(BF16)) for h in heads]
    for h in heads:
        o_ref[:, hd[h]] = (ot[h] / l[h]).T


def _att_prompt_kernel(q_ref, kb_ref, vt_ref, ikb_ref, iq_ref, iwt_ref, sg_ref, w_ref, x_ref, y_ref,
                       o_scr, sc_ref, bias_ref, lo_ref, *, cls_len, n_cls, topk):
    qb = q_ref.shape[0]
    cls = (pl.program_id(0) * qb) // cls_len
    for c in range(n_cls):
        pl.when(cls == c)(functools.partial(
            _att_prompt_body, cls_len * (c + 1), topk, q_ref, kb_ref, vt_ref, ikb_ref, iq_ref, iwt_ref,
            o_scr, sc_ref, bias_ref, lo_ref))
    a = (o_scr[...] * sg_ref[...].astype(F32)).astype(BF16)
    y_ref[...] = x_ref[...] + _dot(a, w_ref[...])


def _att_prompt(q, kb, vt, ikb, iq, iwt, sg, w_out, x, batch, seq, qb=LANES):
    nb = seq // qb
    n_cls = min(8, nb)
    assert seq % (n_cls * qb) == 0
    topk = min(TOPK_MAX, seq // 4)
    kern = functools.partial(_att_prompt_kernel, cls_len=seq // n_cls, n_cls=n_cls, topk=topk)
    return pl.pallas_call(
        kern,
        grid=(nb, batch),
        in_specs=[
            pl.BlockSpec((qb, D_MODEL), lambda j, b: (b * nb + j, 0)),
            pl.BlockSpec((seq, KV_W), lambda j, b: (b, 0)),
            pl.BlockSpec((KV_W, seq), lambda j, b: (0, b)),
            pl.BlockSpec((seq, LANES), lambda j, b: (b, 0)),
            pl.BlockSpec((IDX_HEADS, qb, LANES), lambda j, b: (0, b * nb + j, 0)),
            pl.BlockSpec((IDX_HEADS, qb), lambda j, b: (0, b * nb + j)),
            pl.BlockSpec((qb, D_MODEL), lambda j, b: (b * nb + j, 0)),
            pl.BlockSpec((D_MODEL, D_MODEL), lambda j, b: (0, 0)),
            pl.BlockSpec((qb, D_MODEL), lambda j, b: (b * nb + j, 0)),
        ],
        out_specs=pl.BlockSpec((qb, D_MODEL), lambda j, b: (b * nb + j, 0)),
        out_shape=jax.ShapeDtypeStruct((batch * seq, D_MODEL), F32),
        scratch_shapes=[
            pltpu.VMEM((qb, D_MODEL), F32),
            pltpu.VMEM((seq, qb), F32),
            pltpu.VMEM((seq, qb), F32),
            pltpu.VMEM((1, qb), I32),
        ],
        compiler_params=pltpu.CompilerParams(dimension_semantics=("parallel", "arbitrary"),
                                             vmem_limit_bytes=VMEM_LIMIT),
        name="att_prompt",
    )(q, kb, vt, ikb, iq, iwt, sg, w_out, x)


def _att_sample_kernel(pt_ref, q_ref, iq_ref, ikw_ref, ikn_ref, kn_ref, vn_ref,
                       cik_hbm, ck_hbm, cv_hbm, o_ref,
                       ikbuf, kbuf, vbuf, sc_ref, bias_ref, lo_ref, sem_i, sem_kv,
                       *, n_pages, chunk_pages, topk):
    b = pl.program_id(0)
    t_new = q_ref.shape[0]
    past = n_pages * PAGE_SIZE
    n_keys = past + PAGE_SIZE
    n_chunks = n_pages // chunk_pages
    ck = chunk_pages * PAGE_SIZE

    n_seq = pl.num_programs(0)
    islot = b % 2
    ahead = n_chunks - 1

    def idx_copy(seq, slot, p):
        return pltpu.make_async_copy(cik_hbm.at[pt_ref[seq, p]],
                                     ikbuf.at[slot, :, pl.ds(p * PAGE_SIZE, PAGE_SIZE)], sem_i.at[slot])

    def kv_copies(seq, c, p):
        page = pt_ref[seq, c * chunk_pages + p]
        dst = pl.ds(p * PAGE_SIZE * N_KV_HEADS, PAGE_SIZE * N_KV_HEADS)
        return (pltpu.make_async_copy(ck_hbm.at[page], kbuf.at[c, dst], sem_kv.at[0, c]),
                pltpu.make_async_copy(cv_hbm.at[page], vbuf.at[c, dst], sem_kv.at[1, c]))

    def start_idx(seq, slot):
        def go(p, carry):
            idx_copy(seq, slot, p).start()
            return carry
        lax.fori_loop(0, n_pages, go, 0, unroll=8)

    def wait_idx(seq, slot):
        def go(p, carry):
            idx_copy(seq, slot, p).wait()
            return carry
        lax.fori_loop(0, n_pages, go, 0, unroll=8)

    def start_chunk(seq, c):
        def go(p, carry):
            for cp in kv_copies(seq, c, p):
                cp.start()
            return carry
        lax.fori_loop(0, chunk_pages, go, 0, unroll=8)

    def wait_chunk(seq, c):
        def go(p, carry):
            for cp in kv_copies(seq, c, p):
                cp.wait()
            return carry
        lax.fori_loop(0, chunk_pages, go, 0, unroll=8)

    @pl.when(b == 0)
    def _():
        start_idx(b, islot)
        for c in range(ahead):
            start_chunk(b, c)

    @pl.when(b + 1 < n_seq)
    def _():
        start_idx(b + 1, 1 - islot)

    wait_idx(b, islot)

    iq2 = iq_ref[...].reshape(IDX_HEADS * t_new, LANES).astype(BF16)
    ikw = ikw_ref[...]

    def scores(ik_rows):
        s = _dot_nt(iq2, ik_rows)
        acc = jnp.zeros((t_new, ik_rows.shape[0]), F32)
        for h in range(IDX_HEADS):
            acc = acc + jnp.maximum(s[h * t_new:(h + 1) * t_new, :], 0.0) * ikw[:, IDX_DIM + h:IDX_DIM + h + 1]
        return acc

    iq2_past = iq2[:, 0:IDX_DIM]
    for c in range(n_chunks):
        s = _dot(iq2_past, ikbuf[islot, :, c * ck:(c + 1) * ck].astype(BF16))
        acc = jnp.zeros((t_new, ck), F32)
        for h in range(IDX_HEADS):
            acc = acc + jnp.maximum(s[h * t_new:(h + 1) * t_new, :], 0.0) * ikw[:, IDX_DIM + h:IDX_DIM + h + 1]
        sc_ref[:, c * ck:(c + 1) * ck] = acc
    new_idx = lax.broadcasted_iota(I32, (t_new, PAGE_SIZE), 1)
    tok = lax.broadcasted_iota(I32, (t_new, PAGE_SIZE), 0)
    sc_ref[:, past:n_keys] = jnp.where(new_idx <= tok, scores(ikn_ref[...]), -jnp.inf)
    key_idx = lax.broadcasted_iota(I32, (t_new, n_keys), 1)
    bias_ref[...] = _select_bias(sc_ref, lo_ref, n_keys, key_idx, topk, axis=1)

    rows = GROUP * t_new
    qs = [jnp.concatenate([q_ref[:, (g * GROUP + r) * HEAD_DIM:(g * GROUP + r + 1) * HEAD_DIM]
                           for r in range(GROUP)], axis=0).astype(BF16) for g in range(N_KV_HEADS)]
    m = [jnp.full((rows, 1), NEG_BIG, F32) for _ in range(N_KV_HEADS)]
    l = [jnp.zeros((rows, 1), F32) for _ in range(N_KV_HEADS)]
    acc = [jnp.zeros((rows, HEAD_DIM), F32) for _ in range(N_KV_HEADS)]

    def attend(g, k_rows, v_rows, bias):
        s = _dot_nt(qs[g], k_rows) + jnp.concatenate([bias] * GROUP, axis=0)
        m_new = jnp.maximum(m[g], _reduce(s, 1, jnp.maximum, jnp.max))
        alpha = jnp.exp2(m[g] - m_new)
        p = jnp.exp2(s - m_new)
        l[g] = alpha * l[g] + _reduce(p, 1, jnp.add, jnp.sum)
        acc[g] = alpha * acc[g] + _dot(p.astype(BF16), v_rows)
        m[g] = m_new

    for g in range(N_KV_HEADS):
        new_rows = pl.ds(g, PAGE_SIZE, stride=N_KV_HEADS)
        attend(g, kn_ref[new_rows, :].astype(BF16), vn_ref[new_rows, :].astype(BF16), bias_ref[:, past:n_keys])
    for c in range(n_chunks):
        nxt = c + ahead
        if nxt < n_chunks:
            start_chunk(b, nxt)
        else:
            pl.when(b + 1 < n_seq)(functools.partial(start_chunk, b + 1, nxt - n_chunks))
        wait_chunk(b, c)
        for g in range(N_KV_HEADS):
            head_rows = pl.ds(g, ck, stride=N_KV_HEADS)
            attend(g, kbuf[c, head_rows, :].astype(BF16), vbuf[c, head_rows, :].astype(BF16),
                   bias_ref[:, c * ck:(c + 1) * ck])
    for g in range(N_KV_HEADS):
        og = acc[g] / l[g]
        for r in range(GROUP):
            h = g * GROUP + r
            o_ref[:, h * HEAD_DIM:(h + 1) * HEAD_DIM] = og[r * t_new:(r + 1) * t_new, :]


def _att_sample(page_table, q, iq, ikw, ikn_pad, kn_pad, vn_pad, cache_ik, cache_k, cache_v, t_new):
    nseq, n_pages = page_table.shape
    assert t_new == SUBLANES
    chunk_pages = min(32, n_pages // 2)
    assert n_pages % chunk_pages == 0
    past = n_pages * PAGE_SIZE
    n_keys = past + PAGE_SIZE
    topk = min(TOPK_MAX, (past + t_new) // 4)
    ck = chunk_pages * PAGE_SIZE
    kern = functools.partial(_att_sample_kernel, n_pages=n_pages, chunk_pages=chunk_pages, topk=topk)
    grid_spec = pltpu.PrefetchScalarGridSpec(
        num_scalar_prefetch=1,
        grid=(nseq,),
        in_specs=[
            pl.BlockSpec((t_new, D_MODEL), lambda b, pt: (b, 0)),
            pl.BlockSpec((IDX_HEADS, t_new, LANES), lambda b, pt: (0, b, 0)),
            pl.BlockSpec((t_new, LANES), lambda b, pt: (b, 0)),
            pl.BlockSpec((None, PAGE_SIZE, LANES), lambda b, pt: (b, 0, 0)),
            pl.BlockSpec((None, PAGE_SIZE * N_KV_HEADS, HEAD_DIM), lambda b, pt: (b, 0, 0)),
            pl.BlockSpec((None, PAGE_SIZE * N_KV_HEADS, HEAD_DIM), lambda b, pt: (b, 0, 0)),
            pl.BlockSpec(memory_space=pl.ANY),
            pl.BlockSpec(memory_space=pl.ANY),
            pl.BlockSpec(memory_space=pl.ANY),
        ],
        out_specs=pl.BlockSpec((t_new, D_MODEL), lambda b, pt: (b, 0)),
        scratch_shapes=[
            pltpu.VMEM((2, IDX_DIM, past), F32),
            pltpu.VMEM((n_pages // chunk_pages, ck * N_KV_HEADS, HEAD_DIM), F32),
            pltpu.VMEM((n_pages // chunk_pages, ck * N_KV_HEADS, HEAD_DIM), F32),
            pltpu.VMEM((t_new, n_keys), F32),
            pltpu.VMEM((t_new, n_keys), F32),
            pltpu.VMEM((t_new, 1), I32),
            pltpu.SemaphoreType.DMA((2,)),
            pltpu.SemaphoreType.DMA((2, n_pages // chunk_pages)),
        ],
    )
    return pl.pallas_call(
        kern,
        grid_spec=grid_spec,
        out_shape=jax.ShapeDtypeStruct((nseq * t_new, D_MODEL), F32),
        compiler_params=pltpu.CompilerParams(dimension_semantics=("arbitrary",), vmem_limit_bytes=VMEM_LIMIT),
        name="att_sample",
    )(page_table, q, iq, ikw, ikn_pad, kn_pad, vn_pad, cache_ik, cache_k, cache_v)


def _out_proj_kernel(o_ref, sg_ref, w_ref, x_ref, y_ref):
    a = (o_ref[...] * sg_ref[...].astype(F32)).astype(BF16)
    y_ref[...] = x_ref[...] + _dot(a, w_ref[...])


def _out_proj(o, sg, w_out, x, tm):
    n = o.shape[0]
    row = pl.BlockSpec((tm, D_MODEL), lambda i: (i, 0))
    return pl.pallas_call(
        _out_proj_kernel,
        grid=(n // tm,),
        in_specs=[row, row, pl.BlockSpec((D_MODEL, D_MODEL), lambda i: (0, 0)), row],
        out_specs=row,
        out_shape=jax.ShapeDtypeStruct((n, D_MODEL), F32),
        compiler_params=pltpu.CompilerParams(dimension_semantics=("parallel",), vmem_limit_bytes=VMEM_LIMIT),
        name="out_proj",
    )(o, sg, w_out, x)


def _dn_proj_kernel(x_ref, g_ref, w_ref, cw_ref, buf_ref, alog_ref, dtb_ref,
                    q_ref, k_ref, v_ref, sz_ref, gb_ref, cbuf_ref, xp_ref, *, n_seq, pad):
    tm = x_ref.shape[0]

    @pl.when(pl.program_id(1) == 0)
    def _():
        xp_ref[0:pad, :] = buf_ref[...]

    xn = (_rms(x_ref[...], D_MODEL) * g_ref[...]).astype(BF16)
    p = _dot(xn, w_ref[...])
    xp_ref[pad:pad + tm, :] = p[:, 0:CONV_DIM]
    conv = xp_ref[pad:pad + tm, :] * cw_ref[CONV_W - 1:CONV_W, :]
    for i in range(1, CONV_W):
        conv = conv + xp_ref[pad - i * n_seq:pad - i * n_seq + tm, :] * cw_ref[CONV_W - 1 - i:CONV_W - i, :]
    act = conv * jax.nn.sigmoid(conv)
    for h in range(DN_HEADS):
        sl = slice(h * DN_DK, (h + 1) * DN_DK)
        qh = act[:, sl]
        q_ref[:, sl] = qh * (lax.rsqrt(jnp.sum(qh * qh, axis=-1, keepdims=True) + EPS) * (DN_DK ** -0.5))
        kh = act[:, DN_HEADS * DN_DK + h * DN_DK:DN_HEADS * DN_DK + (h + 1) * DN_DK]
        k_ref[:, sl] = kh * lax.rsqrt(jnp.sum(kh * kh, axis=-1, keepdims=True) + EPS)
    v_ref[...] = act[:, 2 * DN_HEADS * DN_DK:CONV_DIM]
    z = p[:, N_Z:N_BA]
    sz_ref[...] = (z * jax.nn.sigmoid(z)).astype(BF16)
    ba = p[:, N_BA:N_END]
    sp_in = ba + dtb_ref[...]
    softplus = jnp.maximum(sp_in, 0.0) + jnp.log1p(jnp.exp(-jnp.abs(sp_in)))
    lane = lax.broadcasted_iota(I32, (1, LANES), 1)
    gb_ref[...] = jnp.where(lane < DN_HEADS, jax.nn.sigmoid(ba), -jnp.exp(alog_ref[...]) * softplus)
    cbuf_ref[...] = xp_ref[tm:tm + pad, :]
    xp_ref[0:pad, :] = xp_ref[tm:tm + pad, :]


def _dn_proj(x, norm_g, w_packed, conv_w, buf, alog_pad, dtb_pad, n_groups, n_seq, tm):
    n = x.shape[0]
    pad = buf.shape[1]
    tiles = n // n_groups // tm
    assert n == n_groups * tiles * tm and pad % SUBLANES == 0 and pad >= (CONV_W - 1) * n_seq
    row = lambda s, i: (s * tiles + i, 0)
    const = lambda s, i: (0, 0)
    big = lambda d: pl.BlockSpec((tm, d), row)
    return pl.pallas_call(
        functools.partial(_dn_proj_kernel, n_seq=n_seq, pad=pad),
        grid=(n_groups, tiles),
        in_specs=[
            big(D_MODEL),
            pl.BlockSpec((1, D_MODEL), const),
            pl.BlockSpec((D_MODEL, N_END), const),
            pl.BlockSpec((CONV_W, CONV_DIM), const),
            pl.BlockSpec((None, pad, CONV_DIM), lambda s, i: (s, 0, 0)),
            pl.BlockSpec((1, LANES), const),
            pl.BlockSpec((1, LANES), const),
        ],
        out_specs=(big(D_MODEL), big(D_MODEL), big(D_MODEL), big(D_MODEL), big(LANES),
                   pl.BlockSpec((None, pad, CONV_DIM), lambda s, i: (s, 0, 0))),
        out_shape=(
            jax.ShapeDtypeStruct((n, D_MODEL), F32),
            jax.ShapeDtypeStruct((n, D_MODEL), F32),
            jax.ShapeDtypeStruct((n, D_MODEL), F32),
            jax.ShapeDtypeStruct((n, D_MODEL), BF16),
            jax.ShapeDtypeStruct((n, LANES), F32),
            jax.ShapeDtypeStruct((n_groups, pad, CONV_DIM), F32),
        ),
        scratch_shapes=[pltpu.VMEM((pad + tm, CONV_DIM), F32)],
        compiler_params=pltpu.CompilerParams(dimension_semantics=("parallel", "arbitrary"),
                                             vmem_limit_bytes=VMEM_LIMIT),
        name="dn_proj",
    )(x, norm_g, w_packed, conv_w, buf, alog_pad, dtb_pad)


def _dn_rec_kernel(q_ref, k_ref, v_ref, gb_ref, s0_ref, sz_ref, og_ref, w_ref, x_ref, y_ref, sout_ref,
                   s_scr, a_scr, *, c_len):
    n_sub = q_ref.shape[0] // c_len

    @pl.when(pl.program_id(1) == 0)
    def _():
        s_scr[...] = s0_ref[...]

    ri = lax.broadcasted_iota(I32, (c_len, c_len), 0)
    ci = lax.broadcasted_iota(I32, (c_len, c_len), 1)
    incl = ri >= ci
    strict = ri > ci
    tril = jnp.where(incl, 1.0, 0.0)
    eye_c = jnp.where(ri == ci, 1.0, 0.0)
    lane = lax.broadcasted_iota(I32, (c_len, LANES), 1)
    n_sq = max(0, (c_len - 1).bit_length() - 1)

    tile_aligned = c_len % LANES == 0

    subs = range(n_sub)
    rows = [slice(s * c_len, (s + 1) * c_len) for s in subs]
    gbc = [gb_ref[rows[s], :] for s in subs]
    gcum = [_dot(tril, gbc[s], HI) for s in subs]
    gcum_t = [gcum[s].T if tile_aligned else None for s in subs]
    units = [(s, h) for s in subs for h in range(DN_HEADS)]
    n_u = range(len(units))
    at = [(rows[s], slice(h * DN_DK, (h + 1) * DN_DK)) for s, h in units]
    beta = [gbc[s][:, h:h + 1] for s, h in units]
    gcol = [gcum[s][:, DN_HEADS + h:DN_HEADS + h + 1] for s, h in units]
    if tile_aligned:
        grow = [gcum_t[s][DN_HEADS + h:DN_HEADS + h + 1, :] for s, h in units]
    else:
        grow = [_dot_nt(jnp.where(lane == DN_HEADS + h, 1.0, 0.0), gcum[s], HI) for s, h in units]
    glast = [gcol[i][c_len - 1:c_len, :] for i in n_u]
    decay = [jnp.where(incl, jnp.exp(jnp.where(incl, gcol[i] - grow[i], 0.0)), 0.0) for i in n_u]
    eg = [jnp.exp(gcol[i]) for i in n_u]
    a_mat = [beta[i] * _dot_nt(k_ref[at[i]], k_ref[at[i]]) * jnp.where(strict, decay[i], 0.0) for i in n_u]
    pw = [-a_mat[i] for i in n_u]
    t_inv = [eye_c - a_mat[i] for i in n_u]
    for level in range(n_sq):
        if level < NEUMANN_SPLIT_LEVELS:
            ps = [_split_bf16(pw[i]) for i in n_u]
            pw = [_dot3(ps[i], ps[i]) for i in n_u]
            ps = [_split_bf16(pw[i]) for i in n_u]
            t_inv = [t_inv[i] + _dot3(_split_bf16(t_inv[i]), ps[i]) for i in n_u]
        else:
            pw = [_dot(pw[i].astype(BF16), pw[i].astype(BF16)) for i in n_u]
            t_inv = [t_inv[i] + _dot(t_inv[i].astype(BF16), pw[i].astype(BF16)) for i in n_u]
    wu = [_dot(t_inv[i].astype(BF16),
               jnp.concatenate([beta[i] * eg[i] * k_ref[at[i]], beta[i] * v_ref[at[i]]], axis=1).astype(BF16))
          for i in n_u]
    qk = [_dot_nt(q_ref[at[i]], k_ref[at[i]]) * decay[i] for i in n_u]
    kd = [k_ref[at[i]] * jnp.exp(glast[i] - gcol[i]) for i in n_u]
    kdt = [kd[i].T if tile_aligned else None for i in n_u]
    for i, (s, h) in enumerate(units):
        s_old = s_scr[h]
        u = wu[i][:, DN_DK:DN_DK + DN_DV] - _dot(wu[i][:, 0:DN_DK], s_old)
        o = eg[i] * _dot(q_ref[at[i]], s_old) + _dot(qk[i], u)
        a_scr[at[i]] = _rms(o, DN_DV) * og_ref[...] * sz_ref[at[i]].astype(F32)
        s_scr[h] = jnp.exp(glast[i]) * s_old + (_dot(kdt[i], u) if tile_aligned else _dot_tn(kd[i], u))
    y_ref[...] = x_ref[...] + _dot(a_scr[...].astype(BF16), w_ref[...])

    @pl.when(pl.program_id(1) == pl.num_programs(1) - 1)
    def _():
        sout_ref[...] = s_scr[...]


def _dn_rec(q, k, v, gb, s0, sz, o_gain, w_out, x, n_seq, t_len, chunk, chunks_per_step):
    step = chunk * chunks_per_step
    assert t_len % step == 0
    n_steps = t_len // step
    row = lambda b, c: (b * n_steps + c, 0)
    const = lambda b, c: (0, 0)
    wide = pl.BlockSpec((step, D_MODEL), row)
    st = pl.BlockSpec((None, None, DN_HEADS, DN_DK, DN_DV), lambda b, c: (b, 0, 0, 0, 0))
    return pl.pallas_call(
        functools.partial(_dn_rec_kernel, c_len=chunk),
        grid=(n_seq, n_steps),
        in_specs=[wide, wide, wide, pl.BlockSpec((step, LANES), row), st, wide,
                  pl.BlockSpec((1, DN_DV), const), pl.BlockSpec((D_MODEL, D_MODEL), const), wide],
        out_specs=(wide, st),
        out_shape=(jax.ShapeDtypeStruct((n_seq * t_len, D_MODEL), F32),
                   jax.ShapeDtypeStruct((n_seq, 1, DN_HEADS, DN_DK, DN_DV), F32)),
        scratch_shapes=[pltpu.VMEM((DN_HEADS, DN_DK, DN_DV), F32), pltpu.VMEM((step, D_MODEL), F32)],
        compiler_params=pltpu.CompilerParams(dimension_semantics=("parallel", "arbitrary"),
                                             vmem_limit_bytes=VMEM_LIMIT),
        name="dn_rec",
    )(q, k, v, gb, s0, sz, o_gain, w_out, x)


def _pad_lanes(v, offset=0):
    return jnp.zeros((1, LANES), F32).at[0, offset:offset + v.shape[0]].set(v)


def _split_w(w, main, tail):
    small = jnp.pad(w[:, main:tail], ((0, 0), (0, LANES - (tail - main))))
    return w[:, :main].astype(BF16), small.astype(BF16), w[:, tail:].astype(BF16)


def kernel(x_prompt, x_sample, cache_k, cache_v, cache_idx_k, state_dn_S, state_dn_conv, page_table,
           att_norm, att_w_in, att_q_gain, att_k_gain, att_ik_gain, att_w_out,
           dn_norm, dn_w_in, dn_conv_w, dn_A_log, dn_dt_bias, dn_o_gain, dn_w_out):
    bp, seq, d = x_prompt.shape
    bs, t_new, _ = x_sample.shape
    n_pool = cache_k.shape[0]
    np_rows, ns_rows = bp * seq, bs * t_new
    tm, tm_dn = 512, 256

    xp = x_prompt.reshape(np_rows, d)
    xs = x_sample.reshape(ns_rows, d)

    w_att = _split_w(att_w_in[0], A_IKW, A_IKW + IDX_DIM + IDX_HEADS)
    a_norm = att_norm[0][None, :]
    qg, kg = att_q_gain[0][None, :], att_k_gain[0][None, :]
    ikg = _pad_lanes(att_ik_gain[0])
    w_ao = att_w_out[0].astype(BF16)

    q_p, k_p, kb_p, v_p, vt_p, iq_p, ikw_p, ikb_p, iwt_p, sg_p = _att_proj(xp, a_norm, *w_att, qg, kg, ikg, tm)
    y1_p = _att_prompt(q_p, kb_p, vt_p, ikb_p, iq_p, iwt_p, sg_p, w_ao, xp, bp, seq)

    q_s, k_s, _, v_s, _, iq_s, ikw_s, ikb_s, _, sg_s = _att_proj(xs, a_norm, *w_att, qg, kg, ikg, min(tm, ns_rows))
    pad_new = lambda a, per_tok: jnp.pad(a.reshape(bs, t_new * per_tok, a.shape[-1]),
                                         ((0, 0), (0, (PAGE_SIZE - t_new) * per_tok), (0, 0)))
    o_s = _att_sample(page_table, q_s.astype(F32), iq_s.astype(F32), ikw_s, pad_new(ikb_s, 1),
                      pad_new(k_s, N_KV_HEADS), pad_new(v_s, N_KV_HEADS),
                      jnp.transpose(cache_idx_k, (0, 2, 3, 1)).reshape(n_pool, IDX_DIM, PAGE_SIZE),
                      cache_k.reshape(n_pool, PAGE_SIZE * N_KV_HEADS, HEAD_DIM),
                      cache_v.reshape(n_pool, PAGE_SIZE * N_KV_HEADS, HEAD_DIM), t_new)
    y1_s = _out_proj(o_s, sg_s, w_ao, xs, min(tm, ns_rows))

    w_dn = jnp.pad(dn_w_in[0], ((0, 0), (0, N_END - dn_w_in.shape[-1]))).astype(BF16)
    d_norm = dn_norm[0][None, :]
    alog = _pad_lanes(dn_A_log[0], DN_HEADS)
    dtb = _pad_lanes(dn_dt_bias[0], DN_HEADS)
    o_gain = dn_o_gain[0][None, :]
    w_do = dn_w_out[0].astype(BF16)
    hist = CONV_W - 1

    buf_p = jnp.zeros((bp, SUBLANES, CONV_DIM), F32)
    dq_p, dk_p, dv_p, sz_p, gb_p, cb_p = _dn_proj(y1_p, d_norm, w_dn, dn_conv_w[0], buf_p, alog, dtb,
                                                   n_groups=bp, n_seq=1, tm=tm_dn)
    s0_p = jnp.zeros((bp, 1, DN_HEADS, DN_DK, DN_DV), F32)
    y2_p, s_p = _dn_rec(dq_p, dk_p, dv_p, gb_p, s0_p, sz_p, o_gain, w_do, y1_p, bp, seq, chunk=LANES,
                        chunks_per_step=4)

    to_tm = lambda a: a.reshape(bs, t_new, -1).transpose(1, 0, 2).reshape(ns_rows, -1)
    to_bm = lambda a: a.reshape(t_new, bs, -1).transpose(1, 0, 2).reshape(ns_rows, -1)
    buf_s = state_dn_conv[:, 0].transpose(1, 0, 2).reshape(1, hist * bs, CONV_DIM)
    dq_s, dk_s, dv_s, sz_s, gb_s, cb_s = _dn_proj(to_tm(y1_s), d_norm, w_dn, dn_conv_w[0], buf_s, alog, dtb,
                                                   n_groups=1, n_seq=bs, tm=ns_rows)
    y2_s, s_s = _dn_rec(to_bm(dq_s), to_bm(dk_s), to_bm(dv_s), to_bm(gb_s), state_dn_S,
                        to_bm(sz_s.astype(F32)), o_gain, w_do, y1_s, bs, t_new, chunk=t_new,
                        chunks_per_step=1)

    return (
        y2_p.reshape(bp, seq, d),
        y2_s.reshape(bs, t_new, d),
        k_p.reshape(bp, seq, 1, N_KV_HEADS, HEAD_DIM),
        v_p.reshape(bp, seq, 1, N_KV_HEADS, HEAD_DIM),
        ikw_p[:, :IDX_DIM].reshape(bp, seq, 1, IDX_DIM),
        k_s.reshape(bs, t_new, 1, N_KV_HEADS, HEAD_DIM),
        v_s.reshape(bs, t_new, 1, N_KV_HEADS, HEAD_DIM),
        ikw_s[:, :IDX_DIM].reshape(bs, t_new, 1, IDX_DIM),
        s_p,
        cb_p[:, SUBLANES - hist:, :].reshape(bp, 1, hist, CONV_DIM),
        s_s,
        cb_s.reshape(hist, bs, CONV_DIM).transpose(1, 0, 2).reshape(bs, 1, hist, CONV_DIM),
    )
```

```python
import functools

import jax
import jax.numpy as jnp
from jax import lax
from jax.experimental import pallas as pl
from jax.experimental.pallas import tpu as pltpu

F32 = jnp.float32
BF16 = jnp.bfloat16
I32 = jnp.int32

EPS = 1e-6
LANES = 128
SUBLANES = 8
VMEM_LIMIT = 56 * 1024 * 1024

D_MODEL = 1024
N_HEADS = 8
HEAD_DIM = 128
N_KV_HEADS = 2
GROUP = N_HEADS // N_KV_HEADS
IDX_HEADS = 8
IDX_DIM = 64
TOPK_MAX = 256
PAGE_SIZE = 128
DN_HEADS = 8
DN_DK = 128
DN_DV = 128
CONV_W = 4
KV_W = N_KV_HEADS * HEAD_DIM
CONV_DIM = 3 * DN_HEADS * DN_DK

A_Q, A_K, A_V, A_IQ, A_IKW = 0, 1024, 1280, 1536, 2048
N_Z, N_BA, N_END = 3072, 4096, 4224

INT_MIN = -2147483648
INT_MAX = 2147483647
NEG_BIG = -1e30
LOG2_E = 1.4426950408889634
NT_DIMS = (((1,), (1,)), ((), ()))
HI = lax.Precision.HIGHEST
NEUMANN_SPLIT_LEVELS = 4


def _dot(a, b, precision=None):
    return jnp.dot(a, b, preferred_element_type=F32, precision=precision)


def _dot_nt(a, b, precision=None):
    return lax.dot_general(a, b, NT_DIMS, preferred_element_type=F32, precision=precision)


def _dot_tn(a, b):
    return lax.dot_general(a, b, (((0,), (0,)), ((), ())), preferred_element_type=F32)


def _split_bf16(x):
    hi = x.astype(BF16)
    return hi, (x - hi.astype(F32)).astype(BF16)


def _dot3(a, b):
    (ah, al), (bh, bl) = a, b
    return _dot(ah, bh) + (_dot(ah, bl) + _dot(al, bh))


def _rms(x, n):
    return x * lax.rsqrt(jnp.sum(x * x, axis=-1, keepdims=True) * (1.0 / n) + EPS)


def _ordinal_to_float(u):
    key = u ^ jnp.int32(INT_MIN)
    return pltpu.bitcast(jnp.where(key < 0, key ^ jnp.int32(INT_MAX), key), F32)


def _att_proj_kernel(x_ref, g_ref, w_ref, wikw_ref, wg_ref, qg_ref, kg_ref, ikg_ref,
                     q_ref, k_ref, kb_ref, v_ref, vt_ref, iq_ref, ikw_ref, ikb_ref, iwt_ref, sg_ref):
    tm = x_ref.shape[0]
    x = x_ref[...]
    xn = (_rms(x, D_MODEL) * g_ref[...]).astype(BF16)
    p = _dot(xn, w_ref[...])
    q_gain = qg_ref[...] * (HEAD_DIM ** -0.5 * LOG2_E)
    for h in range(N_HEADS):
        qh = _rms(p[:, A_Q + h * HEAD_DIM:A_Q + (h + 1) * HEAD_DIM], HEAD_DIM) * q_gain
        q_ref[:, h * HEAD_DIM:(h + 1) * HEAD_DIM] = qh.astype(BF16)
    v = p[:, A_V:A_V + KV_W]
    for g in range(N_KV_HEADS):
        kh = _rms(p[:, A_K + g * HEAD_DIM:A_K + (g + 1) * HEAD_DIM], HEAD_DIM) * kg_ref[...]
        k_ref[pl.ds(g, tm, stride=N_KV_HEADS), :] = kh
        kb_ref[:, g * HEAD_DIM:(g + 1) * HEAD_DIM] = kh.astype(BF16)
        v_ref[pl.ds(g, tm, stride=N_KV_HEADS), :] = v[:, g * HEAD_DIM:(g + 1) * HEAD_DIM]
    vt_ref[...] = v.T.astype(BF16)
    lane = lax.broadcasted_iota(I32, (1, LANES), 1)
    is_ik = lane < IDX_DIM
    for h in range(IDX_HEADS):
        iqh = p[:, A_IQ + (h // 2) * LANES:A_IQ + (h // 2 + 1) * LANES]
        if h % 2:
            iqh = pltpu.roll(iqh, IDX_DIM, axis=1)
        iq_ref[h] = jnp.where(is_ik, iqh, 0.0).astype(BF16)
    ikw = _dot(xn, wikw_ref[...])
    ik = jnp.where(is_ik, ikw, 0.0)
    ikn = _rms(ik, IDX_DIM) * ikg_ref[...]
    out = jnp.where(is_ik, ikn, ikw * (IDX_HEADS ** -0.5 * IDX_DIM ** -0.5))
    ikw_ref[...] = out
    ikb_ref[...] = ikn.astype(BF16)
    iwt_ref[...] = out.T[IDX_DIM:IDX_DIM + IDX_HEADS, :]
    gate = _dot(xn, wg_ref[...])
    sg_ref[...] = (gate * jax.nn.sigmoid(gate)).astype(BF16)


def _att_proj(x, norm_g, w_main, w_ikw, w_gate, q_gain, k_gain, ik_gain_pad, tm):
    n = x.shape[0]
    assert n % tm == 0 and tm % LANES == 0
    row = lambda i: (i, 0)
    const = lambda i: (0, 0)
    out_shape = (
        jax.ShapeDtypeStruct((n, D_MODEL), BF16),
        jax.ShapeDtypeStruct((N_KV_HEADS * n, HEAD_DIM), F32),
        jax.ShapeDtypeStruct((n, KV_W), BF16),
        jax.ShapeDtypeStruct((N_KV_HEADS * n, HEAD_DIM), F32),
        jax.ShapeDtypeStruct((KV_W, n), BF16),
        jax.ShapeDtypeStruct((IDX_HEADS, n, LANES), BF16),
        jax.ShapeDtypeStruct((n, LANES), F32),
        jax.ShapeDtypeStruct((n, LANES), BF16),
        jax.ShapeDtypeStruct((IDX_HEADS, n), F32),
        jax.ShapeDtypeStruct((n, D_MODEL), BF16),
    )
    out_specs = (
        pl.BlockSpec((tm, D_MODEL), row),
        pl.BlockSpec((N_KV_HEADS * tm, HEAD_DIM), row),
        pl.BlockSpec((tm, KV_W), row),
        pl.BlockSpec((N_KV_HEADS * tm, HEAD_DIM), row),
        pl.BlockSpec((KV_W, tm), lambda i: (0, i)),
        pl.BlockSpec((IDX_HEADS, tm, LANES), lambda i: (0, i, 0)),
        pl.BlockSpec((tm, LANES), row),
        pl.BlockSpec((tm, LANES), row),
        pl.BlockSpec((IDX_HEADS, tm), lambda i: (0, i)),
        pl.BlockSpec((tm, D_MODEL), row),
    )
    return pl.pallas_call(
        _att_proj_kernel,
        grid=(n // tm,),
        in_specs=[
            pl.BlockSpec((tm, D_MODEL), row),
            pl.BlockSpec((1, D_MODEL), const),
            pl.BlockSpec((D_MODEL, A_IKW), const),
            pl.BlockSpec((D_MODEL, LANES), const),
            pl.BlockSpec((D_MODEL, D_MODEL), const),
            pl.BlockSpec((1, HEAD_DIM), const),
            pl.BlockSpec((1, HEAD_DIM), const),
            pl.BlockSpec((1, LANES), const),
        ],
        out_specs=out_specs,
        out_shape=out_shape,
        compiler_params=pltpu.CompilerParams(dimension_semantics=("parallel",), vmem_limit_bytes=VMEM_LIMIT),
        name="att_proj",
    )(x, norm_g, w_main, w_ikw, w_gate, q_gain, k_gain, ik_gain_pad)


def _reduce(x, axis, combine, finish):
    n = x.shape[axis]
    unit = 8 * (SUBLANES if axis == 0 else LANES)
    full = n // unit
    if full < 2:
        return finish(x, axis=axis, keepdims=True)
    cut = (lambda i, j: x[i:j, :]) if axis == 0 else (lambda i, j: x[:, i:j])
    acc = cut(0, unit)
    for i in range(1, full):
        acc = combine(acc, cut(i * unit, (i + 1) * unit))
    out = finish(acc, axis=axis, keepdims=True)
    if n % unit:
        out = combine(out, finish(cut(full * unit, n), axis=axis, keepdims=True))
    return out


def _count(mask, axis):
    return _reduce(jnp.where(mask, 1.0, 0.0), axis, jnp.add, jnp.sum)


def _select_bias(sc_ref, lo_ref, n_keys, key_idx, topk, axis):
    sl = (slice(0, n_keys), slice(None)) if axis == 0 else (slice(None), slice(0, n_keys))
    vec = (1, sc_ref.shape[1]) if axis == 0 else (sc_ref.shape[0], 1)
    kf = float(topk)

    def bit_step(i, prefix):
        cand = prefix | (jnp.int32(1) << (31 - i))
        cnt = _count(sc_ref[sl] >= _ordinal_to_float(cand), axis)
        return jnp.where(cnt >= kf, cand, prefix)

    prefix = lax.fori_loop(0, 32, bit_step, jnp.zeros(vec, I32))
    thr = _ordinal_to_float(prefix)
    sc = sc_ref[sl]
    few = jnp.logical_not(_count(sc >= thr, axis) >= kf)
    thr = jnp.where(few, -jnp.inf, thr)
    need = kf - _count(sc > thr, axis)
    excess = (_count(sc >= thr, axis) > kf) & jnp.logical_not(few)
    lo_ref[...] = jnp.full(vec, INT_MAX, I32)

    @pl.when(jnp.max(jnp.where(excess, 1.0, 0.0)) > 0.0)
    def _():
        nbits = max(1, (n_keys - 1).bit_length())

        def idx_step(i, lo):
            cand = lo + (jnp.int32(1) << (nbits - 1 - i))
            cnt = _count((sc_ref[sl] == thr) & (key_idx < cand), axis)
            return jnp.where(cnt < need, cand, lo)

        lo_ref[...] = lax.fori_loop(0, nbits, idx_step, jnp.zeros(vec, I32))

    lo = jnp.where(few, -1, lo_ref[...])
    sel = (sc > thr) | ((sc == thr) & (key_idx <= lo))
    return jnp.where(sel, 0.0, NEG_BIG)


def _att_prompt_body(n_keys, topk, q_ref, kb_ref, vt_ref, ikb_ref, iq_ref, iwt_ref, o_ref,
                     sc_ref, bias_ref, lo_ref):
    qb = q_ref.shape[0]
    t0 = pl.program_id(0) * qb
    ik = ikb_ref[0:n_keys, :]
    iwt = iwt_ref[...]
    score = jnp.zeros((n_keys, qb), F32)
    for h in range(IDX_HEADS):
        s = _dot_nt(ik, iq_ref[h])
        score = score + jnp.maximum(s, 0.0) * iwt[h:h + 1, :]
    key_idx = lax.broadcasted_iota(I32, (n_keys, qb), 0)
    q_pos = t0 + lax.broadcasted_iota(I32, (n_keys, qb), 1)
    sc_ref[0:n_keys, :] = jnp.where(key_idx <= q_pos, score, -jnp.inf)
    bias_ref[0:n_keys, :] = _select_bias(sc_ref, lo_ref, n_keys, key_idx, topk, axis=0)
    heads = range(N_HEADS)
    hd = [slice(h * HEAD_DIM, (h + 1) * HEAD_DIM) for h in heads]
    gd = [slice((h // GROUP) * HEAD_DIM, (h // GROUP + 1) * HEAD_DIM) for h in heads]
    s = [_dot_nt(kb_ref[0:n_keys, gd[h]], q_ref[:, hd[h]]) + bias_ref[0:n_keys, :] for h in heads]
    m = [_reduce(s[h], 0, jnp.maximum, jnp.max) for h in heads]
    p = [jnp.exp2(s[h] - m[h]) for h in heads]
    ones = jnp.ones((2 * SUBLANES, n_keys), BF16)
    vt1 = [jnp.concatenate([vt_ref[g * HEAD_DIM:(g + 1) * HEAD_DIM, 0:n_keys], ones], axis=0)
           for g in range(N_KV_HEADS)]
    ot = [_dot(vt1[h // GROUP], p[h].astype(BF16)) for h in heads]
    for h in heads:
        o_ref[:, hd[h]] = (ot[h][0:HEAD_DIM, :] / ot[h][HEAD_DIM:HEAD_DIM + 1, :]).T


def _att_prompt_kernel(q_ref, kb_ref, vt_ref, ikb_ref, iq_ref, iwt_ref, sg_ref, w_ref, x_ref, y_ref,
                       o_scr, sc_ref, bias_ref, lo_ref, *, cls_len, n_cls, topk):
    qb = q_ref.shape[0]
    cls = (pl.program_id(0) * qb) // cls_len
    for c in range(n_cls):
        pl.when(cls == c)(functools.partial(
            _att_prompt_body, cls_len * (c + 1), topk, q_ref, kb_ref, vt_ref, ikb_ref, iq_ref, iwt_ref,
            o_scr, sc_ref, bias_ref, lo_ref))
    a = (o_scr[...] * sg_ref[...].astype(F32)).astype(BF16)
    y_ref[...] = x_ref[...] + _dot(a, w_ref[...])


def _att_prompt(q, kb, vt, ikb, iq, iwt, sg, w_out, x, batch, seq, qb=LANES):
    nb = seq // qb
    n_cls = min(8, nb)
    assert seq % (n_cls * qb) == 0
    topk = min(TOPK_MAX, seq // 4)
    kern = functools.partial(_att_prompt_kernel, cls_len=seq // n_cls, n_cls=n_cls, topk=topk)
    return pl.pallas_call(
        kern,
        grid=(nb, batch),
        in_specs=[
            pl.BlockSpec((qb, D_MODEL), lambda j, b: (b * nb + j, 0)),
            pl.BlockSpec((seq, KV_W), lambda j, b: (b, 0)),
            pl.BlockSpec((KV_W, seq), lambda j, b: (0, b)),
            pl.BlockSpec((seq, LANES), lambda j, b: (b, 0)),
            pl.BlockSpec((IDX_HEADS, qb, LANES), lambda j, b: (0, b * nb + j, 0)),
            pl.BlockSpec((IDX_HEADS, qb), lambda j, b: (0, b * nb + j)),
            pl.BlockSpec((qb, D_MODEL), lambda j, b: (b * nb + j, 0)),
            pl.BlockSpec((D_MODEL, D_MODEL), lambda j, b: (0, 0)),
            pl.BlockSpec((qb, D_MODEL), lambda j, b: (b * nb + j, 0)),
        ],
        out_specs=pl.BlockSpec((qb, D_MODEL), lambda j, b: (b * nb + j, 0)),
        out_shape=jax.ShapeDtypeStruct((batch * seq, D_MODEL), F32),
        scratch_shapes=[
            pltpu.VMEM((qb, D_MODEL), F32),
            pltpu.VMEM((seq, qb), F32),
            pltpu.VMEM((seq, qb), F32),
            pltpu.VMEM((1, qb), I32),
        ],
        compiler_params=pltpu.CompilerParams(dimension_semantics=("parallel", "arbitrary"),
                                             vmem_limit_bytes=VMEM_LIMIT),
        name="att_prompt",
    )(q, kb, vt, ikb, iq, iwt, sg, w_out, x)


def _att_sample_kernel(pt_ref, q_ref, iq_ref, ikw_ref, ikn_ref, kn_ref, vn_ref,
                       cik_hbm, ck_hbm, cv_hbm, o_ref,
                       ikbuf, kbuf, vbuf, sc_ref, bias_ref, lo_ref, sem_i, sem_kv,
                       *, n_pages, chunk_pages, topk):
    b = pl.program_id(0)
    t_new = q_ref.shape[0]
    past = n_pages * PAGE_SIZE
    n_keys = past + PAGE_SIZE
    n_chunks = n_pages // chunk_pages
    ck = chunk_pages * PAGE_SIZE

    n_seq = pl.num_programs(0)
    islot = b % 2
    ahead = n_chunks - 1

    def idx_copy(seq, slot, p):
        return pltpu.make_async_copy(cik_hbm.at[pt_ref[seq, p]],
                                     ikbuf.at[slot, :, pl.ds(p * PAGE_SIZE, PAGE_SIZE)], sem_i.at[slot])

    def kv_copies(seq, c, p):
        page = pt_ref[seq, c * chunk_pages + p]
        dst = pl.ds(p * PAGE_SIZE * N_KV_HEADS, PAGE_SIZE * N_KV_HEADS)
        return (pltpu.make_async_copy(ck_hbm.at[page], kbuf.at[c, dst], sem_kv.at[0, c]),
                pltpu.make_async_copy(cv_hbm.at[page], vbuf.at[c, dst], sem_kv.at[1, c]))

    def start_idx(seq, slot):
        def go(p, carry):
            idx_copy(seq, slot, p).start()
            return carry
        lax.fori_loop(0, n_pages, go, 0, unroll=8)

    def wait_idx(seq, slot):
        def go(p, carry):
            idx_copy(seq, slot, p).wait()
            return carry
        lax.fori_loop(0, n_pages, go, 0, unroll=8)

    def start_chunk(seq, c):
        def go(p, carry):
            for cp in kv_copies(seq, c, p):
                cp.start()
            return carry
        lax.fori_loop(0, chunk_pages, go, 0, unroll=8)

    def wait_chunk(seq, c):
        def go(p, carry):
            for cp in kv_copies(seq, c, p):
                cp.wait()
            return carry
        lax.fori_loop(0, chunk_pages, go, 0, unroll=8)

    @pl.when(b == 0)
    def _():
        start_idx(b, islot)
        for c in range(ahead):
            start_chunk(b, c)

    @pl.when(b + 1 < n_seq)
    def _():
        start_idx(b + 1, 1 - islot)

    wait_idx(b, islot)

    iq2 = iq_ref[...].reshape(IDX_HEADS * t_new, LANES).astype(BF16)
    ikw = ikw_ref[...]

    def scores(ik_rows):
        s = _dot_nt(iq2, ik_rows)
        acc = jnp.zeros((t_new, ik_rows.shape[0]), F32)
        for h in range(IDX_HEADS):
            acc = acc + jnp.maximum(s[h * t_new:(h + 1) * t_new, :], 0.0) * ikw[:, IDX_DIM + h:IDX_DIM + h + 1]
        return acc

    iq2_past = iq2[:, 0:IDX_DIM]
    for c in range(n_chunks):
        s = _dot(iq2_past, ikbuf[islot, :, c * ck:(c + 1) * ck].astype(BF16))
        acc = jnp.zeros((t_new, ck), F32)
        for h in range(IDX_HEADS):
            acc = acc + jnp.maximum(s[h * t_new:(h + 1) * t_new, :], 0.0) * ikw[:, IDX_DIM + h:IDX_DIM + h + 1]
        sc_ref[:, c * ck:(c + 1) * ck] = acc
    new_idx = lax.broadcasted_iota(I32, (t_new, PAGE_SIZE), 1)
    tok = lax.broadcasted_iota(I32, (t_new, PAGE_SIZE), 0)
    sc_ref[:, past:n_keys] = jnp.where(new_idx <= tok, scores(ikn_ref[...]), -jnp.inf)
    key_idx = lax.broadcasted_iota(I32, (t_new, n_keys), 1)
    bias_ref[...] = _select_bias(sc_ref, lo_ref, n_keys, key_idx, topk, axis=1)

    rows = GROUP * t_new
    qs = [jnp.concatenate([q_ref[:, (g * GROUP + r) * HEAD_DIM:(g * GROUP + r + 1) * HEAD_DIM]
                           for r in range(GROUP)], axis=0).astype(BF16) for g in range(N_KV_HEADS)]
    m = [jnp.full((rows, 1), NEG_BIG, F32) for _ in range(N_KV_HEADS)]
    l = [jnp.zeros((rows, 1), F32) for _ in range(N_KV_HEADS)]
    acc = [jnp.zeros((rows, HEAD_DIM), F32) for _ in range(N_KV_HEADS)]

    def attend(g, k_rows, v_rows, bias):
        s = _dot_nt(qs[g], k_rows) + jnp.concatenate([bias] * GROUP, axis=0)
        m_new = jnp.maximum(m[g], _reduce(s, 1, jnp.maximum, jnp.max))
        alpha = jnp.exp2(m[g] - m_new)
        p = jnp.exp2(s - m_new)
        l[g] = alpha * l[g] + _reduce(p, 1, jnp.add, jnp.sum)
        acc[g] = alpha * acc[g] + _dot(p.astype(BF16), v_rows)
        m[g] = m_new

    for g in range(N_KV_HEADS):
        new_rows = pl.ds(g, PAGE_SIZE, stride=N_KV_HEADS)
        attend(g, kn_ref[new_rows, :].astype(BF16), vn_ref[new_rows, :].astype(BF16), bias_ref[:, past:n_keys])
    for c in range(n_chunks):
        nxt = c + ahead
        if nxt < n_chunks:
            start_chunk(b, nxt)
        else:
            pl.when(b + 1 < n_seq)(functools.partial(start_chunk, b + 1, nxt - n_chunks))
        wait_chunk(b, c)
        for g in range(N_KV_HEADS):
            head_rows = pl.ds(g, ck, stride=N_KV_HEADS)
            attend(g, kbuf[c, head_rows, :].astype(BF16), vbuf[c, head_rows, :].astype(BF16),
                   bias_ref[:, c * ck:(c + 1) * ck])
    for g in range(N_KV_HEADS):
        og = acc[g] / l[g]
        for r in range(GROUP):
            h = g * GROUP + r
            o_ref[:, h * HEAD_DIM:(h + 1) * HEAD_DIM] = og[r * t_new:(r + 1) * t_new, :]


def _att_sample(page_table, q, iq, ikw, ikn_pad, kn_pad, vn_pad, cache_ik, cache_k, cache_v, t_new):
    nseq, n_pages = page_table.shape
    assert t_new == SUBLANES
    chunk_pages = min(32, n_pages // 2)
    assert n_pages % chunk_pages == 0
    past = n_pages * PAGE_SIZE
    n_keys = past + PAGE_SIZE
    topk = min(TOPK_MAX, (past + t_new) // 4)
    ck = chunk_pages * PAGE_SIZE
    kern = functools.partial(_att_sample_kernel, n_pages=n_pages, chunk_pages=chunk_pages, topk=topk)
    grid_spec = pltpu.PrefetchScalarGridSpec(
        num_scalar_prefetch=1,
        grid=(nseq,),
        in_specs=[
            pl.BlockSpec((t_new, D_MODEL), lambda b, pt: (b, 0)),
            pl.BlockSpec((IDX_HEADS, t_new, LANES), lambda b, pt: (0, b, 0)),
            pl.BlockSpec((t_new, LANES), lambda b, pt: (b, 0)),
            pl.BlockSpec((None, PAGE_SIZE, LANES), lambda b, pt: (b, 0, 0)),
            pl.BlockSpec((None, PAGE_SIZE * N_KV_HEADS, HEAD_DIM), lambda b, pt: (b, 0, 0)),
            pl.BlockSpec((None, PAGE_SIZE * N_KV_HEADS, HEAD_DIM), lambda b, pt: (b, 0, 0)),
            pl.BlockSpec(memory_space=pl.ANY),
            pl.BlockSpec(memory_space=pl.ANY),
            pl.BlockSpec(memory_space=pl.ANY),
        ],
        out_specs=pl.BlockSpec((t_new, D_MODEL), lambda b, pt: (b, 0)),
        scratch_shapes=[
            pltpu.VMEM((2, IDX_DIM, past), F32),
            pltpu.VMEM((n_pages // chunk_pages, ck * N_KV_HEADS, HEAD_DIM), F32),
            pltpu.VMEM((n_pages // chunk_pages, ck * N_KV_HEADS, HEAD_DIM), F32),
            pltpu.VMEM((t_new, n_keys), F32),
            pltpu.VMEM((t_new, n_keys), F32),
            pltpu.VMEM((t_new, 1), I32),
            pltpu.SemaphoreType.DMA((2,)),
            pltpu.SemaphoreType.DMA((2, n_pages // chunk_pages)),
        ],
    )
    return pl.pallas_call(
        kern,
        grid_spec=grid_spec,
        out_shape=jax.ShapeDtypeStruct((nseq * t_new, D_MODEL), F32),
        compiler_params=pltpu.CompilerParams(dimension_semantics=("arbitrary",), vmem_limit_bytes=VMEM_LIMIT),
        name="att_sample",
    )(page_table, q, iq, ikw, ikn_pad, kn_pad, vn_pad, cache_ik, cache_k, cache_v)


def _out_proj_kernel(o_ref, sg_ref, w_ref, x_ref, y_ref):
    a = (o_ref[...] * sg_ref[...].astype(F32)).astype(BF16)
    y_ref[...] = x_ref[...] + _dot(a, w_ref[...])


def _out_proj(o, sg, w_out, x, tm):
    n = o.shape[0]
    row = pl.BlockSpec((tm, D_MODEL), lambda i: (i, 0))
    return pl.pallas_call(
        _out_proj_kernel,
        grid=(n // tm,),
        in_specs=[row, row, pl.BlockSpec((D_MODEL, D_MODEL), lambda i: (0, 0)), row],
        out_specs=row,
        out_shape=jax.ShapeDtypeStruct((n, D_MODEL), F32),
        compiler_params=pltpu.CompilerParams(dimension_semantics=("parallel",), vmem_limit_bytes=VMEM_LIMIT),
        name="out_proj",
    )(o, sg, w_out, x)


def _dn_proj_kernel(x_ref, g_ref, w_ref, cw_ref, buf_ref, alog_ref, dtb_ref,
                    q_ref, k_ref, v_ref, sz_ref, gb_ref, cbuf_ref, xp_ref, *, n_seq, pad):
    tm = x_ref.shape[0]

    @pl.when(pl.program_id(1) == 0)
    def _():
        xp_ref[0:pad, :] = buf_ref[...]

    xn = (_rms(x_ref[...], D_MODEL) * g_ref[...]).astype(BF16)
    p = _dot(xn, w_ref[...])
    xp_ref[pad:pad + tm, :] = p[:, 0:CONV_DIM]
    conv = xp_ref[pad:pad + tm, :] * cw_ref[CONV_W - 1:CONV_W, :]
    for i in range(1, CONV_W):
        conv = conv + xp_ref[pad - i * n_seq:pad - i * n_seq + tm, :] * cw_ref[CONV_W - 1 - i:CONV_W - i, :]
    act = conv * jax.nn.sigmoid(conv)
    for h in range(DN_HEADS):
        sl = slice(h * DN_DK, (h + 1) * DN_DK)
        qh = act[:, sl]
        q_ref[:, sl] = qh * (lax.rsqrt(jnp.sum(qh * qh, axis=-1, keepdims=True) + EPS) * (DN_DK ** -0.5))
        kh = act[:, DN_HEADS * DN_DK + h * DN_DK:DN_HEADS * DN_DK + (h + 1) * DN_DK]
        k_ref[:, sl] = kh * lax.rsqrt(jnp.sum(kh * kh, axis=-1, keepdims=True) + EPS)
    v_ref[...] = act[:, 2 * DN_HEADS * DN_DK:CONV_DIM]
    z = p[:, N_Z:N_BA]
    sz_ref[...] = (z * jax.nn.sigmoid(z)).astype(BF16)
    ba = p[:, N_BA:N_END]
    sp_in = ba + dtb_ref[...]
    softplus = jnp.maximum(sp_in, 0.0) + jnp.log1p(jnp.exp(-jnp.abs(sp_in)))
    lane = lax.broadcasted_iota(I32, (1, LANES), 1)
    gb_ref[...] = jnp.where(lane < DN_HEADS, jax.nn.sigmoid(ba), -jnp.exp(alog_ref[...]) * softplus)
    cbuf_ref[...] = xp_ref[tm:tm + pad, :]
    xp_ref[0:pad, :] = xp_ref[tm:tm + pad, :]


def _dn_proj(x, norm_g, w_packed, conv_w, buf, alog_pad, dtb_pad, n_groups, n_seq, tm):
    n = x.shape[0]
    pad = buf.shape[1]
    tiles = n // n_groups // tm
    assert n == n_groups * tiles * tm and pad % SUBLANES == 0 and pad >= (CONV_W - 1) * n_seq
    row = lambda s, i: (s * tiles + i, 0)
    const = lambda s, i: (0, 0)
    big = lambda d: pl.BlockSpec((tm, d), row)
    return pl.pallas_call(
        functools.partial(_dn_proj_kernel, n_seq=n_seq, pad=pad),
        grid=(n_groups, tiles),
        in_specs=[
            big(D_MODEL),
            pl.BlockSpec((1, D_MODEL), const),
            pl.BlockSpec((D_MODEL, N_END), const),
            pl.BlockSpec((CONV_W, CONV_DIM), const),
            pl.BlockSpec((None, pad, CONV_DIM), lambda s, i: (s, 0, 0)),
            pl.BlockSpec((1, LANES), const),
            pl.BlockSpec((1, LANES), const),
        ],
        out_specs=(big(D_MODEL), big(D_MODEL), big(D_MODEL), big(D_MODEL), big(LANES),
                   pl.BlockSpec((None, pad, CONV_DIM), lambda s, i: (s, 0, 0))),
        out_shape=(
            jax.ShapeDtypeStruct((n, D_MODEL), F32),
            jax.ShapeDtypeStruct((n, D_MODEL), F32),
            jax.ShapeDtypeStruct((n, D_MODEL), F32),
            jax.ShapeDtypeStruct((n, D_MODEL), BF16),
            jax.ShapeDtypeStruct((n, LANES), F32),
            jax.ShapeDtypeStruct((n_groups, pad, CONV_DIM), F32),
        ),
        scratch_shapes=[pltpu.VMEM((pad + tm, CONV_DIM), F32)],
        compiler_params=pltpu.CompilerParams(dimension_semantics=("parallel", "arbitrary"),
                                             vmem_limit_bytes=VMEM_LIMIT),
        name="dn_proj",
    )(x, norm_g, w_packed, conv_w, buf, alog_pad, dtb_pad)


def _dn_rec_kernel(q_ref, k_ref, v_ref, gb_ref, s0_ref, sz_ref, og_ref, w_ref, x_ref, y_ref, sout_ref,
                   s_scr, a_scr, *, c_len):
    n_sub = q_ref.shape[0] // c_len

    @pl.when(pl.program_id(1) == 0)
    def _():
        s_scr[...] = s0_ref[...]

    ri = lax.broadcasted_iota(I32, (c_len, c_len), 0)
    ci = lax.broadcasted_iota(I32, (c_len, c_len), 1)
    incl = ri >= ci
    strict = ri > ci
    tril = jnp.where(incl, 1.0, 0.0)
    eye_c = jnp.where(ri == ci, 1.0, 0.0)
    lane = lax.broadcasted_iota(I32, (c_len, LANES), 1)
    n_sq = max(0, (c_len - 1).bit_length() - 1)

    tile_aligned = c_len % LANES == 0

    subs = range(n_sub)
    rows = [slice(s * c_len, (s + 1) * c_len) for s in subs]
    gbc = [gb_ref[rows[s], :] for s in subs]
    gcum = [_dot(tril, gbc[s], HI) for s in subs]
    gcum_t = [gcum[s].T if tile_aligned else None for s in subs]
    units = [(s, h) for s in subs for h in range(DN_HEADS)]
    n_u = range(len(units))
    at = [(rows[s], slice(h * DN_DK, (h + 1) * DN_DK)) for s, h in units]
    beta = [gbc[s][:, h:h + 1] for s, h in units]
    gcol = [gcum[s][:, DN_HEADS + h:DN_HEADS + h + 1] for s, h in units]
    if tile_aligned:
        grow = [gcum_t[s][DN_HEADS + h:DN_HEADS + h + 1, :] for s, h in units]
    else:
        grow = [_dot_nt(jnp.where(lane == DN_HEADS + h, 1.0, 0.0), gcum[s], HI) for s, h in units]
    glast = [gcol[i][c_len - 1:c_len, :] for i in n_u]
    decay = [jnp.where(incl, jnp.exp(jnp.where(incl, gcol[i] - grow[i], 0.0)), 0.0) for i in n_u]
    eg = [jnp.exp(gcol[i]) for i in n_u]
    a_mat = [beta[i] * _dot_nt(k_ref[at[i]], k_ref[at[i]]) * jnp.where(strict, decay[i], 0.0) for i in n_u]
    pw = [-a_mat[i] for i in n_u]
    t_inv = [eye_c - a_mat[i] for i in n_u]
    for level in range(n_sq):
        if level < NEUMANN_SPLIT_LEVELS:
            ps = [_split_bf16(pw[i]) for i in n_u]
            pw = [_dot3(ps[i], ps[i]) for i in n_u]
            ps = [_split_bf16(pw[i]) for i in n_u]
            t_inv = [t_inv[i] + _dot3(_split_bf16(t_inv[i]), ps[i]) for i in n_u]
        else:
            pw = [_dot(pw[i].astype(BF16), pw[i].astype(BF16)) for i in n_u]
            t_inv = [t_inv[i] + _dot(t_inv[i].astype(BF16), pw[i].astype(BF16)) for i in n_u]
    wu = [_dot(t_inv[i].astype(BF16),
               jnp.concatenate([beta[i] * eg[i] * k_ref[at[i]], beta[i] * v_ref[at[i]]], axis=1).astype(BF16))
          for i in n_u]
    qk = [_dot_nt(q_ref[at[i]], k_ref[at[i]]) * decay[i] for i in n_u]
    kd = [k_ref[at[i]] * jnp.exp(glast[i] - gcol[i]) for i in n_u]
    kdt = [kd[i].T if tile_aligned else None for i in n_u]
    for i, (s, h) in enumerate(units):
        s_old = s_scr[h]
        u = wu[i][:, DN_DK:DN_DK + DN_DV] - _dot(wu[i][:, 0:DN_DK], s_old)
        o = eg[i] * _dot(q_ref[at[i]], s_old) + _dot(qk[i], u)
        a_scr[at[i]] = _rms(o, DN_DV) * og_ref[...] * sz_ref[at[i]].astype(F32)
        s_scr[h] = jnp.exp(glast[i]) * s_old + (_dot(kdt[i], u) if tile_aligned else _dot_tn(kd[i], u))
    y_ref[...] = x_ref[...] + _dot(a_scr[...].astype(BF16), w_ref[...])

    @pl.when(pl.program_id(1) == pl.num_programs(1) - 1)
    def _():
        sout_ref[...] = s_scr[...]


def _dn_rec(q, k, v, gb, s0, sz, o_gain, w_out, x, n_seq, t_len, chunk, chunks_per_step):
    step = chunk * chunks_per_step
    assert t_len % step == 0
    n_steps = t_len // step
    row = lambda b, c: (b * n_steps + c, 0)
    const = lambda b, c: (0, 0)
    wide = pl.BlockSpec((step, D_MODEL), row)
    st = pl.BlockSpec((None, None, DN_HEADS, DN_DK, DN_DV), lambda b, c: (b, 0, 0, 0, 0))
    return pl.pallas_call(
        functools.partial(_dn_rec_kernel, c_len=chunk),
        grid=(n_seq, n_steps),
        in_specs=[wide, wide, wide, pl.BlockSpec((step, LANES), row), st, wide,
                  pl.BlockSpec((1, DN_DV), const), pl.BlockSpec((D_MODEL, D_MODEL), const), wide],
        out_specs=(wide, st),
        out_shape=(jax.ShapeDtypeStruct((n_seq * t_len, D_MODEL), F32),
                   jax.ShapeDtypeStruct((n_seq, 1, DN_HEADS, DN_DK, DN_DV), F32)),
        scratch_shapes=[pltpu.VMEM((DN_HEADS, DN_DK, DN_DV), F32), pltpu.VMEM((step, D_MODEL), F32)],
        compiler_params=pltpu.CompilerParams(dimension_semantics=("parallel", "arbitrary"),
                                             vmem_limit_bytes=VMEM_LIMIT),
        name="dn_rec",
    )(q, k, v, gb, s0, sz, o_gain, w_out, x)


def _pad_lanes(v, offset=0):
    return jnp.zeros((1, LANES), F32).at[0, offset:offset + v.shape[0]].set(v)


def _split_w(w, main, tail):
    small = jnp.pad(w[:, main:tail], ((0, 0), (0, LANES - (tail - main))))
    return w[:, :main].astype(BF16), small.astype(BF16), w[:, tail:].astype(BF16)


def kernel(x_prompt, x_sample, cache_k, cache_v, cache_idx_k, state_dn_S, state_dn_conv, page_table,
           att_norm, att_w_in, att_q_gain, att_k_gain, att_ik_gain, att_w_out,
           dn_norm, dn_w_in, dn_conv_w, dn_A_log, dn_dt_bias, dn_o_gain, dn_w_out):
    bp, seq, d = x_prompt.shape
    bs, t_new, _ = x_sample.shape
    n_pool = cache_k.shape[0]
    np_rows, ns_rows = bp * seq, bs * t_new
    tm, tm_dn = 512, 256

    xp = x_prompt.reshape(np_rows, d)
    xs = x_sample.reshape(ns_rows, d)

    w_att = _split_w(att_w_in[0], A_IKW, A_IKW + IDX_DIM + IDX_HEADS)
    a_norm = att_norm[0][None, :]
    qg, kg = att_q_gain[0][None, :], att_k_gain[0][None, :]
    ikg = _pad_lanes(att_ik_gain[0])
    w_ao = att_w_out[0].astype(BF16)

    q_p, k_p, kb_p, v_p, vt_p, iq_p, ikw_p, ikb_p, iwt_p, sg_p = _att_proj(xp, a_norm, *w_att, qg, kg, ikg, tm)
    y1_p = _att_prompt(q_p, kb_p, vt_p, ikb_p, iq_p, iwt_p, sg_p, w_ao, xp, bp, seq)

    q_s, k_s, _, v_s, _, iq_s, ikw_s, ikb_s, _, sg_s = _att_proj(xs, a_norm, *w_att, qg, kg, ikg, min(tm, ns_rows))
    pad_new = lambda a, per_tok: jnp.pad(a.reshape(bs, t_new * per_tok, a.shape[-1]),
                                         ((0, 0), (0, (PAGE_SIZE - t_new) * per_tok), (0, 0)))
    o_s = _att_sample(page_table, q_s.astype(F32), iq_s.astype(F32), ikw_s, pad_new(ikb_s, 1),
                      pad_new(k_s, N_KV_HEADS), pad_new(v_s, N_KV_HEADS),
                      jnp.transpose(cache_idx_k, (0, 2, 3, 1)).reshape(n_pool, IDX_DIM, PAGE_SIZE),
                      cache_k.reshape(n_pool, PAGE_SIZE * N_KV_HEADS, HEAD_DIM),
                      cache_v.reshape(n_pool, PAGE_SIZE * N_KV_HEADS, HEAD_DIM), t_new)
    y1_s = _out_proj(o_s, sg_s, w_ao, xs, min(tm, ns_rows))

    w_dn = jnp.pad(dn_w_in[0], ((0, 0), (0, N_END - dn_w_in.shape[-1]))).astype(BF16)
    d_norm = dn_norm[0][None, :]
    alog = _pad_lanes(dn_A_log[0], DN_HEADS)
    dtb = _pad_lanes(dn_dt_bias[0], DN_HEADS)
    o_gain = dn_o_gain[0][None, :]
    w_do = dn_w_out[0].astype(BF16)
    hist = CONV_W - 1

    buf_p = jnp.zeros((bp, SUBLANES, CONV_DIM), F32)
    dq_p, dk_p, dv_p, sz_p, gb_p, cb_p = _dn_proj(y1_p, d_norm, w_dn, dn_conv_w[0], buf_p, alog, dtb,
                                                   n_groups=bp, n_seq=1, tm=tm_dn)
    s0_p = jnp.zeros((bp, 1, DN_HEADS, DN_DK, DN_DV), F32)
    y2_p, s_p = _dn_rec(dq_p, dk_p, dv_p, gb_p, s0_p, sz_p, o_gain, w_do, y1_p, bp, seq, chunk=LANES,
                        chunks_per_step=4)

    to_tm = lambda a: a.reshape(bs, t_new, -1).transpose(1, 0, 2).reshape(ns_rows, -1)
    to_bm = lambda a: a.reshape(t_new, bs, -1).transpose(1, 0, 2).reshape(ns_rows, -1)
    buf_s = state_dn_conv[:, 0].transpose(1, 0, 2).reshape(1, hist * bs, CONV_DIM)
    dq_s, dk_s, dv_s, sz_s, gb_s, cb_s = _dn_proj(to_tm(y1_s), d_norm, w_dn, dn_conv_w[0], buf_s, alog, dtb,
                                                   n_groups=1, n_seq=bs, tm=ns_rows)
    y2_s, s_s = _dn_rec(to_bm(dq_s), to_bm(dk_s), to_bm(dv_s), to_bm(gb_s), state_dn_S,
                        to_bm(sz_s.astype(F32)), o_gain, w_do, y1_s, bs, t_new, chunk=t_new,
                        chunks_per_step=1)

    return (
        y2_p.reshape(bp, seq, d),
        y2_s.reshape(bs, t_new, d),
        k_p.reshape(bp, seq, 1, N_KV_HEADS, HEAD_DIM),
        v_p.reshape(bp, seq, 1, N_KV_HEADS, HEAD_DIM),
        ikw_p[:, :IDX_DIM].reshape(bp, seq, 1, IDX_DIM),
        k_s.reshape(bs, t_new, 1, N_KV_HEADS, HEAD_DIM),
        v_s.reshape(bs, t_new, 1, N_KV_HEADS, HEAD_DIM),
        ikw_s[:, :IDX_DIM].reshape(bs, t_new, 1, IDX_DIM),
        s_p,
        cb_p[:, SUBLANES - hist:, :].reshape(bp, 1, hist, CONV_DIM),
        s_s,
        cb_s.reshape(hist, bs, CONV_DIM).transpose(1, 0, 2).reshape(bs, 1, hist, CONV_DIM),
    )
```
